```python
import numpy as np
import jax
import jax.numpy as jnp
from jax import lax

D_MODEL = 2048
BATCH = 4
SEQ = 2048
DEPTH = 1
DEC_BATCH = 128
DEC_SEQ = 1
PAST_LEN = 2048
PAGE_SIZE = 128

HEAD_DIM = 128
MOBA_HEADS = 6
MOBA_BLOCK = 256
MOBA_TOPK = 3
NSA_HEADS = 6
NSA_KV_HEADS = 2
NSA_GROUP = NSA_HEADS // NSA_KV_HEADS
NSA_CMP_LEN = 32
NSA_CMP_STRIDE = 16
NSA_CMP_HIDDEN = 128
NSA_SEL_BLOCK = 64
NSA_TOPN = 16
NSA_WINDOW = 512
NSA_BRANCHES = 3
MEM_HEADS = 4
MEM_LEN = 256
N_BRANCH = 3
FFN_DIM = 5632
CONV_WIDTH = 3
Q_BLOCK = 128
DN_ALPHA = (2.0 * DEPTH) ** 0.25
DN_BETA = (8.0 * DEPTH) ** -0.25
LN_EPS = 1e-5
SCALE = HEAD_DIM ** -0.5
MOBA_W = MOBA_HEADS * HEAD_DIM
NSA_W = NSA_HEADS * HEAD_DIM
NSA_KV_W = NSA_KV_HEADS * HEAD_DIM
MEM_W = MEM_HEADS * HEAD_DIM
SPLITS = (MOBA_W, MOBA_W, MOBA_W, NSA_W, NSA_KV_W, NSA_KV_W, NSA_KV_W, NSA_KV_W, NSA_KV_W, NSA_KV_W,
          NSA_HEADS * NSA_BRANCHES, MEM_W, N_BRANCH * D_MODEL)
N_IN = sum(SPLITS)
F32 = jnp.float32

kernel_name = 'hybrid_moba_nsa_memory_convffn_step'


def layer_norm(x, g, b):
    xf = x.astype(F32)
    mu = jnp.mean(xf, -1, keepdims=True)
    var = jnp.mean(jnp.square(xf - mu), -1, keepdims=True)
    return ((xf - mu) * lax.rsqrt(var + LN_EPS) * g + b).astype(x.dtype)


def alibi_slopes(n):
    return jnp.asarray(2.0 ** (-8.0 * np.arange(1, n + 1) / n), F32)


def masked_softmax(s):
    m = jnp.max(s, -1, keepdims=True)
    m = jnp.where(jnp.isfinite(m), m, 0.0)
    p = jnp.exp(s - m)
    return p / jnp.maximum(jnp.sum(p, -1, keepdims=True), 1e-30)


def dense_scores(q, k, slopes=None, dist=None, ok=None):
    s = jnp.einsum('...grqd,...gnd->...grqn', q, k, preferred_element_type=F32) * SCALE
    if slopes is not None:
        s = s - slopes[:, :, None, None] * dist
    if ok is not None:
        s = jnp.where(ok, s, -jnp.inf)
    return s


def gathered_scores(q, k, slopes, dist, ok):
    s = jnp.einsum('...grqd,...gqmd->...grqm', q, k, preferred_element_type=F32) * SCALE
    return jnp.where(ok, s - slopes[:, :, None, None] * dist, -jnp.inf)


def block_sparse_attend(q, tpos, slopes, own_k, own_v, own_pos, own_ok, sel):
    parts = []
    if sel is not None:
        k_s, v_s, pos_s, ok_s = sel
        dist = jnp.expand_dims(tpos[:, None] - pos_s, -3)
        parts.append(gathered_scores(q, k_s, slopes, dist, jnp.expand_dims(ok_s, -3)))
    parts.append(dense_scores(q, own_k, slopes, tpos[:, None] - own_pos[None, :], own_ok))
    p = masked_softmax(jnp.concatenate(parts, -1))
    n_own = own_k.shape[-2]
    n_sel = p.shape[-1] - n_own
    out = jnp.einsum('...grqn,...gnd->...grqd', p[..., n_sel:], own_v, preferred_element_type=F32)
    if sel is not None:
        out = out + jnp.einsum('...grqm,...gqmd->...grqd', p[..., :n_sel], v_s, preferred_element_type=F32)
    return out


def moba_select(q, kmean, tpos, ksel):
    s = jnp.einsum('nhsd,nhjd->nhsj', q, kmean, preferred_element_type=F32)
    n_past = tpos // MOBA_BLOCK
    valid = jnp.arange(kmean.shape[2])[None, :] < n_past[:, None]
    _, idx = lax.top_k(jnp.where(valid, s, -jnp.inf), ksel)
    ok = jnp.arange(ksel)[None, :] < n_past[:, None]
    return idx, ok


def moba_prompt(q, k, v, slopes):
    B, T, H, hd = q.shape
    nb = -(-T // MOBA_BLOCK)
    pad = nb * MOBA_BLOCK - T

    def blocks(a):
        return jnp.pad(a, ((0, 0), (0, pad), (0, 0), (0, 0))).reshape(B, nb, MOBA_BLOCK, H, hd).transpose(0, 3, 1, 2, 4)

    kb, vb = blocks(k), blocks(v)
    qh = q.transpose(0, 2, 1, 3)[:, :, None]
    ksel = min(MOBA_TOPK, nb - 1)
    if ksel > 0:
        kmean = jnp.mean(kb[:, :, :nb - 1].astype(F32), axis=3)
        idx, ok = moba_select(qh[:, :, 0], kmean, jnp.arange(T), ksel)
    nqb = T // Q_BLOCK
    hi = jnp.arange(H)[:, None, None]
    off = jnp.arange(MOBA_BLOCK)

    def step(i):
        b, qb = i // nqb, i % nqb
        q0 = qb * Q_BLOCK
        tq = q0 + jnp.arange(Q_BLOCK)
        qblk = lax.dynamic_slice_in_dim(qh[b], q0, Q_BLOCK, axis=2)
        kb_b, vb_b = kb[b], vb[b]
        own = q0 // MOBA_BLOCK
        own_k = lax.dynamic_index_in_dim(kb_b, own, 1, keepdims=False)
        own_v = lax.dynamic_index_in_dim(vb_b, own, 1, keepdims=False)
        own_pos = own * MOBA_BLOCK + off
        own_ok = own_pos[None, :] <= tq[:, None]
        sel = None
        if ksel > 0:
            ib = lax.dynamic_slice_in_dim(idx[b], q0, Q_BLOCK, axis=1)
            m = ksel * MOBA_BLOCK
            sel = (kb_b[hi, ib].reshape(H, Q_BLOCK, m, hd),
                   vb_b[hi, ib].reshape(H, Q_BLOCK, m, hd),
                   (ib[..., None] * MOBA_BLOCK + off).reshape(H, Q_BLOCK, m),
                   jnp.repeat(lax.dynamic_slice_in_dim(ok, q0, Q_BLOCK, 0), MOBA_BLOCK, axis=1))
        return block_sparse_attend(qblk, tq, slopes, own_k, own_v, own_pos, own_ok, sel)[:, 0]

    out = lax.map(step, jnp.arange(B * nqb))
    return out.reshape(B, nqb, H, Q_BLOCK, hd).transpose(0, 1, 3, 2, 4).reshape(B, T, H, hd)


def moba_sample(q, k_new, v_new, pool_k, pool_v, page_table, slopes):
    N, S, H, hd = q.shape
    ppb = MOBA_BLOCK // PAGE_SIZE
    n_full = PAST_LEN // MOBA_BLOCK
    own_page = n_full * ppb
    n_own_past = PAST_LEN - own_page * PAGE_SIZE
    tpos = PAST_LEN + jnp.arange(S)
    qh = q.transpose(0, 2, 1, 3)[:, :, None]
    k_past = pool_k[page_table]
    own_k = jnp.concatenate([k_past[:, own_page:].reshape(N, n_own_past, H, hd), k_new], 1).transpose(0, 2, 1, 3)
    own_v = jnp.concatenate([pool_v[page_table[:, own_page:]].reshape(N, n_own_past, H, hd), v_new], 1).transpose(0, 2, 1, 3)
    own_pos = own_page * PAGE_SIZE + jnp.arange(n_own_past + S)
    own_ok = own_pos[None, :] <= tpos[:, None]
    sel = None
    ksel = min(MOBA_TOPK, n_full)
    if ksel > 0:
        kmean = jnp.mean(k_past[:, :own_page].reshape(N, n_full, MOBA_BLOCK, H, hd).astype(F32), axis=2).transpose(0, 2, 1, 3)
        idx, ok = moba_select(qh[:, :, 0], kmean, tpos, ksel)
        ni = jnp.arange(N)[:, None, None, None, None]
        hi = jnp.arange(H)[None, :, None, None, None]
        phys = page_table[ni, idx[..., None] * ppb + jnp.arange(ppb)]
        m = ksel * MOBA_BLOCK
        sel = (pool_k[phys, :, hi].reshape(N, H, S, m, hd),
               pool_v[phys, :, hi].reshape(N, H, S, m, hd),
               (idx[..., None] * MOBA_BLOCK + jnp.arange(MOBA_BLOCK)).reshape(N, H, S, m),
               jnp.repeat(ok, MOBA_BLOCK, axis=1))
    out = block_sparse_attend(qh, tpos, slopes, own_k, own_v, own_pos, own_ok, sel)[:, :, 0]
    return out.transpose(0, 2, 1, 3)


def nsa_compress(a, pe, w1, w2):
    N, L, G, hd = a.shape
    r = NSA_CMP_LEN // NSA_CMP_STRIDE
    n_chunk = -(-L // NSA_CMP_STRIDE)
    ap = jnp.pad(a, ((0, 0), (0, n_chunk * NSA_CMP_STRIDE - L), (0, 0), (0, 0))).reshape(N, n_chunk, NSA_CMP_STRIDE, G, hd)
    nc = n_chunk - r + 1
    blk = jnp.concatenate([ap[:, i:i + nc] for i in range(r)], axis=2) + pe[:, None, :]
    flat = blk.transpose(0, 1, 3, 2, 4).reshape(N, nc, G, NSA_CMP_LEN * hd)
    return jax.nn.gelu(flat @ w1) @ w2


def nsa_cmp_attend(qg, tpos, ck, cv, slopes):
    end = jnp.arange(ck.shape[1]) * NSA_CMP_STRIDE + NSA_CMP_LEN - 1
    dist = tpos[:, None] - end[None, :]
    p = masked_softmax(dense_scores(qg, ck.transpose(0, 2, 1, 3), slopes, dist, dist >= 0))
    out = jnp.einsum('...grqn,...gnd->...grqd', p, cv.transpose(0, 2, 1, 3), preferred_element_type=F32)
    return out, jnp.sum(p, axis=2)


def cmp_to_sel_matrix(n_cmp, n_blocks):
    start = np.arange(n_cmp)[:, None] * NSA_CMP_STRIDE
    j0 = np.arange(n_blocks)[None, :] * NSA_SEL_BLOCK
    ov = np.minimum(start + NSA_CMP_LEN, j0 + NSA_SEL_BLOCK) - np.maximum(start, j0)
    return jnp.asarray(np.clip(ov, 0, None) / NSA_CMP_LEN, F32)


def nsa_select(p_grp, tpos, n_blocks, ksel):
    imp = jnp.einsum('ngsi,ij->ngsj', p_grp, cmp_to_sel_matrix(p_grp.shape[-1], n_blocks))
    cur = tpos // NSA_SEL_BLOCK
    j = jnp.arange(n_blocks)[None, :]
    valid = j < cur[:, None]
    forced = valid & ((j == 0) | (j == cur[:, None] - 1))
    imp = jnp.where(forced, jnp.inf, jnp.where(valid, imp, -jnp.inf))
    _, idx = lax.top_k(imp, ksel)
    ok = jnp.arange(ksel)[None, :] < cur[:, None]
    return idx, ok


def nsa_combine(g_nsa, o_cmp, o_sel, o_win):
    N, S, H, _ = g_nsa.shape
    g = jax.nn.sigmoid(g_nsa.astype(F32)).reshape(N, S, NSA_KV_HEADS, NSA_GROUP, NSA_BRANCHES).transpose(0, 2, 3, 1, 4)[..., None]
    o = g[..., 0, :] * o_cmp + g[..., 1, :] * o_sel + g[..., 2, :] * o_win
    return o.transpose(0, 3, 1, 2, 4).reshape(N, S, H * HEAD_DIM)


def nsa_prompt(q, kc, vc, ks, vs, kw, vw, g_nsa, pe, w1, w2, slopes):
    B, T, H, hd = q.shape
    G, R = NSA_KV_HEADS, NSA_GROUP
    qg = q.reshape(B, T, G, R, hd).transpose(0, 2, 3, 1, 4)
    tpos = jnp.arange(T)
    ck = nsa_compress(kc, pe[0], w1[0], w2[0])
    cv = nsa_compress(vc, pe[1], w1[1], w2[1])
    o_cmp, p_grp = nsa_cmp_attend(qg, tpos, ck, cv, slopes)
    nsel = -(-T // NSA_SEL_BLOCK)
    ksel = min(NSA_TOPN - 1, nsel - 1)
    pad = nsel * NSA_SEL_BLOCK - T

    def blocks(a):
        return jnp.pad(a, ((0, 0), (0, pad), (0, 0), (0, 0))).reshape(B, nsel, NSA_SEL_BLOCK, G, hd).transpose(0, 3, 1, 2, 4)

    ksb, vsb = blocks(ks), blocks(vs)
    ksg, vsg = ks.transpose(0, 2, 1, 3), vs.transpose(0, 2, 1, 3)
    if ksel > 0:
        idx, ok = nsa_select(p_grp, tpos, nsel - 1, ksel)
    nqb = T // Q_BLOCK
    gi = jnp.arange(G)[:, None, None]
    off = jnp.arange(NSA_SEL_BLOCK)

    def step(i):
        b, qb = i // nqb, i % nqb
        q0 = qb * Q_BLOCK
        tq = q0 + jnp.arange(Q_BLOCK)
        qblk = lax.dynamic_slice_in_dim(qg[b], q0, Q_BLOCK, axis=2)
        own_k = lax.dynamic_slice_in_dim(ksg[b], q0, Q_BLOCK, axis=1)
        own_v = lax.dynamic_slice_in_dim(vsg[b], q0, Q_BLOCK, axis=1)
        own_ok = (tq[None, :] <= tq[:, None]) & (tq[None, :] // NSA_SEL_BLOCK == tq[:, None] // NSA_SEL_BLOCK)
        sel = None
        if ksel > 0:
            ib = lax.dynamic_slice_in_dim(idx[b], q0, Q_BLOCK, axis=1)
            m = ksel * NSA_SEL_BLOCK
            sel = (ksb[b][gi, ib].reshape(G, Q_BLOCK, m, hd),
                   vsb[b][gi, ib].reshape(G, Q_BLOCK, m, hd),
                   (ib[..., None] * NSA_SEL_BLOCK + off).reshape(G, Q_BLOCK, m),
                   jnp.repeat(lax.dynamic_slice_in_dim(ok, q0, Q_BLOCK, 0), NSA_SEL_BLOCK, axis=1))
        return block_sparse_attend(qblk, tq, slopes, own_k, own_v, tq, own_ok, sel)

    o_sel = lax.map(step, jnp.arange(B * nqb)).reshape(B, nqb, G, R, Q_BLOCK, hd).transpose(0, 2, 3, 1, 4, 5).reshape(B, G, R, T, hd)
    nwb = NSA_WINDOW // Q_BLOCK
    band = np.arange(nqb)[:, None] + np.arange(nwb + 1)[None, :]

    def banded(a):
        ap = jnp.pad(a, ((0, 0), (NSA_WINDOW, 0), (0, 0), (0, 0))).reshape(B, nqb + nwb, Q_BLOCK, G, hd)
        return ap[:, band].reshape(B, nqb, (nwb + 1) * Q_BLOCK, G, hd).transpose(0, 1, 3, 2, 4)

    kband, vband = banded(kw), banded(vw)
    kpos = (band[:, :, None] * Q_BLOCK + np.arange(Q_BLOCK) - NSA_WINDOW).reshape(nqb, -1)
    tq_all = np.arange(T).reshape(nqb, Q_BLOCK)
    dist = tq_all[:, :, None] - kpos[:, None, :]
    okw = (kpos[:, None, :] >= 0) & (dist >= 0) & (dist <= NSA_WINDOW)
    qw = qg.reshape(B, G, R, nqb, Q_BLOCK, hd).transpose(0, 3, 1, 2, 4, 5)
    s = dense_scores(qw, kband, slopes, jnp.asarray(dist[:, None, None], F32), jnp.asarray(okw[:, None, None]))
    o_win = jnp.einsum('...grqn,...gnd->...grqd', masked_softmax(s), vband, preferred_element_type=F32)
    o_win = o_win.transpose(0, 2, 3, 1, 4, 5).reshape(B, G, R, T, hd)
    return nsa_combine(g_nsa, o_cmp, o_sel, o_win)


def nsa_sample(q, kc, vc, ks, vs, kw, vw, g_nsa, pool_ck, pool_cv, pool_sk, pool_sv, win_k, win_v, page_table, pe, w1, w2, slopes):
    N, S, H, hd = q.shape
    G, R = NSA_KV_HEADS, NSA_GROUP
    assert S <= NSA_SEL_BLOCK and PAST_LEN % PAGE_SIZE == 0
    qg = q.reshape(N, S, G, R, hd).transpose(0, 2, 3, 1, 4)
    tpos = PAST_LEN + jnp.arange(S)

    def past_rows(pool):
        return pool[page_table].reshape(N, PAST_LEN, G, hd)

    ck = nsa_compress(jnp.concatenate([past_rows(pool_ck), kc], 1), pe[0], w1[0], w2[0])
    cv = nsa_compress(jnp.concatenate([past_rows(pool_cv), vc], 1), pe[1], w1[1], w2[1])
    o_cmp, p_grp = nsa_cmp_attend(qg, tpos, ck, cv, slopes)
    n_cand = PAST_LEN // NSA_SEL_BLOCK
    ksel = min(NSA_TOPN - 1, n_cand)
    own_k, own_v = ks.transpose(0, 2, 1, 3), vs.transpose(0, 2, 1, 3)
    own_ok = (tpos[None, :] <= tpos[:, None]) & (tpos[None, :] // NSA_SEL_BLOCK == tpos[:, None] // NSA_SEL_BLOCK)
    sel = None
    if ksel > 0:
        idx, ok = nsa_select(p_grp, tpos, n_cand, ksel)
        spp = PAGE_SIZE // NSA_SEL_BLOCK
        phys = page_table[jnp.arange(N)[:, None, None, None], idx // spp][..., None]
        rows = (idx % spp)[..., None] * NSA_SEL_BLOCK + jnp.arange(NSA_SEL_BLOCK)
        gi = jnp.arange(G)[None, :, None, None, None]
        m = ksel * NSA_SEL_BLOCK
        sel = (pool_sk[phys, rows, gi].reshape(N, G, S, m, hd),
               pool_sv[phys, rows, gi].reshape(N, G, S, m, hd),
               (idx[..., None] * NSA_SEL_BLOCK + jnp.arange(NSA_SEL_BLOCK)).reshape(N, G, S, m),
               jnp.repeat(ok, NSA_SEL_BLOCK, axis=1))
    o_sel = block_sparse_attend(qg, tpos, slopes, own_k, own_v, tpos, own_ok, sel)
    keys = jnp.concatenate([win_k, kw], 1)
    vals = jnp.concatenate([win_v, vw], 1)
    wl = win_k.shape[1]
    kpos = PAST_LEN - wl + jnp.arange(wl + S)
    dist = tpos[:, None] - kpos[None, :]
    s = dense_scores(qg, keys.transpose(0, 2, 1, 3), slopes, dist, (dist >= 0) & (dist <= NSA_WINDOW))
    o_win = jnp.einsum('...grqn,...gnd->...grqd', masked_softmax(s), vals.transpose(0, 2, 1, 3), preferred_element_type=F32)
    return nsa_combine(g_nsa, o_cmp, o_sel, o_win), keys[:, S:], vals[:, S:]


def memory_attend(q, mem_k, mem_v):
    qh = q.transpose(0, 2, 1, 3)[:, :, None]
    p = jax.nn.softmax(dense_scores(qh, mem_k.transpose(0, 2, 1, 3)), axis=-1)
    o = jnp.einsum('...grqn,...gnd->...grqd', p, mem_v.transpose(0, 2, 1, 3), preferred_element_type=F32)
    return o[:, :, 0].transpose(0, 2, 1, 3)


def project(x, w_in, b_in):
    N, S, _ = x.shape
    z = x @ w_in + b_in
    pts = [int(c) for c in np.cumsum(SPLITS)[:-1]]
    q_a, k_a, v_a, q_b, kc, vc, ks, vs, kw, vw, g_nsa, q_m, g_br = jnp.split(z, pts, axis=-1)

    def heads(a):
        return a.reshape(N, S, -1, HEAD_DIM)

    return (heads(q_a), heads(k_a), heads(v_a), heads(q_b), heads(kc), heads(vc), heads(ks), heads(vs),
            heads(kw), heads(vw), g_nsa.reshape(N, S, NSA_HEADS, NSA_BRANCHES), heads(q_m), g_br)


def merge_and_ffn(x, o_a, o_b, o_m, g_br, lw, conv_state):
    N, S, _ = x.shape
    g = jax.nn.sigmoid(g_br.astype(F32)).reshape(N, S, N_BRANCH, D_MODEL).astype(x.dtype)

    def flat(o):
        return o.reshape(N, S, -1).astype(x.dtype)

    merged = (g[:, :, 0] * (flat(o_a) @ lw['p_moba']) + g[:, :, 1] * (flat(o_b) @ lw['p_nsa'])
              + g[:, :, 2] * (flat(o_m) @ lw['p_mem']))
    h = layer_norm(DN_ALPHA * x + merged @ lw['w_o'], lw['ln1_g'], lw['ln1_b'])
    u, gate = jnp.split(h @ lw['w_up'], 2, axis=-1)
    u_ext = jnp.concatenate([conv_state.astype(u.dtype), u], axis=1)
    uc = lw['conv_b'] + sum(lw['conv_w'][k] * u_ext[:, k:k + S] for k in range(CONV_WIDTH))
    f = (jax.nn.gelu(uc) * gate) @ lw['w_down']
    y = layer_norm(DN_ALPHA * h + f, lw['ln2_g'], lw['ln2_b'])
    return y, u_ext[:, S:]


def prompt_layer(x, mem, lw, slopes_a, slopes_b):
    B, T, _ = x.shape
    q_a, k_a, v_a, q_b, kc, vc, ks, vs, kw, vw, g_nsa, q_m, g_br = project(x, lw['w_in'], lw['b_in'])
    kv_m = (mem @ lw['w_mem_kv']).reshape(B, MEM_LEN, 2, MEM_HEADS, HEAD_DIM)
    mem_k, mem_v = kv_m[:, :, 0], kv_m[:, :, 1]
    o_a = moba_prompt(q_a, k_a, v_a, slopes_a)
    o_b = nsa_prompt(q_b, kc, vc, ks, vs, kw, vw, g_nsa, lw['cmp_pe'], lw['cmp_w1'], lw['cmp_w2'], slopes_b)
    o_m = memory_attend(q_m, mem_k, mem_v)
    conv0 = jnp.zeros((B, CONV_WIDTH - 1, FFN_DIM), x.dtype)
    y, conv = merge_and_ffn(x, o_a, o_b, o_m, g_br, lw, conv0)

    def pages(a):
        return a.reshape(B, T // PAGE_SIZE, PAGE_SIZE, a.shape[2], HEAD_DIM)

    wl = min(NSA_WINDOW, T)
    return y, (mem_k, mem_v, pages(k_a), pages(v_a), pages(kc), pages(vc), pages(ks), pages(vs),
               kw[:, T - wl:], vw[:, T - wl:], conv)


def sample_layer(x, mem_k, mem_v, pk_a, pv_a, pck, pcv, psk, psv, win_k, win_v, conv, page_table, lw, slopes_a, slopes_b):
    q_a, k_a, v_a, q_b, kc, vc, ks, vs, kw, vw, g_nsa, q_m, g_br = project(x, lw['w_in'], lw['b_in'])
    o_a = moba_sample(q_a, k_a, v_a, pk_a, pv_a, page_table, slopes_a)
    o_b, new_wk, new_wv = nsa_sample(q_b, kc, vc, ks, vs, kw, vw, g_nsa, pck, pcv, psk, psv, win_k, win_v, page_table,
                                     lw['cmp_pe'], lw['cmp_w1'], lw['cmp_w2'], slopes_b)
    o_m = memory_attend(q_m, mem_k, mem_v)
    y, new_conv = merge_and_ffn(x, o_a, o_b, o_m, g_br, lw, conv)
    return y, (k_a, v_a, kc, vc, ks, vs, new_wk, new_wv, new_conv)


def setup_inputs(seed: int = 0) -> dict:
    key = jax.random.key(seed)
    keys = iter(jax.random.split(key, 40))

    def nrm(shape, scale=1.0):
        return jax.random.normal(next(keys), shape, F32) * scale

    n_pages = PAST_LEN // PAGE_SIZE
    n_phys = (DEC_BATCH * n_pages * 5) // 4
    wl = min(NSA_WINDOW, PAST_LEN)
    G = NSA_KV_HEADS
    d_in = D_MODEL ** -0.5
    page_table = jax.random.permutation(next(keys), n_phys)[:DEC_BATCH * n_pages].reshape(DEC_BATCH, n_pages).astype(jnp.int32)
    return {
        'x_prompt': nrm((BATCH, SEQ, D_MODEL)),
        'x_sample': nrm((DEC_BATCH, DEC_SEQ, D_MODEL)),
        'cache_mem_k': nrm((DEPTH, DEC_BATCH, MEM_LEN, MEM_HEADS, HEAD_DIM)),
        'cache_mem_v': nrm((DEPTH, DEC_BATCH, MEM_LEN, MEM_HEADS, HEAD_DIM)),
        'cache_moba_k': nrm((DEPTH, n_phys, PAGE_SIZE, MOBA_HEADS, HEAD_DIM)),
        'cache_moba_v': nrm((DEPTH, n_phys, PAGE_SIZE, MOBA_HEADS, HEAD_DIM)),
        'cache_nsa_cmp_k': nrm((DEPTH, n_phys, PAGE_SIZE, G, HEAD_DIM)),
        'cache_nsa_cmp_v': nrm((DEPTH, n_phys, PAGE_SIZE, G, HEAD_DIM)),
        'cache_nsa_sel_k': nrm((DEPTH, n_phys, PAGE_SIZE, G, HEAD_DIM)),
        'cache_nsa_sel_v': nrm((DEPTH, n_phys, PAGE_SIZE, G, HEAD_DIM)),
        'cache_nsa_win_k': nrm((DEPTH, DEC_BATCH, wl, G, HEAD_DIM)),
        'cache_nsa_win_v': nrm((DEPTH, DEC_BATCH, wl, G, HEAD_DIM)),
        'cache_ffn_conv': nrm((DEPTH, DEC_BATCH, CONV_WIDTH - 1, FFN_DIM)),
        'page_table': page_table,
        'mem_prompt': nrm((BATCH, MEM_LEN, D_MODEL)),
        'w_in': nrm((DEPTH, D_MODEL, N_IN), d_in),
        'b_in': nrm((DEPTH, N_IN), 0.01),
        'w_mem_kv': nrm((DEPTH, D_MODEL, 2 * MEM_W), d_in),
        'cmp_pe': nrm((DEPTH, 2, NSA_CMP_LEN, HEAD_DIM), 0.1),
        'cmp_w1': nrm((DEPTH, 2, NSA_CMP_LEN * HEAD_DIM, NSA_CMP_HIDDEN), (NSA_CMP_LEN * HEAD_DIM) ** -0.5),
        'cmp_w2': nrm((DEPTH, 2, NSA_CMP_HIDDEN, HEAD_DIM), NSA_CMP_HIDDEN ** -0.5),
        'p_moba': nrm((DEPTH, MOBA_W, D_MODEL), DN_BETA * MOBA_W ** -0.5),
        'p_nsa': nrm((DEPTH, NSA_W, D_MODEL), DN_BETA * NSA_W ** -0.5),
        'p_mem': nrm((DEPTH, MEM_W, D_MODEL), DN_BETA * MEM_W ** -0.5),
        'w_o': nrm((DEPTH, D_MODEL, D_MODEL), DN_BETA * d_in),
        'ln1_g': 1.0 + nrm((DEPTH, D_MODEL), 0.01),
        'ln1_b': nrm((DEPTH, D_MODEL), 0.01),
        'w_up': nrm((DEPTH, D_MODEL, 2 * FFN_DIM), d_in),
        'conv_w': nrm((DEPTH, CONV_WIDTH, FFN_DIM), CONV_WIDTH ** -0.5),
        'conv_b': nrm((DEPTH, FFN_DIM), 0.01),
        'w_down': nrm((DEPTH, FFN_DIM, D_MODEL), DN_BETA * FFN_DIM ** -0.5),
        'ln2_g': 1.0 + nrm((DEPTH, D_MODEL), 0.01),
        'ln2_b': nrm((DEPTH, D_MODEL), 0.01),
    }


def reference(x_prompt, x_sample, cache_mem_k, cache_mem_v, cache_moba_k, cache_moba_v, cache_nsa_cmp_k, cache_nsa_cmp_v,
              cache_nsa_sel_k, cache_nsa_sel_v, cache_nsa_win_k, cache_nsa_win_v, cache_ffn_conv, page_table, mem_prompt,
              w_in, b_in, w_mem_kv, cmp_pe, cmp_w1, cmp_w2, p_moba, p_nsa, p_mem, w_o, ln1_g, ln1_b,
              w_up, conv_w, conv_b, w_down, ln2_g, ln2_b):
    slopes_a = alibi_slopes(MOBA_HEADS).reshape(MOBA_HEADS, 1)
    slopes_b = alibi_slopes(NSA_HEADS).reshape(NSA_KV_HEADS, NSA_GROUP)
    hp, hs = x_prompt, x_sample
    new_p, new_s = [], []
    for l in range(DEPTH):
        lw = {'w_in': w_in[l], 'b_in': b_in[l], 'w_mem_kv': w_mem_kv[l], 'cmp_pe': cmp_pe[l], 'cmp_w1': cmp_w1[l],
              'cmp_w2': cmp_w2[l], 'p_moba': p_moba[l], 'p_nsa': p_nsa[l], 'p_mem': p_mem[l], 'w_o': w_o[l],
              'ln1_g': ln1_g[l], 'ln1_b': ln1_b[l], 'w_up': w_up[l], 'conv_w': conv_w[l], 'conv_b': conv_b[l],
              'w_down': w_down[l], 'ln2_g': ln2_g[l], 'ln2_b': ln2_b[l]}
        hp, st_p = prompt_layer(hp, mem_prompt, lw, slopes_a, slopes_b)
        hs, st_s = sample_layer(hs, cache_mem_k[l], cache_mem_v[l], cache_moba_k[l], cache_moba_v[l],
                                cache_nsa_cmp_k[l], cache_nsa_cmp_v[l], cache_nsa_sel_k[l], cache_nsa_sel_v[l],
                                cache_nsa_win_k[l], cache_nsa_win_v[l], cache_ffn_conv[l], page_table, lw,
                                slopes_a, slopes_b)
        new_p.append(st_p)
        new_s.append(st_s)
    (mem_k_p, mem_v_p, moba_k_p, moba_v_p, cmp_k_p, cmp_v_p, sel_k_p, sel_v_p, win_k_p, win_v_p,
     conv_p) = [jnp.stack([st[i] for st in new_p]) for i in range(11)]
    (moba_k_s, moba_v_s, cmp_k_s, cmp_v_s, sel_k_s, sel_v_s, win_k_s, win_v_s,
     conv_s) = [jnp.stack([st[i] for st in new_s]) for i in range(9)]
    return (hp, hs, mem_k_p, mem_v_p, moba_k_p, moba_v_p, cmp_k_p, cmp_v_p, sel_k_p, sel_v_p, win_k_p, win_v_p, conv_p,
            moba_k_s, moba_v_s, cmp_k_s, cmp_v_s, sel_k_s, sel_v_s, win_k_s, win_v_s, conv_s)
```

```python
import functools

import numpy as np
import jax
import jax.numpy as jnp
from jax import lax
from jax.experimental import pallas as pl
from jax.experimental.pallas import tpu as pltpu

D_MODEL = 2048
BATCH = 4
SEQ = 2048
DEC_BATCH = 128
PAST_LEN = 2048
PAGE = 128
N_PAGES = PAST_LEN // PAGE
HD = 128
MOBA_HEADS = 6
MOBA_BLOCK = 256
MOBA_TOPK = 3
NSA_HEADS = 6
NSA_G = 2
NSA_R = 3
CMP_LEN = 32
CMP_STRIDE = 16
SEL_BLOCK = 64
NSA_TOPN = 16
WINDOW = 512
MEM_HEADS = 4
MEM_LEN = 256
FFN_DIM = 5632
DN_ALPHA = 2.0 ** 0.25
LN_EPS = 1e-5
SCALE = HD ** -0.5
MOBA_W = MOBA_HEADS * HD
NSA_W = NSA_HEADS * HD
KV_W = NSA_G * HD
MEM_W = MEM_HEADS * HD
N_GATE = NSA_HEADS * 3

F32 = jnp.float32
BF = jnp.bfloat16
NEG_BIG = -(2.0 ** 100)
VMEM_LIMIT = 48 * 1024 * 1024
TQ = 256


def _dot(a, b):
    return jnp.dot(a, b, preferred_element_type=F32)


def _dot_nt(a, b):
    return lax.dot_general(a, b, (((1,), (1,)), ((), ())), preferred_element_type=F32)


def _split_bf16(a):
    hi = a.astype(BF)
    lo = (a - hi.astype(F32)).astype(BF)
    return hi, lo


def _alibi_slopes(n):
    return jnp.asarray(2.0 ** (-8.0 * np.arange(1, n + 1) / n), F32)


def _gelu_tanh(x):
    return 0.5 * x * (1.0 + jnp.tanh(np.sqrt(2.0 / np.pi).astype(np.float32) * (x + 0.044715 * (x * x * x))))


def _params(n_axes):
    return pltpu.CompilerParams(dimension_semantics=("arbitrary",) * n_axes, vmem_limit_bytes=VMEM_LIMIT)


def _proj_kernel(x_ref, w_ref, b_ref, *out_refs, kinds):
    j = pl.program_id(1)
    acc = _dot(x_ref[...], w_ref[...]) + b_ref[...]
    for idx, (kind, o_ref) in enumerate(zip(kinds, out_refs)):
        @pl.when(j == idx)
        def _(kind=kind, o_ref=o_ref):
            if kind == "flat":
                o_ref[...] = acc
            else:
                for p in range(o_ref.shape[1]):
                    for h in range(o_ref.shape[2]):
                        o_ref[0, p, h] = acc[p * PAGE:(p + 1) * PAGE, h * HD:(h + 1) * HD]


def _proj(x_bf, w_bf, b, tn, kinds, tm, name):
    rows, k = x_bf.shape
    n_out = len(kinds)
    assert w_bf.shape == (k, n_out * tn) and rows % tm == 0
    tiles_per_batch = SEQ // tm if tm <= SEQ else 1
    out_shape, out_specs = [], []
    for idx, kind in enumerate(kinds):
        if kind == "flat":
            out_shape.append(jax.ShapeDtypeStruct((rows, tn), F32))
            out_specs.append(pl.BlockSpec((tm, tn), lambda i, j: (i, 0)))
        else:
            heads = tn // HD
            out_shape.append(jax.ShapeDtypeStruct((rows // SEQ, SEQ // PAGE, heads, PAGE, HD), F32))
            out_specs.append(pl.BlockSpec((1, tm // PAGE, heads, PAGE, HD),
                                          lambda i, j: (i // tiles_per_batch, i % tiles_per_batch, 0, 0, 0)))
    return pl.pallas_call(
        functools.partial(_proj_kernel, kinds=tuple(kinds)),
        grid=(rows // tm, n_out),
        in_specs=[pl.BlockSpec((tm, k), lambda i, j: (i, 0)),
                  pl.BlockSpec((k, tn), lambda i, j: (0, j)),
                  pl.BlockSpec((1, tn), lambda i, j: (0, j))],
        out_specs=out_specs,
        out_shape=out_shape,
        compiler_params=_params(2),
        name=name,
    )(x_bf, w_bf, b)


def _matmul_kernel(x_ref, w_ref, b_ref, o_ref):
    o_ref[...] = _dot(x_ref[...], w_ref[...]) + b_ref[...]


def _matmul(x_bf, w_bf, b, tm, tn, name):
    rows, k = x_bf.shape
    n = w_bf.shape[1]
    return pl.pallas_call(
        _matmul_kernel,
        grid=(rows // tm, n // tn),
        in_specs=[pl.BlockSpec((tm, k), lambda i, j: (i, 0)),
                  pl.BlockSpec((k, tn), lambda i, j: (0, j)),
                  pl.BlockSpec((1, tn), lambda i, j: (0, j))],
        out_specs=pl.BlockSpec((tm, tn), lambda i, j: (i, j)),
        out_shape=jax.ShapeDtypeStruct((rows, n), F32),
        compiler_params=_params(2),
        name=name,
    )(x_bf, w_bf, b)


def _topk_member(v, lane, n_cand, k):
    member = jnp.zeros(v.shape, F32)
    for j in range(n_cand):
        vj = v[:, j:j + 1]
        beats = (v > vj) | ((v == vj) & (lane < j))
        cnt = jnp.sum(beats.astype(F32), axis=1, keepdims=True)
        member = jnp.where((lane == j) & (cnt < k), 1.0, member)
    return member


def _softmax_start(s, v_bf):
    m = jnp.max(s, axis=1, keepdims=True)
    p = jnp.exp(s - m)
    return m, jnp.sum(p, axis=1, keepdims=True), _dot(p.astype(BF), v_bf)


def _softmax_step(state, s, v_bf):
    m_old, l_old, acc_old = state
    m_new = jnp.maximum(m_old, jnp.max(s, axis=1, keepdims=True))
    alpha = jnp.exp(m_old - m_new)
    p = jnp.exp(s - m_new)
    return (m_new, alpha * l_old + jnp.sum(p, axis=1, keepdims=True),
            alpha * acc_old + _dot(p.astype(BF), v_bf))


def _moba_prompt_kernel(slopes_ref, q_ref, k_ref, v_ref, o_ref, kb_ref, vb_ref, kmh_ref, kml_ref,
                        m_ref, l_ref, acc_ref):
    h = pl.program_id(1)
    qi = pl.program_id(2)
    nb = SEQ // MOBA_BLOCK

    @pl.when(qi == 0)
    def _():
        k = k_ref[...].reshape(SEQ, HD)
        kb_ref[...] = k.astype(BF)
        vb_ref[...] = v_ref[...].reshape(SEQ, HD).astype(BF)
        km = jnp.sum(k.reshape(nb, MOBA_BLOCK, HD), axis=1) / MOBA_BLOCK
        km = jnp.concatenate([km, jnp.zeros((HD - nb, HD), F32)], axis=0)
        hi, lo = _split_bf16(km)
        kmh_ref[...] = hi
        kml_ref[...] = lo

    slope = slopes_ref[h]
    q = q_ref[...]
    qh, ql = _split_bf16(q)
    lane = lax.broadcasted_iota(jnp.int32, (TQ, HD), 1)
    sc = _dot_nt(qh, kmh_ref[...]) + _dot_nt(qh, kml_ref[...]) + _dot_nt(ql, kmh_ref[...])
    sc = jnp.where(lane < qi, sc, -jnp.inf)
    member = _topk_member(sc, lane, nb - 1, MOBA_TOPK)

    rr = lax.broadcasted_iota(jnp.int32, (TQ, MOBA_BLOCK), 0)
    cc = lax.broadcasted_iota(jnp.int32, (TQ, MOBA_BLOCK), 1)
    rel = (rr - cc).astype(F32)

    own = pl.multiple_of(qi * MOBA_BLOCK, MOBA_BLOCK)
    s = _dot_nt(qh, kb_ref[pl.ds(own, MOBA_BLOCK), :]) * SCALE - slope * rel
    s = jnp.where(cc <= rr, s, -jnp.inf)
    m, l, acc = _softmax_start(s, vb_ref[pl.ds(own, MOBA_BLOCK), :])
    m_ref[...] = m
    l_ref[...] = l
    acc_ref[...] = acc

    for jb in range(nb - 1):
        @pl.when(jb < qi)
        def _(jb=jb):
            base = ((qi - jb) * MOBA_BLOCK).astype(F32)
            s = _dot_nt(qh, kb_ref[jb * MOBA_BLOCK:(jb + 1) * MOBA_BLOCK, :]) * SCALE - slope * (rel + base)
            s = jnp.where(member[:, jb:jb + 1] > 0.5, s, -jnp.inf)
            m, l, acc = _softmax_step((m_ref[...], l_ref[...], acc_ref[...]), s,
                                      vb_ref[jb * MOBA_BLOCK:(jb + 1) * MOBA_BLOCK, :])
            m_ref[...] = m
            l_ref[...] = l
            acc_ref[...] = acc

    o_ref[...] = (acc_ref[...] / l_ref[...]).astype(BF)


def _moba_prompt(slopes, q, k_pages, v_pages):
    nq = SEQ // TQ
    kv_spec = pl.BlockSpec((None, SEQ // PAGE, None, PAGE, HD), lambda b, h, qi: (b, 0, h, 0, 0))
    return pl.pallas_call(
        _moba_prompt_kernel,
        grid=(BATCH, MOBA_HEADS, nq),
        in_specs=[pl.BlockSpec(memory_space=pltpu.SMEM),
                  pl.BlockSpec((TQ, HD), lambda b, h, qi: (b * nq + qi, h)),
                  kv_spec, kv_spec],
        out_specs=pl.BlockSpec((TQ, HD), lambda b, h, qi: (b * nq + qi, h)),
        out_shape=jax.ShapeDtypeStruct((BATCH * SEQ, MOBA_W), BF),
        scratch_shapes=[pltpu.VMEM((SEQ, HD), BF), pltpu.VMEM((SEQ, HD), BF),
                        pltpu.VMEM((HD, HD), BF), pltpu.VMEM((HD, HD), BF),
                        pltpu.VMEM((TQ, 1), F32), pltpu.VMEM((TQ, 1), F32), pltpu.VMEM((TQ, HD), F32)],
        compiler_params=_params(3),
        name="moba_prompt",
    )(slopes, q, k_pages, v_pages)


N_CHUNK = SEQ // CMP_STRIDE


def _compress_tokens(get_x, w1_ref, w2_ref, c_row, b_last):
    acc = jnp.zeros((N_CHUNK, 2 * HD), F32)
    for r in range(CMP_STRIDE):
        acc = acc + _dot(get_x(r).astype(BF), w1_ref[r])
    top, bot = acc[:, :HD], acc[:, HD:]
    nxt = pltpu.roll(bot, shift=N_CHUNK - 1, axis=0)
    row = lax.broadcasted_iota(jnp.int32, (N_CHUNK, HD), 0)
    pre = top + jnp.where(row == N_CHUNK - 1, b_last, nxt) + c_row
    return _dot(_gelu_tanh(pre).astype(BF), w2_ref[...])


def _compress_prompt_kernel(kc_ref, vc_ref, pe_ref, w1_ref, w2_ref, ck_ref, cv_ref):
    for which, (src, dst) in enumerate(((kc_ref, ck_ref), (vc_ref, cv_ref))):
        w1 = w1_ref.at[which]
        c_row = _pe_term(pe_ref, w1, which)
        get_x = lambda r, src=src: src[pl.ds(r, N_CHUNK, stride=CMP_STRIDE), :]
        dst[0, 0] = _compress_tokens(get_x, w1, w2_ref.at[which], c_row, jnp.zeros((1, HD), F32))


def _pe_term(pe_ref, w1, which):
    c_row = jnp.zeros((1, HD), F32)
    for r in range(CMP_LEN):
        half, rr = divmod(r, CMP_STRIDE)
        c_row = c_row + _dot(pe_ref[which, :, r * HD:(r + 1) * HD], w1[rr, :, half * HD:(half + 1) * HD])
    return c_row


def _pack_cmp_weights(cmp_pe, cmp_w1, cmp_w2):
    w1 = cmp_w1.reshape(2, 2, CMP_STRIDE, HD, HD).transpose(0, 2, 3, 1, 4).reshape(2, CMP_STRIDE, HD, 2 * HD)
    return cmp_pe.reshape(2, 1, CMP_LEN * HD).astype(BF), w1.astype(BF), cmp_w2.astype(BF)


def _compress_prompt(kc, vc, pe_bf, w1_bf, w2_bf):
    full = lambda a: pl.BlockSpec(a.shape, lambda b, g: (0,) * a.ndim)
    return pl.pallas_call(
        _compress_prompt_kernel,
        grid=(BATCH, NSA_G),
        in_specs=[pl.BlockSpec((SEQ, HD), lambda b, g: (b, g)), pl.BlockSpec((SEQ, HD), lambda b, g: (b, g)),
                  full(pe_bf), full(w1_bf), full(w2_bf)],
        out_specs=[pl.BlockSpec((1, 1, N_CHUNK, HD), lambda b, g: (b, g, 0, 0))] * 2,
        out_shape=[jax.ShapeDtypeStruct((BATCH, NSA_G, N_CHUNK, HD), F32)] * 2,
        compiler_params=_params(2),
        name="nsa_compress_prompt",
    )(kc, vc, pe_bf, w1_bf, w2_bf)


def _cmp_to_sel_matrix(n_cmp, n_blocks):
    start = np.arange(n_cmp)[:, None] * CMP_STRIDE
    j0 = np.arange(n_blocks)[None, :] * SEL_BLOCK
    ov = np.minimum(start + CMP_LEN, j0 + SEL_BLOCK) - np.maximum(start, j0)
    m = np.zeros((HD, HD), np.float32)
    m[:n_cmp, :n_blocks] = np.clip(ov, 0, None) / CMP_LEN
    return jnp.asarray(m, BF)


def _cmp_attend(q, ck_hi, ck_lo, cv_bf, slope, dist):
    qh, ql = _split_bf16(q)
    s = (_dot_nt(qh, ck_hi) + _dot_nt(qh, ck_lo) + _dot_nt(ql, ck_hi)) * SCALE - slope * dist.astype(F32)
    s = jnp.where(dist >= 0, s, -jnp.inf)
    m = jnp.max(s, axis=1, keepdims=True)
    m = jnp.where(m > -jnp.inf, m, 0.0)
    p = jnp.exp(s - m)
    p = p / jnp.maximum(jnp.sum(p, axis=1, keepdims=True), 1e-30)
    return _dot(p.astype(BF), cv_bf), p


def _select_blocks(p_grp, cmat_bf, cur, lane, n_cand, k):
    ph, plo = _split_bf16(p_grp)
    imp = _dot(ph, cmat_bf) + _dot(plo, cmat_bf)
    valid = (lane < cur) & (lane < n_cand)
    forced = valid & ((lane == 0) | (lane == cur - 1))
    v = jnp.where(forced, jnp.inf, jnp.where(valid, imp, -jnp.inf))
    return jnp.where(valid, _topk_member(v, lane, n_cand, k), 0.0)


def _nsa_prompt_kernel(slopes_ref, q_ref, gn_ref, ks_ref, vs_ref, kw_ref, vw_ref, ck_ref, cv_ref, cmat_ref,
                       o_ref, kx_ref, vsb_ref, kwb_ref, vwb_ref, st_ref):
    g = pl.program_id(1)
    qi = pl.program_id(2)
    chunk = TQ

    @pl.when(qi == 0)
    def _():
        key_blk = lax.broadcasted_iota(jnp.int32, (SEQ, HD), 0) // SEL_BLOCK
        col = lax.broadcasted_iota(jnp.int32, (SEQ, HD), 1)
        kx_ref[:, :HD] = ks_ref[...].astype(BF)
        kx_ref[:, HD:] = jnp.where(key_blk == col, 1.0, 0.0).astype(BF)
        vsb_ref[...] = vs_ref[...].astype(BF)
        kwb_ref[...] = kw_ref[...].astype(BF)
        vwb_ref[...] = vw_ref[...].astype(BF)

    lane = lax.broadcasted_iota(jnp.int32, (TQ, HD), 1)
    t = qi * TQ + lax.broadcasted_iota(jnp.int32, (TQ, HD), 0)
    rr = lax.broadcasted_iota(jnp.int32, (TQ, chunk), 0)
    cc = lax.broadcasted_iota(jnp.int32, (TQ, chunk), 1)
    rel = (rr - cc).astype(F32)

    ck_hi, ck_lo = _split_bf16(ck_ref[0, 0])
    cv_bf = cv_ref[0, 0].astype(BF)
    dist_c = t - (lane * CMP_STRIDE + CMP_LEN - 1)
    o_cmp = []
    p_grp = jnp.zeros((TQ, HD), F32)
    for r in range(NSA_R):
        o_r, p_r = _cmp_attend(q_ref[:, r * HD:(r + 1) * HD], ck_hi, ck_lo, cv_bf, slopes_ref[g * NSA_R + r], dist_c)
        o_cmp.append(o_r)
        p_grp = p_grp + p_r
    cur = t // SEL_BLOCK
    n_cand = SEQ // SEL_BLOCK - 1
    member = _select_blocks(p_grp, cmat_ref[...], cur, lane, n_cand, NSA_TOPN - 1)
    allowed = (member > 0.5) | (lane == cur)
    neg_bf = jnp.where(allowed, 0.0, NEG_BIG).astype(BF)

    gates = jax.nn.sigmoid(jnp.where(g == 0, gn_ref[:, 0:NSA_R * 3], gn_ref[:, NSA_R * 3:2 * NSA_R * 3]))
    own = pl.multiple_of(qi * chunk, chunk)

    for r in range(NSA_R):
        slope = slopes_ref[g * NSA_R + r]
        q_bf = q_ref[:, r * HD:(r + 1) * HD].astype(BF)
        qx = jnp.concatenate([q_bf, neg_bf], axis=1)
        s = _dot_nt(qx, kx_ref[pl.ds(own, chunk), :]) * SCALE - slope * rel
        s = jnp.where(cc <= rr, s, -jnp.inf)
        st_ref[0], st_ref[1], st_ref[2] = _pad_state(_softmax_start(s, vsb_ref[pl.ds(own, chunk), :]))
        for c in range(SEQ // chunk - 1):
            @pl.when(c < qi)
            def _(c=c):
                base = ((qi - c) * chunk).astype(F32)
                s = _dot_nt(qx, kx_ref[c * chunk:(c + 1) * chunk, :]) * SCALE - slope * (rel + base)
                st = _softmax_step(_read_state(st_ref), s, vsb_ref[c * chunk:(c + 1) * chunk, :])
                st_ref[0], st_ref[1], st_ref[2] = _pad_state(st)
        m, l, acc = _read_state(st_ref)
        o_sel = acc / l

        s = _dot_nt(q_bf, kwb_ref[pl.ds(own, chunk), :]) * SCALE - slope * rel
        s = jnp.where(cc <= rr, s, -jnp.inf)
        st_ref[0], st_ref[1], st_ref[2] = _pad_state(_softmax_start(s, vwb_ref[pl.ds(own, chunk), :]))

        @pl.when(qi >= 1)
        def _():
            prev = pl.multiple_of((qi - 1) * chunk, chunk)
            s = _dot_nt(q_bf, kwb_ref[pl.ds(prev, chunk), :]) * SCALE - slope * (rel + float(chunk))
            st = _softmax_step(_read_state(st_ref), s, vwb_ref[pl.ds(prev, chunk), :])
            st_ref[0], st_ref[1], st_ref[2] = _pad_state(st)

        @pl.when(qi >= 2)
        def _():
            prev = pl.multiple_of((qi - 2) * chunk, chunk)
            s = _dot_nt(q_bf, kwb_ref[pl.ds(prev, chunk), :]) * SCALE - slope * (rel + float(2 * chunk))
            s = jnp.where(rr <= cc, s, -jnp.inf)
            st = _softmax_step(_read_state(st_ref), s, vwb_ref[pl.ds(prev, chunk), :])
            st_ref[0], st_ref[1], st_ref[2] = _pad_state(st)

        m, l, acc = _read_state(st_ref)
        o_win = acc / l
        o = (gates[:, 3 * r:3 * r + 1] * o_cmp[r] + gates[:, 3 * r + 1:3 * r + 2] * o_sel
             + gates[:, 3 * r + 2:3 * r + 3] * o_win)
        o_ref[:, r * HD:(r + 1) * HD] = o.astype(BF)


def _pad_state(state):
    m, l, acc = state
    return jnp.broadcast_to(m, acc.shape), jnp.broadcast_to(l, acc.shape), acc


def _read_state(st_ref):
    return st_ref[0][:, 0:1], st_ref[1][:, 0:1], st_ref[2]


def _nsa_prompt(slopes, qb, gn, ks, vs, kw, vw, ck, cv, cmat):
    nq = SEQ // TQ
    kv_spec = lambda: pl.BlockSpec((SEQ, HD), lambda b, g, qi: (b, g))
    c_spec = lambda: pl.BlockSpec((1, 1, N_CHUNK, HD), lambda b, g, qi: (b, g, 0, 0))
    return pl.pallas_call(
        _nsa_prompt_kernel,
        grid=(BATCH, NSA_G, nq),
        in_specs=[pl.BlockSpec(memory_space=pltpu.SMEM),
                  pl.BlockSpec((TQ, NSA_R * HD), lambda b, g, qi: (b * nq + qi, g)),
                  pl.BlockSpec((TQ, HD), lambda b, g, qi: (b * nq + qi, 0)),
                  kv_spec(), kv_spec(), kv_spec(), kv_spec(), c_spec(), c_spec(),
                  pl.BlockSpec((HD, HD), lambda b, g, qi: (0, 0))],
        out_specs=pl.BlockSpec((TQ, NSA_R * HD), lambda b, g, qi: (b * nq + qi, g)),
        out_shape=jax.ShapeDtypeStruct((BATCH * SEQ, NSA_W), BF),
        scratch_shapes=[pltpu.VMEM((SEQ, 2 * HD), BF), pltpu.VMEM((SEQ, HD), BF),
                        pltpu.VMEM((SEQ, HD), BF), pltpu.VMEM((SEQ, HD), BF),
                        pltpu.VMEM((3, TQ, HD), F32)],
        compiler_params=_params(3),
        name="nsa_prompt",
    )(slopes, qb, gn, ks, vs, kw, vw, ck, cv, cmat)


def _mem_attend_kernel(q_ref, k_ref, v_ref, o_ref):
    for h in range(MEM_HEADS):
        sl = slice(h * HD, (h + 1) * HD)
        s = _dot_nt(q_ref[:, sl].astype(BF), k_ref[:, sl].astype(BF)) * SCALE
        m = jnp.max(s, axis=1, keepdims=True)
        p = jnp.exp(s - m)
        p = p / jnp.sum(p, axis=1, keepdims=True)
        o_ref[:, sl] = _dot(p.astype(BF), v_ref[:, sl].astype(BF)).astype(BF)


def _mem_attend_prompt(qm, mem_k, mem_v):
    tq = 512
    nq = SEQ // tq
    return pl.pallas_call(
        _mem_attend_kernel,
        grid=(BATCH, nq),
        in_specs=[pl.BlockSpec((tq, MEM_W), lambda b, qi: (b * nq + qi, 0)),
                  pl.BlockSpec((MEM_LEN, MEM_W), lambda b, qi: (b, 0)),
                  pl.BlockSpec((MEM_LEN, MEM_W), lambda b, qi: (b, 0))],
        out_specs=pl.BlockSpec((tq, MEM_W), lambda b, qi: (b * nq + qi, 0)),
        out_shape=jax.ShapeDtypeStruct((BATCH * SEQ, MEM_W), BF),
        compiler_params=_params(2),
        name="mem_attend_prompt",
    )(qm, mem_k, mem_v)


def _merge_kernel(oa_ref, ob_ref, om_ref, gbr_ref, pa_ref, pb_ref, pm_ref, o_ref):
    gate = lambda i: jax.nn.sigmoid(gbr_ref[:, i * D_MODEL:(i + 1) * D_MODEL])
    merged = (gate(0) * _dot(oa_ref[...], pa_ref[...]) + gate(1) * _dot(ob_ref[...], pb_ref[...])
              + gate(2) * _dot(om_ref[...], pm_ref[...]))
    o_ref[...] = merged.astype(BF)


def _merge(oa, ob, om, gbr, pa, pb, pm, tm, tag):
    rows = oa.shape[0]
    row = lambda w: pl.BlockSpec((tm, w), lambda i: (i, 0))
    full = lambda a: pl.BlockSpec(a.shape, lambda i: (0, 0))
    return pl.pallas_call(
        _merge_kernel,
        grid=(rows // tm,),
        in_specs=[row(MOBA_W), row(NSA_W), row(MEM_W), row(3 * D_MODEL), full(pa), full(pb), full(pm)],
        out_specs=row(D_MODEL),
        out_shape=jax.ShapeDtypeStruct((rows, D_MODEL), BF),
        compiler_params=_params(1),
        name="branch_merge_" + tag,
    )(oa, ob, om, gbr, pa, pb, pm)


def _layer_norm(x, g, b):
    mu = jnp.mean(x, axis=-1, keepdims=True)
    xc = x - mu
    var = jnp.mean(xc * xc, axis=-1, keepdims=True)
    return xc * lax.rsqrt(var + LN_EPS) * g + b


def _out_ln_kernel(x_ref, m_ref, wo_ref, g_ref, b_ref, h_ref, hb_ref):
    h = _layer_norm(DN_ALPHA * x_ref[...] + _dot(m_ref[...], wo_ref[...]), g_ref[...], b_ref[...])
    h_ref[...] = h
    hb_ref[...] = h.astype(BF)


def _out_ln(x, merged, wo, g, b, tm, tag):
    rows = x.shape[0]
    row = lambda: pl.BlockSpec((tm, D_MODEL), lambda i: (i, 0))
    vec = lambda: pl.BlockSpec((1, D_MODEL), lambda i: (0, 0))
    return pl.pallas_call(
        _out_ln_kernel,
        grid=(rows // tm,),
        in_specs=[row(), row(), pl.BlockSpec((D_MODEL, D_MODEL), lambda i: (0, 0)), vec(), vec()],
        out_specs=[row(), row()],
        out_shape=[jax.ShapeDtypeStruct((rows, D_MODEL), F32), jax.ShapeDtypeStruct((rows, D_MODEL), BF)],
        compiler_params=_params(1),
        name="out_proj_ln_" + tag,
    )(x, merged, wo, g, b)


TF = 512
N_TF = FFN_DIM // TF
HALO = 8


def _ffn_kernel(hb_ref, halo_ref, h_ref, wu_ref, wg_ref, cw_ref, cb_ref, wd_ref, g_ref, b_ref, p2_ref, p1_ref,
                y_ref, u_ref, acc_ref, *, seq_rows):
    i = pl.program_id(0)
    f = pl.program_id(1)
    tm = hb_ref.shape[0]
    u = _dot(hb_ref[...], wu_ref[...])
    gate = _dot(hb_ref[...], wg_ref[...])
    if seq_rows:
        tiles_per_seq = seq_rows // tm
        live = jnp.where(i % tiles_per_seq == 0, 0.0, 1.0)
        u_halo = _dot(halo_ref[...], wu_ref[...]) * live
        row = lax.broadcasted_iota(jnp.int32, (HALO, TF), 0)
        prev1 = pltpu.roll(u, shift=1, axis=0)
        prev2 = pltpu.roll(u, shift=2, axis=0)
        head1 = jnp.where(row < 1, pltpu.roll(u_halo, shift=1, axis=0), prev1[:HALO])
        head2 = jnp.where(row < 2, pltpu.roll(u_halo, shift=2, axis=0), prev2[:HALO])
        prev1 = jnp.concatenate([head1, prev1[HALO:]], axis=0)
        prev2 = jnp.concatenate([head2, prev2[HALO:]], axis=0)
        u_ref[0] = u[tm - HALO:]
    else:
        prev2 = p2_ref[...]
        prev1 = p1_ref[...]
        u_ref[...] = u
    uc = cb_ref[...] + cw_ref[0:1, :] * prev2 + cw_ref[1:2, :] * prev1 + cw_ref[2:3, :] * u
    act = (_gelu_tanh(uc) * gate).astype(BF)
    part = _dot(act, wd_ref[...])

    @pl.when(f == 0)
    def _():
        acc_ref[...] = part

    @pl.when(f > 0)
    def _():
        acc_ref[...] += part

    @pl.when(f == N_TF - 1)
    def _():
        y_ref[...] = _layer_norm(DN_ALPHA * h_ref[...] + acc_ref[...], g_ref[...], b_ref[...])


def _ffn(h, h_bf, wu_bf, cw, cb, wd_bf, g, b, prev2, prev1, tm, seq_rows, tag):
    rows = h.shape[0]
    n_halo = tm // HALO
    if seq_rows:
        prev2 = prev1 = jnp.zeros((HALO, TF), F32)
        cs_spec = pl.BlockSpec((HALO, TF), lambda i, f: (0, 0))
        halo_spec = pl.BlockSpec((HALO, D_MODEL), lambda i, f: (jnp.maximum(i * n_halo - 1, 0), 0))
        u_shape = jax.ShapeDtypeStruct((rows // tm, HALO, FFN_DIM), F32)
        u_spec = pl.BlockSpec((1, HALO, TF), lambda i, f: (i, 0, f))
    else:
        cs_spec = pl.BlockSpec((tm, TF), lambda i, f: (i, f))
        halo_spec = pl.BlockSpec((HALO, D_MODEL), lambda i, f: (0, 0))
        u_shape = jax.ShapeDtypeStruct((rows, FFN_DIM), F32)
        u_spec = pl.BlockSpec((tm, TF), lambda i, f: (i, f))
    row = lambda: pl.BlockSpec((tm, D_MODEL), lambda i, f: (i, 0))
    vec = lambda: pl.BlockSpec((1, D_MODEL), lambda i, f: (0, 0))
    return pl.pallas_call(
        functools.partial(_ffn_kernel, seq_rows=seq_rows),
        grid=(rows // tm, N_TF),
        in_specs=[row(), halo_spec, row(),
                  pl.BlockSpec((D_MODEL, TF), lambda i, f: (0, f)),
                  pl.BlockSpec((D_MODEL, TF), lambda i, f: (0, N_TF + f)),
                  pl.BlockSpec((3, TF), lambda i, f: (0, f)),
                  pl.BlockSpec((1, TF), lambda i, f: (0, f)),
                  pl.BlockSpec((TF, D_MODEL), lambda i, f: (f, 0)),
                  vec(), vec(), cs_spec, cs_spec],
        out_specs=[row(), u_spec],
        out_shape=[jax.ShapeDtypeStruct((rows, D_MODEL), F32), u_shape],
        scratch_shapes=[pltpu.VMEM((tm, D_MODEL), F32)],
        compiler_params=_params(2),
        name="conv_ffn_" + tag,
    )(h_bf, h_bf, h, wu_bf, wu_bf, cw, cb, wd_bf, g, b, prev2, prev1)


def _rowdot(mat, vec):
    return jnp.sum(mat * vec, axis=1, keepdims=True)


def _moba_sample_kernel(pt_ref, slopes_ref, q_ref, kn_ref, vn_ref, *refs):
    del pt_ref
    k_pages, v_pages, o_ref = refs[:N_PAGES], refs[N_PAGES:2 * N_PAGES], refs[2 * N_PAGES]
    nb = PAST_LEN // MOBA_BLOCK
    ppb = MOBA_BLOCK // PAGE
    off = lax.broadcasted_iota(jnp.int32, (MOBA_BLOCK, 1), 0)
    for h in range(MOBA_HEADS):
        sl = slice(h * HD, (h + 1) * HD)
        q = q_ref[0, :, sl]
        block = lambda pages, j: jnp.concatenate([pages[j * ppb + i][h] for i in range(ppb)], axis=0)
        k_blk = [block(k_pages, j) for j in range(nb)]
        sc = [_rowdot(jnp.sum(kb, axis=0, keepdims=True) / MOBA_BLOCK, q) for kb in k_blk]
        s_own = _rowdot(kn_ref[0, :, sl], q) * SCALE
        s_blk, m = [], s_own
        for j in range(nb):
            beats = [(sc[i] > sc[j]) | ((sc[i] == sc[j]) & (i < j)) for i in range(nb) if i != j]
            rank = sum(b.astype(F32) for b in beats)
            dist = (PAST_LEN - (j * MOBA_BLOCK + off)).astype(F32)
            s = _rowdot(k_blk[j], q) * SCALE - slopes_ref[h] * dist
            s = jnp.where(rank < MOBA_TOPK, s, -jnp.inf)
            s_blk.append(s)
            m = jnp.maximum(m, jnp.max(s, axis=0, keepdims=True))
        p_own = jnp.exp(s_own - m)
        den = p_own
        acc = p_own * vn_ref[0, :, sl]
        for j in range(nb):
            p = jnp.exp(s_blk[j] - m)
            den = den + jnp.sum(p, axis=0, keepdims=True)
            acc = acc + jnp.sum(p * block(v_pages, j), axis=0, keepdims=True)
        o_ref[0, :, sl] = acc / den


def _moba_sample(page_table, slopes, q, k_new, v_new, k_pool, v_pool):
    row = lambda: pl.BlockSpec((1, 1, MOBA_W), lambda n, pt: (n, 0, 0))
    page = lambda p: pl.BlockSpec((None, MOBA_HEADS, PAGE, HD), lambda n, pt: (pt[n, p], 0, 0, 0))
    return pl.pallas_call(
        _moba_sample_kernel,
        grid_spec=pltpu.PrefetchScalarGridSpec(
            num_scalar_prefetch=1, grid=(DEC_BATCH,),
            in_specs=[pl.BlockSpec(memory_space=pltpu.SMEM), row(), row(), row()]
            + [page(p) for p in range(N_PAGES)] * 2,
            out_specs=row()),
        out_shape=jax.ShapeDtypeStruct((DEC_BATCH, 1, MOBA_W), F32),
        compiler_params=_params(1),
        name="moba_sample",
    )(page_table, slopes, q, k_new, v_new, *([k_pool] * N_PAGES), *([v_pool] * N_PAGES))


NS = 8
ROWS_PER_PAGE = PAGE * NSA_G
N_SEL = NSA_TOPN - 1
N_CAND_S = PAST_LEN // SEL_BLOCK


def _nsa_sample_cmp_kernel(pt_ref, slopes_ref, kcn_ref, vcn_ref, q_ref, pe_ref, w1_ref, w2_ref, cmat_ref, tri_ref,
                           *refs):
    del pt_ref
    k_pages, v_pages = refs[:N_PAGES], refs[N_PAGES:2 * N_PAGES]
    ocmp_ref, idx_ref, stage_ref = refs[2 * N_PAGES:2 * N_PAGES + 3]
    s_in = pl.program_id(1)
    unit = N_CHUNK
    chunks_per_page = PAGE // CMP_STRIDE

    for which, pages in enumerate((k_pages, v_pages)):
        for g in range(NSA_G):
            base = pl.multiple_of(s_in * (NSA_G * unit) + g * unit, unit)
            for r in range(CMP_STRIDE):
                x = jnp.concatenate(
                    [p[pl.ds(NSA_G * r + g, chunks_per_page, stride=NSA_G * CMP_STRIDE), :] for p in pages], axis=0)
                stage_ref[which, pl.ds(base, unit), r * HD:(r + 1) * HD] = x.astype(BF)

    @pl.when(s_in == NS - 1)
    def _():
        n_unit = NS * NSA_G
        rows = n_unit * unit
        row = lax.broadcasted_iota(jnp.int32, (rows, HD), 0)
        toks = []
        for which, new_ref in enumerate((kcn_ref, vcn_ref)):
            w1 = w1_ref.at[which]
            acc = _dot(stage_ref[which], w1[...])
            top, bot = acc[:, :HD], acc[:, HD:]
            nxt = pltpu.roll(bot, shift=rows - 1, axis=0)
            new_rows = jnp.concatenate([new_ref[s:s + 1, g * HD:(g + 1) * HD]
                                        for s in range(NS) for g in range(NSA_G)], axis=0)
            b_new = _dot(new_rows.astype(BF), w1[0:HD, HD:])
            b_last = jnp.concatenate([jnp.broadcast_to(b_new[u:u + 1], (unit, HD)) for u in range(n_unit)], axis=0)
            c_row = jnp.zeros((1, HD), F32)
            for r in range(CMP_LEN):
                half, rr = divmod(r, CMP_STRIDE)
                c_row = c_row + _dot(pe_ref[which, :, r * HD:(r + 1) * HD],
                                     w1[rr * HD:(rr + 1) * HD, half * HD:(half + 1) * HD])
            pre = top + jnp.where(row % unit == unit - 1, b_last, nxt) + c_row
            toks.append(_dot(_gelu_tanh(pre).astype(BF), w2_ref[which]))
        ck_all, cv_all = toks

        lane = lax.broadcasted_iota(jnp.int32, (8, HD), 1)
        row8 = lax.broadcasted_iota(jnp.int32, (8, HD), 0)
        dist = PAST_LEN - (lane * CMP_STRIDE + CMP_LEN - 1)
        for s in range(NS):
            for g in range(NSA_G):
                u = s * NSA_G + g
                ck_hi, ck_lo = _split_bf16(ck_all[u * unit:(u + 1) * unit])
                cv_bf = cv_all[u * unit:(u + 1) * unit].astype(BF)
                q8 = jnp.concatenate([q_ref[s:s + 1, (g * NSA_R + r) * HD:(g * NSA_R + r + 1) * HD]
                                      for r in range(NSA_R)] + [jnp.zeros((8 - NSA_R, HD), F32)], axis=0)
                slope = jnp.where(row8[:, 0:1] == 0, slopes_ref[g * NSA_R],
                                  jnp.where(row8[:, 0:1] == 1, slopes_ref[g * NSA_R + 1], slopes_ref[g * NSA_R + 2]))
                o8, p8 = _cmp_attend(q8, ck_hi, ck_lo, cv_bf, slope, dist)
                p_grp = jnp.sum(jnp.where(row8 < NSA_R, p8, 0.0), axis=0, keepdims=True)
                member = _select_blocks(jnp.broadcast_to(p_grp, (8, HD)), cmat_ref[...], N_CAND_S, lane,
                                        N_CAND_S, N_SEL)
                rank = _dot(member.astype(BF), tri_ref[...])
                idx = jnp.zeros((8, HD), F32)
                for c in range(N_SEL):
                    hit = (member > 0.5) & (rank == c)
                    idx_c = jnp.sum(jnp.where(hit, lane.astype(F32), 0.0), axis=1, keepdims=True)
                    idx = jnp.where(lane == c, idx_c, idx)
                ocmp_ref[s, g] = o8
                idx_ref[s, g] = idx.astype(jnp.int32)


def _nsa_sample_cmp(page_table, slopes, kc_new, vc_new, q, pe_bf, w1_flat, w2_bf, cmat, tri, ck_pool, cv_pool):
    page = lambda p: pl.BlockSpec((ROWS_PER_PAGE, HD), lambda i, s, pt: (pt[i * NS + s, p], 0))
    full = lambda a: pl.BlockSpec(a.shape, lambda i, s, pt: (0,) * a.ndim)
    rows = lambda w: pl.BlockSpec((NS, w), lambda i, s, pt: (i, 0))
    out = lambda: pl.BlockSpec((NS, NSA_G, 8, HD), lambda i, s, pt: (i, 0, 0, 0))
    return pl.pallas_call(
        _nsa_sample_cmp_kernel,
        grid_spec=pltpu.PrefetchScalarGridSpec(
            num_scalar_prefetch=1, grid=(DEC_BATCH // NS, NS),
            in_specs=[pl.BlockSpec(memory_space=pltpu.SMEM), rows(KV_W), rows(KV_W), rows(NSA_W),
                      full(pe_bf), full(w1_flat), full(w2_bf), full(cmat), full(tri)]
            + [page(p) for p in range(N_PAGES)] * 2,
            out_specs=[out(), out()],
            scratch_shapes=[pltpu.VMEM((2, NS * NSA_G * N_CHUNK, CMP_STRIDE * HD), BF)]),
        out_shape=[jax.ShapeDtypeStruct((DEC_BATCH, NSA_G, 8, HD), F32),
                   jax.ShapeDtypeStruct((DEC_BATCH, NSA_G, 8, HD), jnp.int32)],
        compiler_params=_params(2),
        name="nsa_sample_cmp",
    )(page_table, slopes, kc_new, vc_new, q, pe_bf, w1_flat, w2_bf, cmat, tri,
      *([ck_pool] * N_PAGES), *([cv_pool] * N_PAGES))


def _attend_rows(q, slope, k, v, dist, k_own, v_own):
    s = _rowdot(k, q) * SCALE - slope * dist
    s_own = _rowdot(k_own, q) * SCALE
    m = jnp.maximum(jnp.max(s, axis=0, keepdims=True), s_own)
    p = jnp.exp(s - m)
    p_own = jnp.exp(s_own - m)
    den = jnp.sum(p, axis=0, keepdims=True) + p_own
    return (jnp.sum(p * v, axis=0, keepdims=True) + p_own * v_own) / den


BLK_ROWS = SEL_BLOCK * NSA_G


def _nsa_sample_attend_kernel(pt_ref, ix_ref, slopes_ref, q_ref, gn_ref, ksn_ref, vsn_ref, kwn_ref, vwn_ref,
                              ocmp_ref, wk_ref, wv_ref, sk_hbm, sv_hbm, o_ref, kbuf, vbuf, sem):
    n = pl.program_id(0)
    halves = PAGE // SEL_BLOCK

    def block_copies(sample, slot):
        copies = []
        for g in range(NSA_G):
            for c in range(N_SEL):
                j = ix_ref[(sample * NSA_G + g) * N_SEL + c]
                row0 = pl.multiple_of((pt_ref[sample, j // halves] * halves + j % halves) * BLK_ROWS, BLK_ROWS)
                dst = pl.ds((g * N_SEL + c) * BLK_ROWS, BLK_ROWS)
                for pool, buf in ((sk_hbm, kbuf), (sv_hbm, vbuf)):
                    copies.append(pltpu.make_async_copy(pool.at[pl.ds(row0, BLK_ROWS), :], buf.at[slot, dst, :],
                                                        sem.at[slot]))
        return copies

    @pl.when(n == 0)
    def _():
        for cp in block_copies(0, 0):
            cp.start()

    @pl.when(n + 1 < DEC_BATCH)
    def _():
        for cp in block_copies(n + 1, (n + 1) % 2):
            cp.start()

    slot = n % 2
    for cp in block_copies(n, slot):
        cp.wait()

    wl = wk_ref.shape[0] // NSA_G
    off = lax.broadcasted_iota(jnp.int32, (SEL_BLOCK, 1), 0)
    dist_w = (wl - lax.broadcasted_iota(jnp.int32, (wl, 1), 0)).astype(F32)
    gates = jax.nn.sigmoid(gn_ref[0])
    for g in range(NSA_G):
        gsl = slice(g * HD, (g + 1) * HD)
        sel_rows = pl.ds(g * N_SEL * BLK_ROWS + g, N_SEL * SEL_BLOCK, stride=NSA_G)
        k_sel = kbuf[slot, sel_rows, :]
        v_sel = vbuf[slot, sel_rows, :]
        pos = jnp.concatenate([ix_ref[(n * NSA_G + g) * N_SEL + c] * SEL_BLOCK + off for c in range(N_SEL)], axis=0)
        dist_s = (PAST_LEN - pos).astype(F32)
        k_win = wk_ref[pl.ds(g, wl, stride=NSA_G), :]
        v_win = wv_ref[pl.ds(g, wl, stride=NSA_G), :]
        for r in range(NSA_R):
            hd = g * NSA_R + r
            hsl = slice(hd * HD, (hd + 1) * HD)
            q = q_ref[0, :, hsl]
            slope = slopes_ref[hd]
            o_sel = _attend_rows(q, slope, k_sel, v_sel, dist_s, ksn_ref[0, :, gsl], vsn_ref[0, :, gsl])
            o_win = _attend_rows(q, slope, k_win, v_win, dist_w, kwn_ref[0, :, gsl], vwn_ref[0, :, gsl])
            o_cmp = ocmp_ref[0, g, r:r + 1, :]
            o_ref[0, :, hsl] = (gates[:, 3 * hd:3 * hd + 1] * o_cmp + gates[:, 3 * hd + 1:3 * hd + 2] * o_sel
                                + gates[:, 3 * hd + 2:3 * hd + 3] * o_win)


def _nsa_sample_attend(page_table, sel_idx, slopes, q, gn, ks_new, vs_new, kw_new, vw_new, o_cmp,
                       win_k, win_v, sk_pool, sv_pool):
    row = lambda w: pl.BlockSpec((1, 1, w), lambda n, pt, ix: (n, 0, 0))
    win_rows = win_k.shape[0] // DEC_BATCH
    win = lambda: pl.BlockSpec((win_rows, HD), lambda n, pt, ix: (n, 0))
    any_spec = pl.BlockSpec(memory_space=pl.ANY)
    buf = pltpu.VMEM((2, NSA_G * N_SEL * BLK_ROWS, HD), F32)
    return pl.pallas_call(
        _nsa_sample_attend_kernel,
        grid_spec=pltpu.PrefetchScalarGridSpec(
            num_scalar_prefetch=2, grid=(DEC_BATCH,),
            in_specs=[pl.BlockSpec(memory_space=pltpu.SMEM), row(NSA_W), row(HD), row(KV_W), row(KV_W), row(KV_W),
                      row(KV_W), pl.BlockSpec((1, NSA_G, 8, HD), lambda n, pt, ix: (n, 0, 0, 0)), win(), win(),
                      any_spec, any_spec],
            out_specs=row(NSA_W),
            scratch_shapes=[buf, buf, pltpu.SemaphoreType.DMA((2,))]),
        out_shape=jax.ShapeDtypeStruct((DEC_BATCH, 1, NSA_W), F32),
        compiler_params=_params(1),
        name="nsa_sample_attend",
    )(page_table, sel_idx, slopes, q, gn, ks_new, vs_new, kw_new, vw_new, o_cmp, win_k, win_v, sk_pool, sv_pool)


def _mem_sample_kernel(q_ref, k_ref, v_ref, o_ref):
    for h in range(MEM_HEADS):
        sl = slice(h * HD, (h + 1) * HD)
        q = q_ref[0, :, sl]
        k = k_ref[pl.ds(h, MEM_LEN, stride=MEM_HEADS), :]
        v = v_ref[pl.ds(h, MEM_LEN, stride=MEM_HEADS), :]
        s = _rowdot(k, q) * SCALE
        p = jnp.exp(s - jnp.max(s, axis=0, keepdims=True))
        o_ref[0, :, sl] = jnp.sum(p * v, axis=0, keepdims=True) / jnp.sum(p, axis=0, keepdims=True)


def _mem_sample(q, mem_k, mem_v):
    rows = MEM_LEN * MEM_HEADS
    row = lambda: pl.BlockSpec((1, 1, MEM_W), lambda n: (n, 0, 0))
    kv = lambda: pl.BlockSpec((rows, HD), lambda n: (n, 0))
    return pl.pallas_call(
        _mem_sample_kernel,
        grid=(DEC_BATCH,),
        in_specs=[row(), kv(), kv()],
        out_specs=row(),
        out_shape=jax.ShapeDtypeStruct((DEC_BATCH, 1, MEM_W), F32),
        compiler_params=_params(1),
        name="mem_attend_sample",
    )(q, mem_k, mem_v)


def _project_all(x_bf, w, tm, paged, tag):
    kv_kind = "pages" if paged else "flat"
    qa, ka, va, qb = _proj(x_bf, w["wa"], w["ba"], MOBA_W, ["flat", kv_kind, kv_kind, "flat"], tm, "proj_moba_q_" + tag)
    kc, vc, ks, vs, kw, vw = _proj(x_bf, w["wb"], w["bb"], KV_W, ["flat"] * 6, tm, "proj_nsa_kv_" + tag)
    (qm,) = _proj(x_bf, w["wm"], w["bm"], MEM_W, ["flat"], tm, "proj_mem_q_" + tag)
    (gn,) = _proj(x_bf, w["wg"], w["bg"], HD, ["flat"], tm, "proj_nsa_gate_" + tag)
    gbr = _matmul(x_bf, w["wd"], w["bd"], tm, 768, "proj_branch_gate_" + tag)
    return qa, ka, va, qb, kc, vc, ks, vs, kw, vw, qm, gn, gbr


def _strict_lower_ones():
    return jnp.asarray(np.triu(np.ones((HD, HD), np.float32), 1), BF)


def kernel(x_prompt, x_sample, cache_mem_k, cache_mem_v, cache_moba_k, cache_moba_v, cache_nsa_cmp_k, cache_nsa_cmp_v, cache_nsa_sel_k, cache_nsa_sel_v, cache_nsa_win_k, cache_nsa_win_v, cache_ffn_conv, page_table, mem_prompt, w_in, b_in, w_mem_kv, cmp_pe, cmp_w1, cmp_w2, p_moba, p_nsa, p_mem, w_o, ln1_g, ln1_b, w_up, conv_w, conv_b, w_down, ln2_g, ln2_b):
    slopes = _alibi_slopes(MOBA_HEADS)
    w_bf = w_in[0].astype(BF)
    b0 = b_in[0][None, :]
    c0, c1, c2, c3 = 4 * MOBA_W, 4 * MOBA_W + 6 * KV_W, 4 * MOBA_W + 6 * KV_W + N_GATE, 4 * MOBA_W + 6 * KV_W + N_GATE + MEM_W
    w = {"wa": w_bf[:, :c0], "ba": b0[:, :c0], "wb": w_bf[:, c0:c1], "bb": b0[:, c0:c1],
         "wg": jnp.pad(w_bf[:, c1:c2], ((0, 0), (0, HD - N_GATE))), "bg": jnp.pad(b0[:, c1:c2], ((0, 0), (0, HD - N_GATE))),
         "wm": w_bf[:, c2:c3], "bm": b0[:, c2:c3], "wd": w_bf[:, c3:], "bd": b0[:, c3:]}
    pe_bf, w1_bf, w2_bf = _pack_cmp_weights(cmp_pe[0], cmp_w1[0], cmp_w2[0])
    pa_bf, pb_bf, pm_bf, wo_bf = p_moba[0].astype(BF), p_nsa[0].astype(BF), p_mem[0].astype(BF), w_o[0].astype(BF)
    wu_bf, wd_bf = w_up[0].astype(BF), w_down[0].astype(BF)

    xp = x_prompt.reshape(BATCH * SEQ, D_MODEL)
    xp_bf = xp.astype(BF)
    qa, ka, va, qb, kc, vc, ks, vs, kw, vw, qm, gn, gbr = _project_all(xp_bf, w, 1024, True, "prompt")
    mem_k, mem_v = _proj(mem_prompt.reshape(BATCH * MEM_LEN, D_MODEL).astype(BF), w_mem_kv[0].astype(BF),
                         jnp.zeros((1, 2 * MEM_W), F32), MEM_W, ["flat", "flat"], 512, "proj_mem_kv")
    o_a = _moba_prompt(slopes, qa, ka, va)
    ck, cv = _compress_prompt(kc, vc, pe_bf, w1_bf, w2_bf)
    cmat = _cmp_to_sel_matrix(N_CHUNK - 1, SEQ // SEL_BLOCK - 1)
    o_b = _nsa_prompt(slopes, qb, gn, ks, vs, kw, vw, ck, cv, cmat)
    o_m = _mem_attend_prompt(qm, mem_k, mem_v)
    merged = _merge(o_a, o_b, o_m, gbr, pa_bf, pb_bf, pm_bf, 256, "prompt")
    h, h_bf = _out_ln(xp, merged, wo_bf, ln1_g, ln1_b, 512, "prompt")
    y_p, u_tail = _ffn(h, h_bf, wu_bf, conv_w[0], conv_b, wd_bf, ln2_g, ln2_b, None, None, 512, SEQ, "prompt")

    pages = lambda a: a.reshape(1, BATCH, SEQ // PAGE, PAGE, NSA_G, HD)
    head_pages = lambda a: a.transpose(0, 1, 3, 2, 4)[None]
    last = lambda a: a.reshape(BATCH, SEQ, NSA_G, HD)[None, :, SEQ - WINDOW:]
    tiles_per_seq = SEQ // 512
    conv_p = u_tail.reshape(BATCH, tiles_per_seq, HALO, FFN_DIM)[:, -1, HALO - 2:][None]
    prompt_out = (y_p.reshape(BATCH, SEQ, D_MODEL),
                  mem_k.reshape(1, BATCH, MEM_LEN, MEM_HEADS, HD), mem_v.reshape(1, BATCH, MEM_LEN, MEM_HEADS, HD),
                  head_pages(ka), head_pages(va), pages(kc), pages(vc), pages(ks), pages(vs), last(kw), last(vw), conv_p)

    n = DEC_BATCH
    xs = x_sample.reshape(n, D_MODEL)
    qa, ka, va, qb, kc, vc, ks, vs, kw, vw, qm, gn, gbr = _project_all(xs.astype(BF), w, n, False, "sample")
    r3 = lambda a: a.reshape(n, 1, a.shape[-1])
    moba_pool = lambda c: c[0].transpose(0, 2, 1, 3)
    rows2d = lambda c: c.reshape(-1, HD)
    o_a = _moba_sample(page_table, slopes, r3(qa), r3(ka), r3(va), moba_pool(cache_moba_k), moba_pool(cache_moba_v))
    w1_flat = w1_bf.reshape(2, CMP_STRIDE * HD, 2 * HD)
    cmat_s = _cmp_to_sel_matrix(N_CHUNK, N_CAND_S)
    o_cmp, sel = _nsa_sample_cmp(page_table, slopes, kc, vc, qb, pe_bf, w1_flat, w2_bf, cmat_s, _strict_lower_ones(),
                                 rows2d(cache_nsa_cmp_k), rows2d(cache_nsa_cmp_v))
    sel_idx = sel[:, :, 0, :N_SEL].reshape(-1)
    o_b = _nsa_sample_attend(page_table, sel_idx, slopes, r3(qb), r3(gn), r3(ks), r3(vs), r3(kw), r3(vw), o_cmp,
                             rows2d(cache_nsa_win_k), rows2d(cache_nsa_win_v),
                             rows2d(cache_nsa_sel_k), rows2d(cache_nsa_sel_v))
    o_m = _mem_sample(r3(qm), rows2d(cache_mem_k), rows2d(cache_mem_v))
    flat_bf = lambda a: a.reshape(n, a.shape[-1]).astype(BF)
    merged = _merge(flat_bf(o_a), flat_bf(o_b), flat_bf(o_m), gbr, pa_bf, pb_bf, pm_bf, n, "sample")
    h, h_bf = _out_ln(xs, merged, wo_bf, ln1_g, ln1_b, n, "sample")
    conv_old = cache_ffn_conv[0]
    y_s, u_new = _ffn(h, h_bf, wu_bf, conv_w[0], conv_b, wd_bf, ln2_g, ln2_b, conv_old[:, 0], conv_old[:, 1], n, 0,
                      "sample")

    heads = lambda a, nh: a.reshape(1, n, 1, nh, HD)
    roll_in = lambda cache, new: jnp.concatenate([cache[:, :, 1:], new.reshape(1, n, 1, NSA_G, HD)], axis=2)
    sample_out = (heads(ka, MOBA_HEADS), heads(va, MOBA_HEADS), heads(kc, NSA_G), heads(vc, NSA_G),
                  heads(ks, NSA_G), heads(vs, NSA_G), roll_in(cache_nsa_win_k, kw), roll_in(cache_nsa_win_v, vw),
                  jnp.stack([conv_old[:, 1], u_new], axis=1)[None])
    return (prompt_out[0], y_s.reshape(n, 1, D_MODEL)) + prompt_out[1:] + sample_out
```

```python
import functools

import numpy as np
import jax
import jax.numpy as jnp
from jax import lax
from jax.experimental import pallas as pl
from jax.experimental.pallas import tpu as pltpu

D_MODEL = 2048
BATCH = 4
SEQ = 2048
DEC_BATCH = 128
PAST_LEN = 2048
PAGE = 128
N_PAGES = PAST_LEN // PAGE
HD = 128
MOBA_HEADS = 6
MOBA_BLOCK = 256
MOBA_TOPK = 3
NSA_HEADS = 6
NSA_G = 2
NSA_R = 3
CMP_LEN = 32
CMP_STRIDE = 16
SEL_BLOCK = 64
NSA_TOPN = 16
WINDOW = 512
MEM_HEADS = 4
MEM_LEN = 256
FFN_DIM = 5632
DN_ALPHA = 2.0 ** 0.25
LN_EPS = 1e-5
SCALE = HD ** -0.5
MOBA_W = MOBA_HEADS * HD
NSA_W = NSA_HEADS * HD
KV_W = NSA_G * HD
MEM_W = MEM_HEADS * HD
N_GATE = NSA_HEADS * 3

F32 = jnp.float32
BF = jnp.bfloat16
NEG_BIG = -(2.0 ** 100)
VMEM_LIMIT = 48 * 1024 * 1024
TQ = 256


def _dot(a, b):
    return jnp.dot(a, b, preferred_element_type=F32)


def _dot_nt(a, b):
    return lax.dot_general(a, b, (((1,), (1,)), ((), ())), preferred_element_type=F32)


def _split_bf16(a):
    hi = a.astype(BF)
    lo = (a - hi.astype(F32)).astype(BF)
    return hi, lo


def _alibi_slopes(n):
    return jnp.asarray(2.0 ** (-8.0 * np.arange(1, n + 1) / n), F32)


def _gelu_tanh(x):
    return 0.5 * x * (1.0 + jnp.tanh(np.sqrt(2.0 / np.pi).astype(np.float32) * (x + 0.044715 * (x * x * x))))


def _params(n_axes):
    return pltpu.CompilerParams(dimension_semantics=("arbitrary",) * n_axes, vmem_limit_bytes=VMEM_LIMIT)


def _proj_kernel(x_ref, w_ref, b_ref, *out_refs, kinds, w_rows):
    j = pl.program_id(1)
    acc = (_dot_nt if w_rows else _dot)(x_ref[...], w_ref[...]) + b_ref[...]
    for idx, (kind, o_ref) in enumerate(zip(kinds, out_refs)):
        @pl.when(j == idx)
        def _(kind=kind, o_ref=o_ref):
            if kind == "flat":
                o_ref[...] = acc
            else:
                for p in range(o_ref.shape[1]):
                    for h in range(o_ref.shape[2]):
                        o_ref[0, p, h] = acc[p * PAGE:(p + 1) * PAGE, h * HD:(h + 1) * HD]


def _proj(x_bf, w_bf, b, tn, kinds, tm, name, w_rows=False):
    rows, k = x_bf.shape
    n_out = len(kinds)
    assert w_bf.shape == ((n_out * tn, k) if w_rows else (k, n_out * tn)) and rows % tm == 0
    w_spec = pl.BlockSpec((tn, k), lambda i, j: (j, 0)) if w_rows else pl.BlockSpec((k, tn), lambda i, j: (0, j))
    tiles_per_batch = SEQ // tm if tm <= SEQ else 1
    out_shape, out_specs = [], []
    for idx, kind in enumerate(kinds):
        if kind == "flat":
            out_shape.append(jax.ShapeDtypeStruct((rows, tn), F32))
            out_specs.append(pl.BlockSpec((tm, tn), lambda i, j: (i, 0)))
        else:
            heads = tn // HD
            out_shape.append(jax.ShapeDtypeStruct((rows // SEQ, SEQ // PAGE, heads, PAGE, HD), F32))
            out_specs.append(pl.BlockSpec((1, tm // PAGE, heads, PAGE, HD),
                                          lambda i, j: (i // tiles_per_batch, i % tiles_per_batch, 0, 0, 0)))
    return pl.pallas_call(
        functools.partial(_proj_kernel, kinds=tuple(kinds), w_rows=w_rows),
        grid=(rows // tm, n_out),
        in_specs=[pl.BlockSpec((tm, k), lambda i, j: (i, 0)), w_spec,
                  pl.BlockSpec((1, tn), lambda i, j: (0, j))],
        out_specs=out_specs,
        out_shape=out_shape,
        compiler_params=_params(2),
        name=name,
    )(x_bf, w_bf, b)


def _matmul_kernel(x_ref, w_ref, b_ref, o_ref):
    o_ref[...] = _dot_nt(x_ref[...], w_ref[...]) + b_ref[...]


def _matmul(x_bf, w_bf, b, tm, tn, name):
    rows, k = x_bf.shape
    n = w_bf.shape[0]
    return pl.pallas_call(
        _matmul_kernel,
        grid=(rows // tm, n // tn),
        in_specs=[pl.BlockSpec((tm, k), lambda i, j: (i, 0)),
                  pl.BlockSpec((tn, k), lambda i, j: (j, 0)),
                  pl.BlockSpec((1, tn), lambda i, j: (0, j))],
        out_specs=pl.BlockSpec((tm, tn), lambda i, j: (i, j)),
        out_shape=jax.ShapeDtypeStruct((rows, n), F32),
        compiler_params=_params(2),
        name=name,
    )(x_bf, w_bf, b)


def _topk_member(v, cand, n_cand, k):
    member = jnp.zeros(v.shape, F32)
    for j in range(n_cand):
        vj = v[j:j + 1, :]
        beats = (v > vj) | ((v == vj) & (cand < j))
        cnt = jnp.sum(beats.astype(F32), axis=0, keepdims=True)
        member = jnp.where((cand == j) & (cnt < k), 1.0, member)
    return member


def _to_rows(member_t):
    pad = jnp.zeros((HD - member_t.shape[0], member_t.shape[1]), F32)
    return jnp.transpose(jnp.concatenate([member_t, pad], axis=0))


EXT_MASK = 32
N_PIECE = 3
POS_SPLIT = 64
EXP2_SCALE = float(SCALE * np.log2(np.e))


def _slope_rows(n_heads):
    slope = np.asarray(2.0 ** (-8.0 * np.arange(1, n_heads + 1) / n_heads), np.float32) / np.float32(SCALE)
    rows = np.zeros((8, HD), np.float32)
    rest = slope.astype(np.float32)
    for i in range(N_PIECE):
        piece = rest.astype(BF).astype(np.float32)
        rows[:n_heads, EXT_MASK + i] = piece * POS_SPLIT
        rows[:n_heads, EXT_MASK + N_PIECE + i] = piece
        rest = rest - piece
    return jnp.asarray(rows)


def _key_ext(n_keys, block):
    key = lax.broadcasted_iota(jnp.int32, (n_keys, HD), 0)
    lane = lax.broadcasted_iota(jnp.int32, (n_keys, HD), 1)
    hot = jnp.where(key // block == lane, 1.0, 0.0)
    pos = jnp.where(lane < EXT_MASK + N_PIECE, (key // POS_SPLIT).astype(F32), (key % POS_SPLIT).astype(F32))
    return jnp.where(lane < EXT_MASK, hot, jnp.where(lane < EXT_MASK + 2 * N_PIECE, pos, 0.0)).astype(BF)


def _query_ext(allowed, slope_row, lane):
    mask = 0.0 if allowed is None else jnp.where(allowed, 0.0, NEG_BIG)
    return jnp.where(lane < EXT_MASK, mask, slope_row).astype(BF)


def _flash_init(qx, kx, v_bf, keep):
    s = jnp.where(keep, _dot_nt(qx, kx), NEG_BIG)
    m = jnp.max(s, axis=1, keepdims=True)
    p = jnp.exp2((s - m) * EXP2_SCALE)
    return m, jnp.sum(p, axis=1, keepdims=True), _dot(p.astype(BF), v_bf)


def _flash_step(state, qx, kx, v_bf, keep=None):
    m_old, l_old, acc_old = state
    s = _dot_nt(qx, kx)
    if keep is not None:
        s = jnp.where(keep, s, NEG_BIG)
    m_new = jnp.maximum(m_old, jnp.max(s, axis=1, keepdims=True))
    alpha = jnp.exp2((m_old - m_new) * EXP2_SCALE)
    p = jnp.exp2((s - m_new) * EXP2_SCALE)
    return (m_new, alpha * l_old + jnp.sum(p, axis=1, keepdims=True),
            alpha * acc_old + _dot(p.astype(BF), v_bf))


MOBA_HB = 3


def _moba_prompt_kernel(srow_ref, q_ref, k_ref, v_ref, o_ref, kx_ref, vb_ref, kmh_ref, kml_ref):
    hg = pl.program_id(1)
    qi = pl.program_id(2)
    nb = SEQ // MOBA_BLOCK

    @pl.when(qi == 0)
    def _():
        ext = _key_ext(SEQ, MOBA_BLOCK)
        for hh in range(MOBA_HB):
            k = k_ref[:, hh].reshape(SEQ, HD)
            kx_ref[hh, :, :HD] = k.astype(BF)
            kx_ref[hh, :, HD:] = ext
            vb_ref[hh] = v_ref[:, hh].reshape(SEQ, HD).astype(BF)
            km = jnp.sum(k.reshape(nb, MOBA_BLOCK, HD), axis=1) / MOBA_BLOCK
            hi, lo = _split_bf16(jnp.concatenate([km, jnp.zeros((HD - nb, HD), F32)], axis=0))
            kmh_ref[hh] = hi
            kml_ref[hh] = lo

    lane = lax.broadcasted_iota(jnp.int32, (TQ, HD), 1)
    cand = lax.broadcasted_iota(jnp.int32, (nb, TQ), 0)
    rr = lax.broadcasted_iota(jnp.int32, (TQ, MOBA_BLOCK), 0)
    cc = lax.broadcasted_iota(jnp.int32, (TQ, MOBA_BLOCK), 1)
    own = pl.multiple_of(qi * MOBA_BLOCK, MOBA_BLOCK)
    qx, states = [], []
    for hh in range(MOBA_HB):
        q = q_ref[:, hh * HD:(hh + 1) * HD]
        qh, ql = _split_bf16(q)
        sc = (_dot_nt(kmh_ref[hh], qh) + _dot_nt(kml_ref[hh], qh) + _dot_nt(kmh_ref[hh], ql))[:nb]
        sc = jnp.where(cand < qi, sc, -jnp.inf)
        member = _topk_member(sc, cand, nb - 1, MOBA_TOPK)
        allowed = (_to_rows(member) > 0.5) | (lane == qi)
        ext = _query_ext(allowed, srow_ref[pl.ds(hg * MOBA_HB + hh, 1), :], lane)
        qx.append(jnp.concatenate([qh, ext], axis=1))
        states.append(_flash_init(qx[hh], kx_ref[hh, pl.ds(own, MOBA_BLOCK), :],
                                  vb_ref[hh, pl.ds(own, MOBA_BLOCK), :], cc <= rr))

    def body(c, states):
        rows = pl.ds(pl.multiple_of(c * MOBA_BLOCK, MOBA_BLOCK), MOBA_BLOCK)
        return tuple(_flash_step(states[hh], qx[hh], kx_ref[hh, rows, :], vb_ref[hh, rows, :])
                     for hh in range(MOBA_HB))

    states = lax.fori_loop(0, qi, body, tuple(states))
    for hh in range(MOBA_HB):
        m, l, acc = states[hh]
        o_ref[:, hh * HD:(hh + 1) * HD] = (acc / l).astype(BF)


def _moba_prompt(slope_rows, q, k_pages, v_pages):
    nq = SEQ // TQ
    kv_spec = pl.BlockSpec((None, SEQ // PAGE, MOBA_HB, PAGE, HD), lambda b, h, qi: (b, 0, h, 0, 0))
    q_spec = pl.BlockSpec((TQ, MOBA_HB * HD), lambda b, h, qi: (b * nq + qi, h))
    return pl.pallas_call(
        _moba_prompt_kernel,
        grid=(BATCH, MOBA_HEADS // MOBA_HB, nq),
        in_specs=[pl.BlockSpec((8, HD), lambda b, h, qi: (0, 0)), q_spec, kv_spec, kv_spec],
        out_specs=q_spec,
        out_shape=jax.ShapeDtypeStruct((BATCH * SEQ, MOBA_W), BF),
        scratch_shapes=[pltpu.VMEM((MOBA_HB, SEQ, 2 * HD), BF), pltpu.VMEM((MOBA_HB, SEQ, HD), BF),
                        pltpu.VMEM((MOBA_HB, HD, HD), BF), pltpu.VMEM((MOBA_HB, HD, HD), BF)],
        compiler_params=_params(3),
        name="moba_prompt",
    )(slope_rows, q, k_pages, v_pages)


N_CHUNK = SEQ // CMP_STRIDE


def _compress_tokens(get_x, w1_ref, w2_ref, c_row, b_last):
    acc = jnp.zeros((N_CHUNK, 2 * HD), F32)
    for r in range(CMP_STRIDE):
        acc = acc + _dot(get_x(r).astype(BF), w1_ref[r])
    top, bot = acc[:, :HD], acc[:, HD:]
    nxt = pltpu.roll(bot, shift=N_CHUNK - 1, axis=0)
    row = lax.broadcasted_iota(jnp.int32, (N_CHUNK, HD), 0)
    pre = top + jnp.where(row == N_CHUNK - 1, b_last, nxt) + c_row
    return _dot(_gelu_tanh(pre).astype(BF), w2_ref[...])


def _compress_prompt_kernel(kc_ref, vc_ref, pe_ref, w1_ref, w2_ref, ck_ref, cv_ref):
    for which, (src, dst) in enumerate(((kc_ref, ck_ref), (vc_ref, cv_ref))):
        w1 = w1_ref.at[which]
        c_row = _pe_term(pe_ref, w1, which)
        get_x = lambda r, src=src: src[pl.ds(r, N_CHUNK, stride=CMP_STRIDE), :]
        dst[0, 0] = _compress_tokens(get_x, w1, w2_ref.at[which], c_row, jnp.zeros((1, HD), F32))


def _pe_term(pe_ref, w1, which):
    c_row = jnp.zeros((1, HD), F32)
    for r in range(CMP_LEN):
        half, rr = divmod(r, CMP_STRIDE)
        c_row = c_row + _dot(pe_ref[which, :, r * HD:(r + 1) * HD], w1[rr, :, half * HD:(half + 1) * HD])
    return c_row


def _pack_cmp_weights(cmp_pe, cmp_w1, cmp_w2):
    w1 = cmp_w1.reshape(2, 2, CMP_STRIDE, HD, HD).transpose(0, 2, 3, 1, 4).reshape(2, CMP_STRIDE, HD, 2 * HD)
    return cmp_pe.reshape(2, 1, CMP_LEN * HD).astype(BF), w1.astype(BF), cmp_w2.astype(BF)


def _compress_prompt(kc, vc, pe_bf, w1_bf, w2_bf):
    full = lambda a: pl.BlockSpec(a.shape, lambda b, g: (0,) * a.ndim)
    return pl.pallas_call(
        _compress_prompt_kernel,
        grid=(BATCH, NSA_G),
        in_specs=[pl.BlockSpec((SEQ, HD), lambda b, g: (b, g)), pl.BlockSpec((SEQ, HD), lambda b, g: (b, g)),
                  full(pe_bf), full(w1_bf), full(w2_bf)],
        out_specs=[pl.BlockSpec((1, 1, N_CHUNK, HD), lambda b, g: (b, g, 0, 0))] * 2,
        out_shape=[jax.ShapeDtypeStruct((BATCH, NSA_G, N_CHUNK, HD), F32)] * 2,
        compiler_params=_params(2),
        name="nsa_compress_prompt",
    )(kc, vc, pe_bf, w1_bf, w2_bf)


def _cmp_to_sel_matrix(n_cmp, n_blocks):
    start = np.arange(n_cmp)[:, None] * CMP_STRIDE
    j0 = np.arange(n_blocks)[None, :] * SEL_BLOCK
    ov = np.minimum(start + CMP_LEN, j0 + SEL_BLOCK) - np.maximum(start, j0)
    m = np.zeros((HD, HD), np.float32)
    m[:n_blocks, :n_cmp] = (np.clip(ov, 0, None) / CMP_LEN).T
    return jnp.asarray(m, BF)


def _cmp_attend(q, ck_hi, ck_lo, cv_bf, slope, dist):
    qh, ql = _split_bf16(q)
    s = (_dot_nt(qh, ck_hi) + _dot_nt(qh, ck_lo) + _dot_nt(ql, ck_hi)) * SCALE - slope * dist.astype(F32)
    s = jnp.where(dist >= 0, s, -jnp.inf)
    m = jnp.max(s, axis=1, keepdims=True)
    m = jnp.where(m > -jnp.inf, m, 0.0)
    p = jnp.exp(s - m)
    p = p / jnp.maximum(jnp.sum(p, axis=1, keepdims=True), 1e-30)
    return _dot(p.astype(BF), cv_bf), p


def _select_blocks(p_grp, cmat_bf, cur, n_cand, k):
    ph, plo = _split_bf16(p_grp)
    imp = (_dot_nt(cmat_bf, ph) + _dot_nt(cmat_bf, plo))[:EXT_MASK]
    cand = lax.broadcasted_iota(jnp.int32, imp.shape, 0)
    valid = (cand < cur) & (cand < n_cand)
    forced = valid & ((cand == 0) | (cand == cur - 1))
    v = jnp.where(forced, jnp.inf, jnp.where(valid, imp, -jnp.inf))
    return jnp.where(valid, _topk_member(v, cand, n_cand, k), 0.0)


def _nsa_prompt_kernel(slopes_ref, srow_ref, q_ref, gn_ref, ks_ref, vs_ref, kw_ref, vw_ref, ck_ref, cv_ref, cmat_ref,
                       o_ref, kx_ref, kwx_ref, vsb_ref, vwb_ref):
    g = pl.program_id(1)
    qi = pl.program_id(2)
    chunk = TQ
    rows = NSA_R * TQ

    @pl.when(qi == 0)
    def _():
        ext = _key_ext(SEQ, SEL_BLOCK)
        kx_ref[:, :HD] = ks_ref[...].astype(BF)
        kx_ref[:, HD:] = ext
        kwx_ref[:, :HD] = kw_ref[...].astype(BF)
        kwx_ref[:, HD:] = ext
        vsb_ref[...] = vs_ref[...].astype(BF)
        vwb_ref[...] = vw_ref[...].astype(BF)

    lane = lax.broadcasted_iota(jnp.int32, (TQ, HD), 1)
    t = qi * TQ + lax.broadcasted_iota(jnp.int32, (TQ, HD), 0)
    stack = lambda parts: jnp.concatenate(parts, axis=0)
    q_all = stack([q_ref[:, r * HD:(r + 1) * HD] for r in range(NSA_R)])

    ck_hi, ck_lo = _split_bf16(ck_ref[0, 0])
    slope_col = stack([jnp.full((TQ, 1), slopes_ref[g * NSA_R + r], F32) for r in range(NSA_R)])
    dist_c = stack([t - (lane * CMP_STRIDE + CMP_LEN - 1)] * NSA_R)
    o_cmp, p = _cmp_attend(q_all, ck_hi, ck_lo, cv_ref[0, 0].astype(BF), slope_col, dist_c)
    p_grp = p[0:TQ] + p[TQ:2 * TQ] + p[2 * TQ:3 * TQ]
    cur_t = (qi * TQ + lax.broadcasted_iota(jnp.int32, (EXT_MASK, TQ), 1)) // SEL_BLOCK
    member = _select_blocks(p_grp, cmat_ref[...], cur_t, SEQ // SEL_BLOCK - 1, NSA_TOPN - 1)
    allowed = (_to_rows(member) > 0.5) | (lane == t // SEL_BLOCK)

    srow = [srow_ref[pl.ds(g * NSA_R + r, 1), :] for r in range(NSA_R)]
    q_bf = q_all.astype(BF)
    qx = jnp.concatenate([q_bf, stack([_query_ext(allowed, srow[r], lane) for r in range(NSA_R)])], axis=1)
    qw = jnp.concatenate([q_bf, stack([_query_ext(None, srow[r], lane) for r in range(NSA_R)])], axis=1)
    rr = lax.broadcasted_iota(jnp.int32, (rows, chunk), 0) % TQ
    cc = lax.broadcasted_iota(jnp.int32, (rows, chunk), 1)
    own = pl.ds(pl.multiple_of(qi * chunk, chunk), chunk)
    past = lambda c: pl.ds(pl.multiple_of(c * chunk, chunk), chunk)

    st = _flash_init(qx, kx_ref[own, :], vsb_ref[own, :], cc <= rr)
    st = lax.fori_loop(0, qi, lambda c, st: _flash_step(st, qx, kx_ref[past(c), :], vsb_ref[past(c), :]), st)
    o_sel = st[2] / st[1]

    st = _flash_init(qw, kwx_ref[own, :], vwb_ref[own, :], cc <= rr)

    def win_step(c, st):
        slack = jnp.where(c == qi - WINDOW // chunk, 0, chunk)
        return _flash_step(st, qw, kwx_ref[past(c), :], vwb_ref[past(c), :], rr - cc <= slack)

    st = lax.fori_loop(jnp.maximum(qi - WINDOW // chunk, 0), qi, win_step, st)
    o_win = st[2] / st[1]

    gates = jax.nn.sigmoid(jnp.where(g == 0, gn_ref[:, 0:NSA_R * 3], gn_ref[:, NSA_R * 3:2 * NSA_R * 3]))
    for r in range(NSA_R):
        sl = slice(r * TQ, (r + 1) * TQ)
        o = (gates[:, 3 * r:3 * r + 1] * o_cmp[sl] + gates[:, 3 * r + 1:3 * r + 2] * o_sel[sl]
             + gates[:, 3 * r + 2:3 * r + 3] * o_win[sl])
        o_ref[:, r * HD:(r + 1) * HD] = o.astype(BF)


def _nsa_prompt(slopes, slope_rows, qb, gn, ks, vs, kw, vw, ck, cv, cmat):
    nq = SEQ // TQ
    kv_spec = lambda: pl.BlockSpec((SEQ, HD), lambda b, g, qi: (b, g))
    c_spec = lambda: pl.BlockSpec((1, 1, N_CHUNK, HD), lambda b, g, qi: (b, g, 0, 0))
    return pl.pallas_call(
        _nsa_prompt_kernel,
        grid=(BATCH, NSA_G, nq),
        in_specs=[pl.BlockSpec(memory_space=pltpu.SMEM),
                  pl.BlockSpec((8, HD), lambda b, g, qi: (0, 0)),
                  pl.BlockSpec((TQ, NSA_R * HD), lambda b, g, qi: (b * nq + qi, g)),
                  pl.BlockSpec((TQ, HD), lambda b, g, qi: (b * nq + qi, 0)),
                  kv_spec(), kv_spec(), kv_spec(), kv_spec(), c_spec(), c_spec(),
                  pl.BlockSpec((HD, HD), lambda b, g, qi: (0, 0))],
        out_specs=pl.BlockSpec((TQ, NSA_R * HD), lambda b, g, qi: (b * nq + qi, g)),
        out_shape=jax.ShapeDtypeStruct((BATCH * SEQ, NSA_W), BF),
        scratch_shapes=[pltpu.VMEM((SEQ, 2 * HD), BF), pltpu.VMEM((SEQ, 2 * HD), BF),
                        pltpu.VMEM((SEQ, HD), BF), pltpu.VMEM((SEQ, HD), BF)],
        compiler_params=_params(3),
        name="nsa_prompt",
    )(slopes, slope_rows, qb, gn, ks, vs, kw, vw, ck, cv, cmat)


def _mem_attend_kernel(q_ref, k_ref, v_ref, o_ref):
    for h in range(MEM_HEADS):
        sl = slice(h * HD, (h + 1) * HD)
        s = _dot_nt(q_ref[:, sl].astype(BF), k_ref[:, sl].astype(BF)) * SCALE
        m = jnp.max(s, axis=1, keepdims=True)
        p = jnp.exp(s - m)
        p = p / jnp.sum(p, axis=1, keepdims=True)
        o_ref[:, sl] = _dot(p.astype(BF), v_ref[:, sl].astype(BF)).astype(BF)


def _mem_attend_prompt(qm, mem_k, mem_v):
    tq = 512
    nq = SEQ // tq
    return pl.pallas_call(
        _mem_attend_kernel,
        grid=(BATCH, nq),
        in_specs=[pl.BlockSpec((tq, MEM_W), lambda b, qi: (b * nq + qi, 0)),
                  pl.BlockSpec((MEM_LEN, MEM_W), lambda b, qi: (b, 0)),
                  pl.BlockSpec((MEM_LEN, MEM_W), lambda b, qi: (b, 0))],
        out_specs=pl.BlockSpec((tq, MEM_W), lambda b, qi: (b * nq + qi, 0)),
        out_shape=jax.ShapeDtypeStruct((BATCH * SEQ, MEM_W), BF),
        compiler_params=_params(2),
        name="mem_attend_prompt",
    )(qm, mem_k, mem_v)


def _merge_kernel(oa_ref, ob_ref, om_ref, gbr_ref, pa_ref, pb_ref, pm_ref, o_ref):
    gate = lambda i: jax.nn.sigmoid(gbr_ref[:, i * D_MODEL:(i + 1) * D_MODEL])
    merged = (gate(0) * _dot(oa_ref[...], pa_ref[...]) + gate(1) * _dot(ob_ref[...], pb_ref[...])
              + gate(2) * _dot(om_ref[...], pm_ref[...]))
    o_ref[...] = merged.astype(BF)


def _merge(oa, ob, om, gbr, pa, pb, pm, tm, tag):
    rows = oa.shape[0]
    row = lambda w: pl.BlockSpec((tm, w), lambda i: (i, 0))
    full = lambda a: pl.BlockSpec(a.shape, lambda i: (0, 0))
    return pl.pallas_call(
        _merge_kernel,
        grid=(rows // tm,),
        in_specs=[row(MOBA_W), row(NSA_W), row(MEM_W), row(3 * D_MODEL), full(pa), full(pb), full(pm)],
        out_specs=row(D_MODEL),
        out_shape=jax.ShapeDtypeStruct((rows, D_MODEL), BF),
        compiler_params=_params(1),
        name="branch_merge_" + tag,
    )(oa, ob, om, gbr, pa, pb, pm)


def _layer_norm(x, g, b):
    mu = jnp.mean(x, axis=-1, keepdims=True)
    xc = x - mu
    var = jnp.mean(xc * xc, axis=-1, keepdims=True)
    return xc * lax.rsqrt(var + LN_EPS) * g + b


def _out_ln_kernel(x_ref, m_ref, wo_ref, g_ref, b_ref, h_ref, hb_ref):
    h = _layer_norm(DN_ALPHA * x_ref[...] + _dot(m_ref[...], wo_ref[...]), g_ref[...], b_ref[...])
    h_ref[...] = h
    hb_ref[...] = h.astype(BF)


def _out_ln(x, merged, wo, g, b, tm, tag):
    rows = x.shape[0]
    row = lambda: pl.BlockSpec((tm, D_MODEL), lambda i: (i, 0))
    vec = lambda: pl.BlockSpec((1, D_MODEL), lambda i: (0, 0))
    return pl.pallas_call(
        _out_ln_kernel,
        grid=(rows // tm,),
        in_specs=[row(), row(), pl.BlockSpec((D_MODEL, D_MODEL), lambda i: (0, 0)), vec(), vec()],
        out_specs=[row(), row()],
        out_shape=[jax.ShapeDtypeStruct((rows, D_MODEL), F32), jax.ShapeDtypeStruct((rows, D_MODEL), BF)],
        compiler_params=_params(1),
        name="out_proj_ln_" + tag,
    )(x, merged, wo, g, b)


TF = 512
N_TF = FFN_DIM // TF
HALO = 8


def _ffn_kernel(hb_ref, halo_ref, h_ref, wu_ref, wg_ref, cw_ref, cb_ref, wd_ref, g_ref, b_ref, p2_ref, p1_ref,
                y_ref, u_ref, acc_ref, *, seq_rows):
    i = pl.program_id(0)
    f = pl.program_id(1)
    tm = hb_ref.shape[0]
    u = _dot(hb_ref[...], wu_ref[...])
    gate = _dot(hb_ref[...], wg_ref[...])
    if seq_rows:
        tiles_per_seq = seq_rows // tm
        live = jnp.where(i % tiles_per_seq == 0, 0.0, 1.0)
        u_halo = _dot(halo_ref[...], wu_ref[...]) * live
        row = lax.broadcasted_iota(jnp.int32, (HALO, TF), 0)
        prev1 = pltpu.roll(u, shift=1, axis=0)
        prev2 = pltpu.roll(u, shift=2, axis=0)
        head1 = jnp.where(row < 1, pltpu.roll(u_halo, shift=1, axis=0), prev1[:HALO])
        head2 = jnp.where(row < 2, pltpu.roll(u_halo, shift=2, axis=0), prev2[:HALO])
        prev1 = jnp.concatenate([head1, prev1[HALO:]], axis=0)
        prev2 = jnp.concatenate([head2, prev2[HALO:]], axis=0)
        u_ref[0] = u[tm - HALO:]
    else:
        prev2 = p2_ref[...]
        prev1 = p1_ref[...]
        u_ref[...] = u
    uc = cb_ref[...] + cw_ref[0:1, :] * prev2 + cw_ref[1:2, :] * prev1 + cw_ref[2:3, :] * u
    act = (_gelu_tanh(uc) * gate).astype(BF)
    part = _dot(act, wd_ref[...])

    @pl.when(f == 0)
    def _():
        acc_ref[...] = part

    @pl.when(f > 0)
    def _():
        acc_ref[...] += part

    @pl.when(f == N_TF - 1)
    def _():
        y_ref[...] = _layer_norm(DN_ALPHA * h_ref[...] + acc_ref[...], g_ref[...], b_ref[...])


def _ffn(h, h_bf, wu_bf, cw, cb, wd_bf, g, b, prev2, prev1, tm, seq_rows, tag):
    rows = h.shape[0]
    n_halo = tm // HALO
    if seq_rows:
        prev2 = prev1 = jnp.zeros((HALO, TF), F32)
        cs_spec = pl.BlockSpec((HALO, TF), lambda i, f: (0, 0))
        halo_spec = pl.BlockSpec((HALO, D_MODEL), lambda i, f: (jnp.maximum(i * n_halo - 1, 0), 0))
        u_shape = jax.ShapeDtypeStruct((rows // tm, HALO, FFN_DIM), F32)
        u_spec = pl.BlockSpec((1, HALO, TF), lambda i, f: (i, 0, f))
    else:
        cs_spec = pl.BlockSpec((tm, TF), lambda i, f: (i, f))
        halo_spec = pl.BlockSpec((HALO, D_MODEL), lambda i, f: (0, 0))
        u_shape = jax.ShapeDtypeStruct((rows, FFN_DIM), F32)
        u_spec = pl.BlockSpec((tm, TF), lambda i, f: (i, f))
    row = lambda: pl.BlockSpec((tm, D_MODEL), lambda i, f: (i, 0))
    vec = lambda: pl.BlockSpec((1, D_MODEL), lambda i, f: (0, 0))
    return pl.pallas_call(
        functools.partial(_ffn_kernel, seq_rows=seq_rows),
        grid=(rows // tm, N_TF),
        in_specs=[row(), halo_spec, row(),
                  pl.BlockSpec((D_MODEL, TF), lambda i, f: (0, f)),
                  pl.BlockSpec((D_MODEL, TF), lambda i, f: (0, N_TF + f)),
                  pl.BlockSpec((3, TF), lambda i, f: (0, f)),
                  pl.BlockSpec((1, TF), lambda i, f: (0, f)),
                  pl.BlockSpec((TF, D_MODEL), lambda i, f: (f, 0)),
                  vec(), vec(), cs_spec, cs_spec],
        out_specs=[row(), u_spec],
        out_shape=[jax.ShapeDtypeStruct((rows, D_MODEL), F32), u_shape],
        scratch_shapes=[pltpu.VMEM((tm, D_MODEL), F32)],
        compiler_params=_params(2),
        name="conv_ffn_" + tag,
    )(h_bf, h_bf, h, wu_bf, wu_bf, cw, cb, wd_bf, g, b, prev2, prev1)


def _rowdot(mat, vec):
    return jnp.sum(mat * vec, axis=1, keepdims=True)


def _moba_sample_kernel(pt_ref, slopes_ref, q_ref, kn_ref, vn_ref, *refs):
    del pt_ref
    k_pages, v_pages, o_ref = refs[:N_PAGES], refs[N_PAGES:2 * N_PAGES], refs[2 * N_PAGES]
    nb = PAST_LEN // MOBA_BLOCK
    ppb = MOBA_BLOCK // PAGE
    off = lax.broadcasted_iota(jnp.int32, (MOBA_BLOCK, 1), 0)
    for h in range(MOBA_HEADS):
        sl = slice(h * HD, (h + 1) * HD)
        q = q_ref[0, :, sl]
        block = lambda pages, j: jnp.concatenate([pages[j * ppb + i][h] for i in range(ppb)], axis=0)
        k_blk = [block(k_pages, j) for j in range(nb)]
        sc = [_rowdot(jnp.sum(kb, axis=0, keepdims=True) / MOBA_BLOCK, q) for kb in k_blk]
        s_own = _rowdot(kn_ref[0, :, sl], q) * SCALE
        s_blk, m = [], s_own
        for j in range(nb):
            beats = [(sc[i] > sc[j]) | ((sc[i] == sc[j]) & (i < j)) for i in range(nb) if i != j]
            rank = sum(b.astype(F32) for b in beats)
            dist = (PAST_LEN - (j * MOBA_BLOCK + off)).astype(F32)
            s = _rowdot(k_blk[j], q) * SCALE - slopes_ref[h] * dist
            s = jnp.where(rank < MOBA_TOPK, s, -jnp.inf)
            s_blk.append(s)
            m = jnp.maximum(m, jnp.max(s, axis=0, keepdims=True))
        p_own = jnp.exp(s_own - m)
        den = p_own
        acc = p_own * vn_ref[0, :, sl]
        for j in range(nb):
            p = jnp.exp(s_blk[j] - m)
            den = den + jnp.sum(p, axis=0, keepdims=True)
            acc = acc + jnp.sum(p * block(v_pages, j), axis=0, keepdims=True)
        o_ref[0, :, sl] = acc / den


def _moba_sample(page_table, slopes, q, k_new, v_new, k_pool, v_pool):
    row = lambda: pl.BlockSpec((1, 1, MOBA_W), lambda n, pt: (n, 0, 0))
    page = lambda p: pl.BlockSpec((None, MOBA_HEADS, PAGE, HD), lambda n, pt: (pt[n, p], 0, 0, 0))
    return pl.pallas_call(
        _moba_sample_kernel,
        grid_spec=pltpu.PrefetchScalarGridSpec(
            num_scalar_prefetch=1, grid=(DEC_BATCH,),
            in_specs=[pl.BlockSpec(memory_space=pltpu.SMEM), row(), row(), row()]
            + [page(p) for p in range(N_PAGES)] * 2,
            out_specs=row()),
        out_shape=jax.ShapeDtypeStruct((DEC_BATCH, 1, MOBA_W), F32),
        compiler_params=_params(1),
        name="moba_sample",
    )(page_table, slopes, q, k_new, v_new, *([k_pool] * N_PAGES), *([v_pool] * N_PAGES))


NS = 8
ROWS_PER_PAGE = PAGE * NSA_G
N_SEL = NSA_TOPN - 1
N_CAND_S = PAST_LEN // SEL_BLOCK


def _nsa_sample_cmp_kernel(pt_ref, slopes_ref, kcn_ref, vcn_ref, q_ref, pe_ref, w1_ref, w2_ref, cmat_ref, tri_ref,
                           *refs):
    del pt_ref
    k_pages, v_pages = refs[:N_PAGES], refs[N_PAGES:2 * N_PAGES]
    ocmp_ref, idx_ref, stage_ref = refs[2 * N_PAGES:2 * N_PAGES + 3]
    s_in = pl.program_id(1)
    unit = N_CHUNK
    chunks_per_page = PAGE // CMP_STRIDE

    for which, pages in enumerate((k_pages, v_pages)):
        for g in range(NSA_G):
            base = pl.multiple_of(s_in * (NSA_G * unit) + g * unit, unit)
            for r in range(CMP_STRIDE):
                x = jnp.concatenate(
                    [p[pl.ds(NSA_G * r + g, chunks_per_page, stride=NSA_G * CMP_STRIDE), :] for p in pages], axis=0)
                stage_ref[which, pl.ds(base, unit), r * HD:(r + 1) * HD] = x.astype(BF)

    @pl.when(s_in == NS - 1)
    def _():
        n_unit = NS * NSA_G
        rows = n_unit * unit
        row = lax.broadcasted_iota(jnp.int32, (rows, HD), 0)
        toks = []
        for which, new_ref in enumerate((kcn_ref, vcn_ref)):
            w1 = w1_ref.at[which]
            acc = _dot(stage_ref[which], w1[...])
            top, bot = acc[:, :HD], acc[:, HD:]
            nxt = pltpu.roll(bot, shift=rows - 1, axis=0)
            new_rows = jnp.concatenate([new_ref[s:s + 1, g * HD:(g + 1) * HD]
                                        for s in range(NS) for g in range(NSA_G)], axis=0)
            b_new = _dot(new_rows.astype(BF), w1[0:HD, HD:])
            b_last = jnp.concatenate([jnp.broadcast_to(b_new[u:u + 1], (unit, HD)) for u in range(n_unit)], axis=0)
            c_row = jnp.zeros((1, HD), F32)
            for r in range(CMP_LEN):
                half, rr = divmod(r, CMP_STRIDE)
                c_row = c_row + _dot(pe_ref[which, :, r * HD:(r + 1) * HD],
                                     w1[rr * HD:(rr + 1) * HD, half * HD:(half + 1) * HD])
            pre = top + jnp.where(row % unit == unit - 1, b_last, nxt) + c_row
            toks.append(_dot(_gelu_tanh(pre).astype(BF), w2_ref[which]))
        ck_all, cv_all = toks

        lane = lax.broadcasted_iota(jnp.int32, (8, HD), 1)
        row8 = lax.broadcasted_iota(jnp.int32, (8, HD), 0)
        dist = PAST_LEN - (lane * CMP_STRIDE + CMP_LEN - 1)
        p_rows = []
        for s in range(NS):
            for g in range(NSA_G):
                u = s * NSA_G + g
                ck_hi, ck_lo = _split_bf16(ck_all[u * unit:(u + 1) * unit])
                cv_bf = cv_all[u * unit:(u + 1) * unit].astype(BF)
                q8 = jnp.concatenate([q_ref[s:s + 1, (g * NSA_R + r) * HD:(g * NSA_R + r + 1) * HD]
                                      for r in range(NSA_R)] + [jnp.zeros((8 - NSA_R, HD), F32)], axis=0)
                slope = jnp.where(row8[:, 0:1] == 0, slopes_ref[g * NSA_R],
                                  jnp.where(row8[:, 0:1] == 1, slopes_ref[g * NSA_R + 1], slopes_ref[g * NSA_R + 2]))
                o8, p8 = _cmp_attend(q8, ck_hi, ck_lo, cv_bf, slope, dist)
                ocmp_ref[s, g] = o8
                p_rows.append(jnp.sum(jnp.where(row8 < NSA_R, p8, 0.0), axis=0, keepdims=True))
        p_grp = jnp.concatenate(p_rows + [jnp.zeros((HD - n_unit, HD), F32)], axis=0)
        member = _select_blocks(p_grp, cmat_ref[...], jnp.full((EXT_MASK, HD), N_CAND_S, jnp.int32), N_CAND_S, N_SEL)
        member_pad = jnp.concatenate([member, jnp.zeros((HD - EXT_MASK, HD), F32)], axis=0)
        rank = _dot(tri_ref[...], member_pad.astype(BF))[:EXT_MASK]
        cand = lax.broadcasted_iota(jnp.int32, (EXT_MASK, HD), 0).astype(F32)
        picks = [jnp.sum(jnp.where((member > 0.5) & (rank == c), cand, 0.0), axis=0, keepdims=True)
                 for c in range(N_SEL)]
        idx_ref[0] = jnp.concatenate(picks + [jnp.zeros((1, HD), F32)], axis=0).astype(jnp.int32)


def _nsa_sample_cmp(page_table, slopes, kc_new, vc_new, q, pe_bf, w1_flat, w2_bf, cmat, tri, ck_pool, cv_pool):
    page = lambda p: pl.BlockSpec((ROWS_PER_PAGE, HD), lambda i, s, pt: (pt[i * NS + s, p], 0))
    full = lambda a: pl.BlockSpec(a.shape, lambda i, s, pt: (0,) * a.ndim)
    rows = lambda w: pl.BlockSpec((NS, w), lambda i, s, pt: (i, 0))
    out = lambda: pl.BlockSpec((NS, NSA_G, 8, HD), lambda i, s, pt: (i, 0, 0, 0))
    return pl.pallas_call(
        _nsa_sample_cmp_kernel,
        grid_spec=pltpu.PrefetchScalarGridSpec(
            num_scalar_prefetch=1, grid=(DEC_BATCH // NS, NS),
            in_specs=[pl.BlockSpec(memory_space=pltpu.SMEM), rows(KV_W), rows(KV_W), rows(NSA_W),
                      full(pe_bf), full(w1_flat), full(w2_bf), full(cmat), full(tri)]
            + [page(p) for p in range(N_PAGES)] * 2,
            out_specs=[out(), pl.BlockSpec((1, N_SEL + 1, HD), lambda i, s, pt: (i, 0, 0))],
            scratch_shapes=[pltpu.VMEM((2, NS * NSA_G * N_CHUNK, CMP_STRIDE * HD), BF)]),
        out_shape=[jax.ShapeDtypeStruct((DEC_BATCH, NSA_G, 8, HD), F32),
                   jax.ShapeDtypeStruct((DEC_BATCH // NS, N_SEL + 1, HD), jnp.int32)],
        compiler_params=_params(2),
        name="nsa_sample_cmp",
    )(page_table, slopes, kc_new, vc_new, q, pe_bf, w1_flat, w2_bf, cmat, tri,
      *([ck_pool] * N_PAGES), *([cv_pool] * N_PAGES))


def _attend_rows(q, slope, k, v, dist, k_own, v_own):
    s = _rowdot(k, q) * SCALE - slope * dist
    s_own = _rowdot(k_own, q) * SCALE
    m = jnp.maximum(jnp.max(s, axis=0, keepdims=True), s_own)
    p = jnp.exp(s - m)
    p_own = jnp.exp(s_own - m)
    den = jnp.sum(p, axis=0, keepdims=True) + p_own
    return (jnp.sum(p * v, axis=0, keepdims=True) + p_own * v_own) / den


BLK_ROWS = SEL_BLOCK * NSA_G


def _nsa_sample_attend_kernel(pt_ref, ix_ref, slopes_ref, q_ref, gn_ref, ksn_ref, vsn_ref, kwn_ref, vwn_ref,
                              ocmp_ref, wk_ref, wv_ref, sk_hbm, sv_hbm, o_ref, kbuf, vbuf, sem):
    n = pl.program_id(0)
    halves = PAGE // SEL_BLOCK

    def block_copies(sample, slot):
        copies = []
        for g in range(NSA_G):
            for c in range(N_SEL):
                j = ix_ref[(sample * NSA_G + g) * N_SEL + c]
                row0 = pl.multiple_of((pt_ref[sample, j // halves] * halves + j % halves) * BLK_ROWS, BLK_ROWS)
                dst = pl.ds((g * N_SEL + c) * BLK_ROWS, BLK_ROWS)
                for pool, buf in ((sk_hbm, kbuf), (sv_hbm, vbuf)):
                    copies.append(pltpu.make_async_copy(pool.at[pl.ds(row0, BLK_ROWS), :], buf.at[slot, dst, :],
                                                        sem.at[slot]))
        return copies

    @pl.when(n == 0)
    def _():
        for cp in block_copies(0, 0):
            cp.start()

    @pl.when(n + 1 < DEC_BATCH)
    def _():
        for cp in block_copies(n + 1, (n + 1) % 2):
            cp.start()

    slot = n % 2
    for cp in block_copies(n, slot):
        cp.wait()

    wl = wk_ref.shape[0] // NSA_G
    off = lax.broadcasted_iota(jnp.int32, (SEL_BLOCK, 1), 0)
    dist_w = (wl - lax.broadcasted_iota(jnp.int32, (wl, 1), 0)).astype(F32)
    gates = jax.nn.sigmoid(gn_ref[0])
    for g in range(NSA_G):
        gsl = slice(g * HD, (g + 1) * HD)
        sel_rows = pl.ds(g * N_SEL * BLK_ROWS + g, N_SEL * SEL_BLOCK, stride=NSA_G)
        k_sel = kbuf[slot, sel_rows, :]
        v_sel = vbuf[slot, sel_rows, :]
        pos = jnp.concatenate([ix_ref[(n * NSA_G + g) * N_SEL + c] * SEL_BLOCK + off for c in range(N_SEL)], axis=0)
        dist_s = (PAST_LEN - pos).astype(F32)
        k_win = wk_ref[pl.ds(g, wl, stride=NSA_G), :]
        v_win = wv_ref[pl.ds(g, wl, stride=NSA_G), :]
        for r in range(NSA_R):
            hd = g * NSA_R + r
            hsl = slice(hd * HD, (hd + 1) * HD)
            q = q_ref[0, :, hsl]
            slope = slopes_ref[hd]
            o_sel = _attend_rows(q, slope, k_sel, v_sel, dist_s, ksn_ref[0, :, gsl], vsn_ref[0, :, gsl])
            o_win = _attend_rows(q, slope, k_win, v_win, dist_w, kwn_ref[0, :, gsl], vwn_ref[0, :, gsl])
            o_cmp = ocmp_ref[0, g, r:r + 1, :]
            o_ref[0, :, hsl] = (gates[:, 3 * hd:3 * hd + 1] * o_cmp + gates[:, 3 * hd + 1:3 * hd + 2] * o_sel
                                + gates[:, 3 * hd + 2:3 * hd + 3] * o_win)


def _nsa_sample_attend(page_table, sel_idx, slopes, q, gn, ks_new, vs_new, kw_new, vw_new, o_cmp,
                       win_k, win_v, sk_pool, sv_pool):
    row = lambda w: pl.BlockSpec((1, 1, w), lambda n, pt, ix: (n, 0, 0))
    win_rows = win_k.shape[0] // DEC_BATCH
    win = lambda: pl.BlockSpec((win_rows, HD), lambda n, pt, ix: (n, 0))
    any_spec = pl.BlockSpec(memory_space=pl.ANY)
    buf = pltpu.VMEM((2, NSA_G * N_SEL * BLK_ROWS, HD), F32)
    return pl.pallas_call(
        _nsa_sample_attend_kernel,
        grid_spec=pltpu.PrefetchScalarGridSpec(
            num_scalar_prefetch=2, grid=(DEC_BATCH,),
            in_specs=[pl.BlockSpec(memory_space=pltpu.SMEM), row(NSA_W), row(HD), row(KV_W), row(KV_W), row(KV_W),
                      row(KV_W), pl.BlockSpec((1, NSA_G, 8, HD), lambda n, pt, ix: (n, 0, 0, 0)), win(), win(),
                      any_spec, any_spec],
            out_specs=row(NSA_W),
            scratch_shapes=[buf, buf, pltpu.SemaphoreType.DMA((2,))]),
        out_shape=jax.ShapeDtypeStruct((DEC_BATCH, 1, NSA_W), F32),
        compiler_params=_params(1),
        name="nsa_sample_attend",
    )(page_table, sel_idx, slopes, q, gn, ks_new, vs_new, kw_new, vw_new, o_cmp, win_k, win_v, sk_pool, sv_pool)


def _mem_sample_kernel(q_ref, k_ref, v_ref, o_ref):
    for h in range(MEM_HEADS):
        sl = slice(h * HD, (h + 1) * HD)
        q = q_ref[0, :, sl]
        k = k_ref[pl.ds(h, MEM_LEN, stride=MEM_HEADS), :]
        v = v_ref[pl.ds(h, MEM_LEN, stride=MEM_HEADS), :]
        s = _rowdot(k, q) * SCALE
        p = jnp.exp(s - jnp.max(s, axis=0, keepdims=True))
        o_ref[0, :, sl] = jnp.sum(p * v, axis=0, keepdims=True) / jnp.sum(p, axis=0, keepdims=True)


def _mem_sample(q, mem_k, mem_v):
    rows = MEM_LEN * MEM_HEADS
    row = lambda: pl.BlockSpec((1, 1, MEM_W), lambda n: (n, 0, 0))
    kv = lambda: pl.BlockSpec((rows, HD), lambda n: (n, 0))
    return pl.pallas_call(
        _mem_sample_kernel,
        grid=(DEC_BATCH,),
        in_specs=[row(), kv(), kv()],
        out_specs=row(),
        out_shape=jax.ShapeDtypeStruct((DEC_BATCH, 1, MEM_W), F32),
        compiler_params=_params(1),
        name="mem_attend_sample",
    )(q, mem_k, mem_v)


def _project_all(x_bf, w, tm, paged, tag):
    kv_kind = "pages" if paged else "flat"
    qa, ka, va, qb = _proj(x_bf, w["wa"], w["ba"], MOBA_W, ["flat", kv_kind, kv_kind, "flat"], tm,
                           "proj_moba_q_" + tag, True)
    kc, vc, ks, vs, kw, vw = _proj(x_bf, w["wb"], w["bb"], KV_W, ["flat"] * 6, tm, "proj_nsa_kv_" + tag, True)
    (qm,) = _proj(x_bf, w["wm"], w["bm"], MEM_W, ["flat"], tm, "proj_mem_q_" + tag, True)
    (gn,) = _proj(x_bf, w["wg"], w["bg"], HD, ["flat"], tm, "proj_nsa_gate_" + tag, True)
    gbr = _matmul(x_bf, w["wd"], w["bd"], tm, 768, "proj_branch_gate_" + tag)
    return qa, ka, va, qb, kc, vc, ks, vs, kw, vw, qm, gn, gbr


def _strict_lower_ones():
    return jnp.asarray(np.tril(np.ones((HD, HD), np.float32), -1), BF)


def kernel(x_prompt, x_sample, cache_mem_k, cache_mem_v, cache_moba_k, cache_moba_v, cache_nsa_cmp_k, cache_nsa_cmp_v, cache_nsa_sel_k, cache_nsa_sel_v, cache_nsa_win_k, cache_nsa_win_v, cache_ffn_conv, page_table, mem_prompt, w_in, b_in, w_mem_kv, cmp_pe, cmp_w1, cmp_w2, p_moba, p_nsa, p_mem, w_o, ln1_g, ln1_b, w_up, conv_w, conv_b, w_down, ln2_g, ln2_b):
    slopes = _alibi_slopes(MOBA_HEADS)
    w_bf = w_in[0].T.astype(BF)
    b0 = b_in[0][None, :]
    c0, c1, c2, c3 = 4 * MOBA_W, 4 * MOBA_W + 6 * KV_W, 4 * MOBA_W + 6 * KV_W + N_GATE, 4 * MOBA_W + 6 * KV_W + N_GATE + MEM_W
    w = {"wa": w_bf[:c0], "ba": b0[:, :c0], "wb": w_bf[c0:c1], "bb": b0[:, c0:c1],
         "wg": jnp.pad(w_bf[c1:c2], ((0, HD - N_GATE), (0, 0))), "bg": jnp.pad(b0[:, c1:c2], ((0, 0), (0, HD - N_GATE))),
         "wm": w_bf[c2:c3], "bm": b0[:, c2:c3], "wd": w_bf[c3:], "bd": b0[:, c3:]}
    pe_bf, w1_bf, w2_bf = _pack_cmp_weights(cmp_pe[0], cmp_w1[0], cmp_w2[0])
    pa_bf, pb_bf, pm_bf, wo_bf = p_moba[0].astype(BF), p_nsa[0].astype(BF), p_mem[0].astype(BF), w_o[0].astype(BF)
    wu_bf, wd_bf = w_up[0].astype(BF), w_down[0].astype(BF)

    xp = x_prompt.reshape(BATCH * SEQ, D_MODEL)
    xp_bf = xp.astype(BF)
    qa, ka, va, qb, kc, vc, ks, vs, kw, vw, qm, gn, gbr = _project_all(xp_bf, w, 1024, True, "prompt")
    mem_k, mem_v = _proj(mem_prompt.reshape(BATCH * MEM_LEN, D_MODEL).astype(BF), w_mem_kv[0].astype(BF),
                         jnp.zeros((1, 2 * MEM_W), F32), MEM_W, ["flat", "flat"], 512, "proj_mem_kv")
    slope_rows = _slope_rows(MOBA_HEADS)
    o_a = _moba_prompt(slope_rows, qa, ka, va)
    ck, cv = _compress_prompt(kc, vc, pe_bf, w1_bf, w2_bf)
    cmat = _cmp_to_sel_matrix(N_CHUNK - 1, SEQ // SEL_BLOCK - 1)
    o_b = _nsa_prompt(slopes, slope_rows, qb, gn, ks, vs, kw, vw, ck, cv, cmat)
    o_m = _mem_attend_prompt(qm, mem_k, mem_v)
    merged = _merge(o_a, o_b, o_m, gbr, pa_bf, pb_bf, pm_bf, 256, "prompt")
    h, h_bf = _out_ln(xp, merged, wo_bf, ln1_g, ln1_b, 512, "prompt")
    y_p, u_tail = _ffn(h, h_bf, wu_bf, conv_w[0], conv_b, wd_bf, ln2_g, ln2_b, None, None, 512, SEQ, "prompt")

    pages = lambda a: a.reshape(1, BATCH, SEQ // PAGE, PAGE, NSA_G, HD)
    head_pages = lambda a: a.transpose(0, 1, 3, 2, 4)[None]
    last = lambda a: a.reshape(BATCH, SEQ, NSA_G, HD)[None, :, SEQ - WINDOW:]
    tiles_per_seq = SEQ // 512
    conv_p = u_tail.reshape(BATCH, tiles_per_seq, HALO, FFN_DIM)[:, -1, HALO - 2:][None]
    prompt_out = (y_p.reshape(BATCH, SEQ, D_MODEL),
                  mem_k.reshape(1, BATCH, MEM_LEN, MEM_HEADS, HD), mem_v.reshape(1, BATCH, MEM_LEN, MEM_HEADS, HD),
                  head_pages(ka), head_pages(va), pages(kc), pages(vc), pages(ks), pages(vs), last(kw), last(vw), conv_p)

    n = DEC_BATCH
    xs = x_sample.reshape(n, D_MODEL)
    qa, ka, va, qb, kc, vc, ks, vs, kw, vw, qm, gn, gbr = _project_all(xs.astype(BF), w, n, False, "sample")
    r3 = lambda a: a.reshape(n, 1, a.shape[-1])
    moba_pool = lambda c: c[0].transpose(0, 2, 1, 3)
    rows2d = lambda c: c.reshape(-1, HD)
    o_a = _moba_sample(page_table, slopes, r3(qa), r3(ka), r3(va), moba_pool(cache_moba_k), moba_pool(cache_moba_v))
    w1_flat = w1_bf.reshape(2, CMP_STRIDE * HD, 2 * HD)
    cmat_s = _cmp_to_sel_matrix(N_CHUNK, N_CAND_S)
    o_cmp, sel = _nsa_sample_cmp(page_table, slopes, kc, vc, qb, pe_bf, w1_flat, w2_bf, cmat_s, _strict_lower_ones(),
                                 rows2d(cache_nsa_cmp_k), rows2d(cache_nsa_cmp_v))
    sel_idx = sel[:, :N_SEL, :NS * NSA_G].transpose(0, 2, 1).reshape(-1)
    o_b = _nsa_sample_attend(page_table, sel_idx, slopes, r3(qb), r3(gn), r3(ks), r3(vs), r3(kw), r3(vw), o_cmp,
                             rows2d(cache_nsa_win_k), rows2d(cache_nsa_win_v),
                             rows2d(cache_nsa_sel_k), rows2d(cache_nsa_sel_v))
    o_m = _mem_sample(r3(qm), rows2d(cache_mem_k), rows2d(cache_mem_v))
    flat_bf = lambda a: a.reshape(n, a.shape[-1]).astype(BF)
    merged = _merge(flat_bf(o_a), flat_bf(o_b), flat_bf(o_m), gbr, pa_bf, pb_bf, pm_bf, n, "sample")
    h, h_bf = _out_ln(xs, merged, wo_bf, ln1_g, ln1_b, n, "sample")
    conv_old = cache_ffn_conv[0]
    y_s, u_new = _ffn(h, h_bf, wu_bf, conv_w[0], conv_b, wd_bf, ln2_g, ln2_b, conv_old[:, 0], conv_old[:, 1], n, 0,
                      "sample")

    heads = lambda a, nh: a.reshape(1, n, 1, nh, HD)
    roll_in = lambda cache, new: jnp.concatenate([cache[:, :, 1:], new.reshape(1, n, 1, NSA_G, HD)], axis=2)
    sample_out = (heads(ka, MOBA_HEADS), heads(va, MOBA_HEADS), heads(kc, NSA_G), heads(vc, NSA_G),
                  heads(ks, NSA_G), heads(vs, NSA_G), roll_in(cache_nsa_win_k, kw), roll_in(cache_nsa_win_v, vw),
                  jnp.stack([conv_old[:, 1], u_new], axis=1)[None])
    return (prompt_out[0], y_s.reshape(n, 1, D_MODEL)) + prompt_out[1:] + sample_out
```

```python
import functools

import numpy as np
import jax
import jax.numpy as jnp
from jax import lax
from jax.experimental import pallas as pl
from jax.experimental.pallas import tpu as pltpu

D_MODEL = 2048
BATCH = 4
SEQ = 2048
DEC_BATCH = 128
PAST_LEN = 2048
PAGE = 128
N_PAGES = PAST_LEN // PAGE
HD = 128
MOBA_HEADS = 6
MOBA_BLOCK = 256
MOBA_TOPK = 3
NSA_HEADS = 6
NSA_G = 2
NSA_R = 3
CMP_LEN = 32
CMP_STRIDE = 16
SEL_BLOCK = 64
NSA_TOPN = 16
WINDOW = 512
MEM_HEADS = 4
MEM_LEN = 256
FFN_DIM = 5632
DN_ALPHA = 2.0 ** 0.25
LN_EPS = 1e-5
SCALE = HD ** -0.5
MOBA_W = MOBA_HEADS * HD
NSA_W = NSA_HEADS * HD
KV_W = NSA_G * HD
MEM_W = MEM_HEADS * HD
N_GATE = NSA_HEADS * 3

F32 = jnp.float32
BF = jnp.bfloat16
NEG_BIG = -(2.0 ** 100)
VMEM_LIMIT = 48 * 1024 * 1024
TQ = 256


def _dot(a, b):
    return jnp.dot(a, b, preferred_element_type=F32)


def _dot_nt(a, b):
    return lax.dot_general(a, b, (((1,), (1,)), ((), ())), preferred_element_type=F32)


def _split_bf16(a):
    hi = a.astype(BF)
    lo = (a - hi.astype(F32)).astype(BF)
    return hi, lo


def _alibi_slopes(n):
    return jnp.asarray(2.0 ** (-8.0 * np.arange(1, n + 1) / n), F32)


def _gelu_tanh(x):
    return 0.5 * x * (1.0 + jnp.tanh(np.sqrt(2.0 / np.pi).astype(np.float32) * (x + 0.044715 * (x * x * x))))


def _params(n_axes):
    return pltpu.CompilerParams(dimension_semantics=("arbitrary",) * n_axes, vmem_limit_bytes=VMEM_LIMIT)


def _proj_kernel(x_ref, w_ref, b_ref, *out_refs, kinds, w_rows):
    j = pl.program_id(1)
    acc = (_dot_nt if w_rows else _dot)(x_ref[...], w_ref[...]) + b_ref[...]
    for idx, (kind, o_ref) in enumerate(zip(kinds, out_refs)):
        @pl.when(j == idx)
        def _(kind=kind, o_ref=o_ref):
            if kind == "flat":
                o_ref[...] = acc
            else:
                for p in range(o_ref.shape[1]):
                    for h in range(o_ref.shape[2]):
                        o_ref[0, p, h] = acc[p * PAGE:(p + 1) * PAGE, h * HD:(h + 1) * HD]


def _proj(x_bf, w_bf, b, tn, kinds, tm, name, w_rows=False):
    rows, k = x_bf.shape
    n_out = len(kinds)
    assert w_bf.shape == ((n_out * tn, k) if w_rows else (k, n_out * tn)) and rows % tm == 0
    w_spec = pl.BlockSpec((tn, k), lambda i, j: (j, 0)) if w_rows else pl.BlockSpec((k, tn), lambda i, j: (0, j))
    tiles_per_batch = SEQ // tm if tm <= SEQ else 1
    out_shape, out_specs = [], []
    for idx, kind in enumerate(kinds):
        if kind == "flat":
            out_shape.append(jax.ShapeDtypeStruct((rows, tn), F32))
            out_specs.append(pl.BlockSpec((tm, tn), lambda i, j: (i, 0)))
        else:
            heads = tn // HD
            out_shape.append(jax.ShapeDtypeStruct((rows // SEQ, SEQ // PAGE, heads, PAGE, HD), F32))
            out_specs.append(pl.BlockSpec((1, tm // PAGE, heads, PAGE, HD),
                                          lambda i, j: (i // tiles_per_batch, i % tiles_per_batch, 0, 0, 0)))
    return pl.pallas_call(
        functools.partial(_proj_kernel, kinds=tuple(kinds), w_rows=w_rows),
        grid=(rows // tm, n_out),
        in_specs=[pl.BlockSpec((tm, k), lambda i, j: (i, 0)), w_spec,
                  pl.BlockSpec((1, tn), lambda i, j: (0, j))],
        out_specs=out_specs,
        out_shape=out_shape,
        compiler_params=_params(2),
        name=name,
    )(x_bf, w_bf, b)


def _matmul_kernel(x_ref, w_ref, b_ref, o_ref):
    o_ref[...] = _dot_nt(x_ref[...], w_ref[...]) + b_ref[...]


def _matmul(x_bf, w_bf, b, tm, tn, name):
    rows, k = x_bf.shape
    n = w_bf.shape[0]
    return pl.pallas_call(
        _matmul_kernel,
        grid=(rows // tm, n // tn),
        in_specs=[pl.BlockSpec((tm, k), lambda i, j: (i, 0)),
                  pl.BlockSpec((tn, k), lambda i, j: (j, 0)),
                  pl.BlockSpec((1, tn), lambda i, j: (0, j))],
        out_specs=pl.BlockSpec((tm, tn), lambda i, j: (i, j)),
        out_shape=jax.ShapeDtypeStruct((rows, n), F32),
        compiler_params=_params(2),
        name=name,
    )(x_bf, w_bf, b)


def _topk_member(v, cand, n_cand, k):
    member = jnp.zeros(v.shape, F32)
    for j in range(n_cand):
        vj = v[j:j + 1, :]
        beats = (v > vj) | ((v == vj) & (cand < j))
        cnt = jnp.sum(beats.astype(F32), axis=0, keepdims=True)
        member = jnp.where((cand == j) & (cnt < k), 1.0, member)
    return member


def _to_rows(member_t):
    pad = jnp.zeros((HD - member_t.shape[0], member_t.shape[1]), F32)
    return jnp.transpose(jnp.concatenate([member_t, pad], axis=0))


EXT_MASK = 32
N_PIECE = 3
POS_SPLIT = 64
EXP2_SCALE = float(SCALE * np.log2(np.e))


def _slope_rows(n_heads):
    slope = np.asarray(2.0 ** (-8.0 * np.arange(1, n_heads + 1) / n_heads), np.float32) / np.float32(SCALE)
    rows = np.zeros((8, HD), np.float32)
    rest = slope.astype(np.float32)
    for i in range(N_PIECE):
        piece = rest.astype(BF).astype(np.float32)
        rows[:n_heads, EXT_MASK + i] = piece * POS_SPLIT
        rows[:n_heads, EXT_MASK + N_PIECE + i] = piece
        rest = rest - piece
    return jnp.asarray(rows)


def _key_ext(n_keys, block):
    key = lax.broadcasted_iota(jnp.int32, (n_keys, HD), 0)
    lane = lax.broadcasted_iota(jnp.int32, (n_keys, HD), 1)
    hot = jnp.where(key // block == lane, 1.0, 0.0)
    pos = jnp.where(lane < EXT_MASK + N_PIECE, (key // POS_SPLIT).astype(F32), (key % POS_SPLIT).astype(F32))
    return jnp.where(lane < EXT_MASK, hot, jnp.where(lane < EXT_MASK + 2 * N_PIECE, pos, 0.0)).astype(BF)


def _query_ext(allowed, slope_row, lane):
    mask = 0.0 if allowed is None else jnp.where(allowed, 0.0, NEG_BIG)
    return jnp.where(lane < EXT_MASK, mask, slope_row).astype(BF)


def _flash_init(qx, kx, v_bf, keep):
    s = jnp.where(keep, _dot_nt(qx, kx), NEG_BIG)
    m = jnp.max(s, axis=1, keepdims=True)
    p = jnp.exp2((s - m) * EXP2_SCALE)
    return m, jnp.sum(p, axis=1, keepdims=True), _dot(p.astype(BF), v_bf)


def _flash_step(state, qx, kx, v_bf, keep=None):
    m_old, l_old, acc_old = state
    s = _dot_nt(qx, kx)
    if keep is not None:
        s = jnp.where(keep, s, NEG_BIG)
    m_new = jnp.maximum(m_old, jnp.max(s, axis=1, keepdims=True))
    alpha = jnp.exp2((m_old - m_new) * EXP2_SCALE)
    p = jnp.exp2((s - m_new) * EXP2_SCALE)
    return (m_new, alpha * l_old + jnp.sum(p, axis=1, keepdims=True),
            alpha * acc_old + _dot(p.astype(BF), v_bf))


MOBA_HB = 3


def _moba_prompt_kernel(srow_ref, q_ref, k_ref, v_ref, o_ref, kx_ref, vb_ref, kmh_ref, kml_ref):
    hg = pl.program_id(1)
    qi = pl.program_id(2)
    nb = SEQ // MOBA_BLOCK

    @pl.when(qi == 0)
    def _():
        ext = _key_ext(SEQ, MOBA_BLOCK)
        for hh in range(MOBA_HB):
            k = k_ref[:, hh].reshape(SEQ, HD)
            kx_ref[hh, :, :HD] = k.astype(BF)
            kx_ref[hh, :, HD:] = ext
            vb_ref[hh] = v_ref[:, hh].reshape(SEQ, HD).astype(BF)
            km = jnp.sum(k.reshape(nb, MOBA_BLOCK, HD), axis=1) / MOBA_BLOCK
            hi, lo = _split_bf16(jnp.concatenate([km, jnp.zeros((HD - nb, HD), F32)], axis=0))
            kmh_ref[hh] = hi
            kml_ref[hh] = lo

    lane = lax.broadcasted_iota(jnp.int32, (TQ, HD), 1)
    cand = lax.broadcasted_iota(jnp.int32, (nb, TQ), 0)
    rr = lax.broadcasted_iota(jnp.int32, (TQ, MOBA_BLOCK), 0)
    cc = lax.broadcasted_iota(jnp.int32, (TQ, MOBA_BLOCK), 1)
    own = pl.multiple_of(qi * MOBA_BLOCK, MOBA_BLOCK)
    qx, states = [], []
    for hh in range(MOBA_HB):
        q = q_ref[:, hh * HD:(hh + 1) * HD]
        qh, ql = _split_bf16(q)
        sc = (_dot_nt(kmh_ref[hh], qh) + _dot_nt(kml_ref[hh], qh) + _dot_nt(kmh_ref[hh], ql))[:nb]
        sc = jnp.where(cand < qi, sc, -jnp.inf)
        member = _topk_member(sc, cand, nb - 1, MOBA_TOPK)
        allowed = (_to_rows(member) > 0.5) | (lane == qi)
        ext = _query_ext(allowed, srow_ref[pl.ds(hg * MOBA_HB + hh, 1), :], lane)
        qx.append(jnp.concatenate([qh, ext], axis=1))
        states.append(_flash_init(qx[hh], kx_ref[hh, pl.ds(own, MOBA_BLOCK), :],
                                  vb_ref[hh, pl.ds(own, MOBA_BLOCK), :], cc <= rr))

    def body(c, states):
        rows = pl.ds(pl.multiple_of(c * MOBA_BLOCK, MOBA_BLOCK), MOBA_BLOCK)
        return tuple(_flash_step(states[hh], qx[hh], kx_ref[hh, rows, :], vb_ref[hh, rows, :])
                     for hh in range(MOBA_HB))

    states = lax.fori_loop(0, qi, body, tuple(states))
    for hh in range(MOBA_HB):
        m, l, acc = states[hh]
        o_ref[:, hh * HD:(hh + 1) * HD] = (acc / l).astype(BF)


def _moba_prompt(slope_rows, q, k_pages, v_pages):
    nq = SEQ // TQ
    kv_spec = pl.BlockSpec((None, SEQ // PAGE, MOBA_HB, PAGE, HD), lambda b, h, qi: (b, 0, h, 0, 0))
    q_spec = pl.BlockSpec((TQ, MOBA_HB * HD), lambda b, h, qi: (b * nq + qi, h))
    return pl.pallas_call(
        _moba_prompt_kernel,
        grid=(BATCH, MOBA_HEADS // MOBA_HB, nq),
        in_specs=[pl.BlockSpec((8, HD), lambda b, h, qi: (0, 0)), q_spec, kv_spec, kv_spec],
        out_specs=q_spec,
        out_shape=jax.ShapeDtypeStruct((BATCH * SEQ, MOBA_W), BF),
        scratch_shapes=[pltpu.VMEM((MOBA_HB, SEQ, 2 * HD), BF), pltpu.VMEM((MOBA_HB, SEQ, HD), BF),
                        pltpu.VMEM((MOBA_HB, HD, HD), BF), pltpu.VMEM((MOBA_HB, HD, HD), BF)],
        compiler_params=_params(3),
        name="moba_prompt",
    )(slope_rows, q, k_pages, v_pages)


N_CHUNK = SEQ // CMP_STRIDE


def _compress_tokens(get_x, w1_ref, w2_ref, c_row, b_last):
    acc = jnp.zeros((N_CHUNK, 2 * HD), F32)
    for r in range(CMP_STRIDE):
        acc = acc + _dot(get_x(r).astype(BF), w1_ref[r])
    top, bot = acc[:, :HD], acc[:, HD:]
    nxt = pltpu.roll(bot, shift=N_CHUNK - 1, axis=0)
    row = lax.broadcasted_iota(jnp.int32, (N_CHUNK, HD), 0)
    pre = top + jnp.where(row == N_CHUNK - 1, b_last, nxt) + c_row
    return _dot(_gelu_tanh(pre).astype(BF), w2_ref[...])


def _compress_prompt_kernel(kc_ref, vc_ref, pe_ref, w1_ref, w2_ref, ck_ref, cv_ref):
    for which, (src, dst) in enumerate(((kc_ref, ck_ref), (vc_ref, cv_ref))):
        w1 = w1_ref.at[which]
        c_row = _pe_term(pe_ref, w1, which)
        get_x = lambda r, src=src: src[pl.ds(r, N_CHUNK, stride=CMP_STRIDE), :]
        dst[0, 0] = _compress_tokens(get_x, w1, w2_ref.at[which], c_row, jnp.zeros((1, HD), F32))


def _pe_term(pe_ref, w1, which):
    c_row = jnp.zeros((1, HD), F32)
    for r in range(CMP_LEN):
        half, rr = divmod(r, CMP_STRIDE)
        c_row = c_row + _dot(pe_ref[which, :, r * HD:(r + 1) * HD], w1[rr, :, half * HD:(half + 1) * HD])
    return c_row


def _pack_cmp_weights(cmp_pe, cmp_w1, cmp_w2):
    w1 = cmp_w1.reshape(2, 2, CMP_STRIDE, HD, HD).transpose(0, 2, 3, 1, 4).reshape(2, CMP_STRIDE, HD, 2 * HD)
    return cmp_pe.reshape(2, 1, CMP_LEN * HD).astype(BF), w1.astype(BF), cmp_w2.astype(BF)


def _compress_prompt(kc, vc, pe_bf, w1_bf, w2_bf):
    full = lambda a: pl.BlockSpec(a.shape, lambda b, g: (0,) * a.ndim)
    return pl.pallas_call(
        _compress_prompt_kernel,
        grid=(BATCH, NSA_G),
        in_specs=[pl.BlockSpec((SEQ, HD), lambda b, g: (b, g)), pl.BlockSpec((SEQ, HD), lambda b, g: (b, g)),
                  full(pe_bf), full(w1_bf), full(w2_bf)],
        out_specs=[pl.BlockSpec((1, 1, N_CHUNK, HD), lambda b, g: (b, g, 0, 0))] * 2,
        out_shape=[jax.ShapeDtypeStruct((BATCH, NSA_G, N_CHUNK, HD), F32)] * 2,
        compiler_params=_params(2),
        name="nsa_compress_prompt",
    )(kc, vc, pe_bf, w1_bf, w2_bf)


def _cmp_to_sel_matrix(n_cmp, n_blocks):
    start = np.arange(n_cmp)[:, None] * CMP_STRIDE
    j0 = np.arange(n_blocks)[None, :] * SEL_BLOCK
    ov = np.minimum(start + CMP_LEN, j0 + SEL_BLOCK) - np.maximum(start, j0)
    m = np.zeros((HD, HD), np.float32)
    m[:n_blocks, :n_cmp] = (np.clip(ov, 0, None) / CMP_LEN).T
    return jnp.asarray(m, BF)


def _cmp_attend(q, ck_hi, ck_lo, cv_bf, slope, dist):
    qh, ql = _split_bf16(q)
    s = (_dot_nt(qh, ck_hi) + _dot_nt(qh, ck_lo) + _dot_nt(ql, ck_hi)) * SCALE - slope * dist.astype(F32)
    s = jnp.where(dist >= 0, s, -jnp.inf)
    m = jnp.max(s, axis=1, keepdims=True)
    m = jnp.where(m > -jnp.inf, m, 0.0)
    p = jnp.exp(s - m)
    p = p / jnp.maximum(jnp.sum(p, axis=1, keepdims=True), 1e-30)
    return _dot(p.astype(BF), cv_bf), p


def _select_blocks(p_grp, cmat_bf, cur, n_cand, k):
    ph, plo = _split_bf16(p_grp)
    imp = (_dot_nt(cmat_bf, ph) + _dot_nt(cmat_bf, plo))[:EXT_MASK]
    cand = lax.broadcasted_iota(jnp.int32, imp.shape, 0)
    valid = (cand < cur) & (cand < n_cand)
    forced = valid & ((cand == 0) | (cand == cur - 1))
    v = jnp.where(forced, jnp.inf, jnp.where(valid, imp, -jnp.inf))
    return jnp.where(valid, _topk_member(v, cand, n_cand, k), 0.0)


def _nsa_prompt_kernel(slopes_ref, srow_ref, q_ref, gn_ref, ks_ref, vs_ref, kw_ref, vw_ref, ck_ref, cv_ref, cmat_ref,
                       o_ref, kx_ref, kwx_ref, vsb_ref, vwb_ref):
    g = pl.program_id(1)
    qi = pl.program_id(2)
    chunk = TQ
    rows = NSA_R * TQ

    @pl.when(qi == 0)
    def _():
        ext = _key_ext(SEQ, SEL_BLOCK)
        kx_ref[:, :HD] = ks_ref[...].astype(BF)
        kx_ref[:, HD:] = ext
        kwx_ref[:, :HD] = kw_ref[...].astype(BF)
        kwx_ref[:, HD:] = ext
        vsb_ref[...] = vs_ref[...].astype(BF)
        vwb_ref[...] = vw_ref[...].astype(BF)

    lane = lax.broadcasted_iota(jnp.int32, (TQ, HD), 1)
    t = qi * TQ + lax.broadcasted_iota(jnp.int32, (TQ, HD), 0)
    stack = lambda parts: jnp.concatenate(parts, axis=0)
    q_all = stack([q_ref[:, r * HD:(r + 1) * HD] for r in range(NSA_R)])

    ck_hi, ck_lo = _split_bf16(ck_ref[0, 0])
    slope_col = stack([jnp.full((TQ, 1), slopes_ref[g * NSA_R + r], F32) for r in range(NSA_R)])
    dist_c = stack([t - (lane * CMP_STRIDE + CMP_LEN - 1)] * NSA_R)
    o_cmp, p = _cmp_attend(q_all, ck_hi, ck_lo, cv_ref[0, 0].astype(BF), slope_col, dist_c)
    p_grp = p[0:TQ] + p[TQ:2 * TQ] + p[2 * TQ:3 * TQ]
    cur_t = (qi * TQ + lax.broadcasted_iota(jnp.int32, (EXT_MASK, TQ), 1)) // SEL_BLOCK
    member = _select_blocks(p_grp, cmat_ref[...], cur_t, SEQ // SEL_BLOCK - 1, NSA_TOPN - 1)
    allowed = (_to_rows(member) > 0.5) | (lane == t // SEL_BLOCK)

    srow = [srow_ref[pl.ds(g * NSA_R + r, 1), :] for r in range(NSA_R)]
    q_bf = q_all.astype(BF)
    qx = jnp.concatenate([q_bf, stack([_query_ext(allowed, srow[r], lane) for r in range(NSA_R)])], axis=1)
    qw = jnp.concatenate([q_bf, stack([_query_ext(None, srow[r], lane) for r in range(NSA_R)])], axis=1)
    rr = lax.broadcasted_iota(jnp.int32, (rows, chunk), 0) % TQ
    cc = lax.broadcasted_iota(jnp.int32, (rows, chunk), 1)
    own = pl.ds(pl.multiple_of(qi * chunk, chunk), chunk)
    past = lambda c: pl.ds(pl.multiple_of(c * chunk, chunk), chunk)

    st = _flash_init(qx, kx_ref[own, :], vsb_ref[own, :], cc <= rr)
    st = lax.fori_loop(0, qi, lambda c, st: _flash_step(st, qx, kx_ref[past(c), :], vsb_ref[past(c), :]), st)
    o_sel = st[2] / st[1]

    st = _flash_init(qw, kwx_ref[own, :], vwb_ref[own, :], cc <= rr)

    def win_step(c, st):
        slack = jnp.where(c == qi - WINDOW // chunk, 0, chunk)
        return _flash_step(st, qw, kwx_ref[past(c), :], vwb_ref[past(c), :], rr - cc <= slack)

    st = lax.fori_loop(jnp.maximum(qi - WINDOW // chunk, 0), qi, win_step, st)
    o_win = st[2] / st[1]

    gates = jax.nn.sigmoid(jnp.where(g == 0, gn_ref[:, 0:NSA_R * 3], gn_ref[:, NSA_R * 3:2 * NSA_R * 3]))
    for r in range(NSA_R):
        sl = slice(r * TQ, (r + 1) * TQ)
        o = (gates[:, 3 * r:3 * r + 1] * o_cmp[sl] + gates[:, 3 * r + 1:3 * r + 2] * o_sel[sl]
             + gates[:, 3 * r + 2:3 * r + 3] * o_win[sl])
        o_ref[:, r * HD:(r + 1) * HD] = o.astype(BF)


def _nsa_prompt(slopes, slope_rows, qb, gn, ks, vs, kw, vw, ck, cv, cmat):
    nq = SEQ // TQ
    kv_spec = lambda: pl.BlockSpec((SEQ, HD), lambda b, g, qi: (b, g))
    c_spec = lambda: pl.BlockSpec((1, 1, N_CHUNK, HD), lambda b, g, qi: (b, g, 0, 0))
    return pl.pallas_call(
        _nsa_prompt_kernel,
        grid=(BATCH, NSA_G, nq),
        in_specs=[pl.BlockSpec(memory_space=pltpu.SMEM),
                  pl.BlockSpec((8, HD), lambda b, g, qi: (0, 0)),
                  pl.BlockSpec((TQ, NSA_R * HD), lambda b, g, qi: (b * nq + qi, g)),
                  pl.BlockSpec((TQ, HD), lambda b, g, qi: (b * nq + qi, 0)),
                  kv_spec(), kv_spec(), kv_spec(), kv_spec(), c_spec(), c_spec(),
                  pl.BlockSpec((HD, HD), lambda b, g, qi: (0, 0))],
        out_specs=pl.BlockSpec((TQ, NSA_R * HD), lambda b, g, qi: (b * nq + qi, g)),
        out_shape=jax.ShapeDtypeStruct((BATCH * SEQ, NSA_W), BF),
        scratch_shapes=[pltpu.VMEM((SEQ, 2 * HD), BF), pltpu.VMEM((SEQ, 2 * HD), BF),
                        pltpu.VMEM((SEQ, HD), BF), pltpu.VMEM((SEQ, HD), BF)],
        compiler_params=_params(3),
        name="nsa_prompt",
    )(slopes, slope_rows, qb, gn, ks, vs, kw, vw, ck, cv, cmat)


def _mem_attend_kernel(q_ref, k_ref, v_ref, o_ref):
    for h in range(MEM_HEADS):
        sl = slice(h * HD, (h + 1) * HD)
        s = _dot_nt(q_ref[:, sl].astype(BF), k_ref[:, sl].astype(BF)) * SCALE
        m = jnp.max(s, axis=1, keepdims=True)
        p = jnp.exp(s - m)
        p = p / jnp.sum(p, axis=1, keepdims=True)
        o_ref[:, sl] = _dot(p.astype(BF), v_ref[:, sl].astype(BF)).astype(BF)


def _mem_attend_prompt(qm, mem_k, mem_v):
    tq = 512
    nq = SEQ // tq
    return pl.pallas_call(
        _mem_attend_kernel,
        grid=(BATCH, nq),
        in_specs=[pl.BlockSpec((tq, MEM_W), lambda b, qi: (b * nq + qi, 0)),
                  pl.BlockSpec((MEM_LEN, MEM_W), lambda b, qi: (b, 0)),
                  pl.BlockSpec((MEM_LEN, MEM_W), lambda b, qi: (b, 0))],
        out_specs=pl.BlockSpec((tq, MEM_W), lambda b, qi: (b * nq + qi, 0)),
        out_shape=jax.ShapeDtypeStruct((BATCH * SEQ, MEM_W), BF),
        compiler_params=_params(2),
        name="mem_attend_prompt",
    )(qm, mem_k, mem_v)


def _merge_kernel(oa_ref, ob_ref, om_ref, gbr_ref, pa_ref, pb_ref, pm_ref, o_ref):
    gate = lambda i: jax.nn.sigmoid(gbr_ref[:, i * D_MODEL:(i + 1) * D_MODEL])
    merged = (gate(0) * _dot(oa_ref[...], pa_ref[...]) + gate(1) * _dot(ob_ref[...], pb_ref[...])
              + gate(2) * _dot(om_ref[...], pm_ref[...]))
    o_ref[...] = merged.astype(BF)


def _merge(oa, ob, om, gbr, pa, pb, pm, tm, tag):
    rows = oa.shape[0]
    row = lambda w: pl.BlockSpec((tm, w), lambda i: (i, 0))
    full = lambda a: pl.BlockSpec(a.shape, lambda i: (0, 0))
    return pl.pallas_call(
        _merge_kernel,
        grid=(rows // tm,),
        in_specs=[row(MOBA_W), row(NSA_W), row(MEM_W), row(3 * D_MODEL), full(pa), full(pb), full(pm)],
        out_specs=row(D_MODEL),
        out_shape=jax.ShapeDtypeStruct((rows, D_MODEL), BF),
        compiler_params=_params(1),
        name="branch_merge_" + tag,
    )(oa, ob, om, gbr, pa, pb, pm)


def _layer_norm(x, g, b):
    mu = jnp.mean(x, axis=-1, keepdims=True)
    xc = x - mu
    var = jnp.mean(xc * xc, axis=-1, keepdims=True)
    return xc * lax.rsqrt(var + LN_EPS) * g + b


def _out_ln_kernel(x_ref, m_ref, wo_ref, g_ref, b_ref, h_ref, hb_ref):
    h = _layer_norm(DN_ALPHA * x_ref[...] + _dot(m_ref[...], wo_ref[...]), g_ref[...], b_ref[...])
    h_ref[...] = h
    hb_ref[...] = h.astype(BF)


def _out_ln(x, merged, wo, g, b, tm, tag):
    rows = x.shape[0]
    row = lambda: pl.BlockSpec((tm, D_MODEL), lambda i: (i, 0))
    vec = lambda: pl.BlockSpec((1, D_MODEL), lambda i: (0, 0))
    return pl.pallas_call(
        _out_ln_kernel,
        grid=(rows // tm,),
        in_specs=[row(), row(), pl.BlockSpec((D_MODEL, D_MODEL), lambda i: (0, 0)), vec(), vec()],
        out_specs=[row(), row()],
        out_shape=[jax.ShapeDtypeStruct((rows, D_MODEL), F32), jax.ShapeDtypeStruct((rows, D_MODEL), BF)],
        compiler_params=_params(1),
        name="out_proj_ln_" + tag,
    )(x, merged, wo, g, b)


TF = 512
N_TF = FFN_DIM // TF
HALO = 8


def _ffn_kernel(hb_ref, halo_ref, h_ref, wu_ref, wg_ref, cw_ref, cb_ref, wd_ref, g_ref, b_ref, p2_ref, p1_ref,
                y_ref, u_ref, acc_ref, *, seq_rows):
    i = pl.program_id(0)
    f = pl.program_id(1)
    tm = hb_ref.shape[0]
    u = _dot(hb_ref[...], wu_ref[...])
    gate = _dot(hb_ref[...], wg_ref[...])
    if seq_rows:
        tiles_per_seq = seq_rows // tm
        live = jnp.where(i % tiles_per_seq == 0, 0.0, 1.0)
        u_halo = _dot(halo_ref[...], wu_ref[...]) * live
        row = lax.broadcasted_iota(jnp.int32, (HALO, TF), 0)
        prev1 = pltpu.roll(u, shift=1, axis=0)
        prev2 = pltpu.roll(u, shift=2, axis=0)
        head1 = jnp.where(row < 1, pltpu.roll(u_halo, shift=1, axis=0), prev1[:HALO])
        head2 = jnp.where(row < 2, pltpu.roll(u_halo, shift=2, axis=0), prev2[:HALO])
        prev1 = jnp.concatenate([head1, prev1[HALO:]], axis=0)
        prev2 = jnp.concatenate([head2, prev2[HALO:]], axis=0)
        u_ref[0] = u[tm - HALO:]
    else:
        prev2 = p2_ref[...]
        prev1 = p1_ref[...]
        u_ref[...] = u
    uc = cb_ref[...] + cw_ref[0:1, :] * prev2 + cw_ref[1:2, :] * prev1 + cw_ref[2:3, :] * u
    act = (_gelu_tanh(uc) * gate).astype(BF)

    @pl.when(f == 0)
    def _():
        acc_ref[...] = jnp.zeros(acc_ref.shape, F32)

    acc_ref[...] += _dot(act, wd_ref[...])

    @pl.when(f == N_TF - 1)
    def _():
        y_ref[...] = _layer_norm(DN_ALPHA * h_ref[...] + acc_ref[...], g_ref[...], b_ref[...])


def _ffn(h, h_bf, wu_bf, cw, cb, wd_bf, g, b, prev2, prev1, tm, seq_rows, tag):
    rows = h.shape[0]
    n_halo = tm // HALO
    if seq_rows:
        prev2 = prev1 = jnp.zeros((HALO, TF), F32)
        cs_spec = pl.BlockSpec((HALO, TF), lambda i, f: (0, 0))
        halo_spec = pl.BlockSpec((HALO, D_MODEL), lambda i, f: (jnp.maximum(i * n_halo - 1, 0), 0))
        u_shape = jax.ShapeDtypeStruct((rows // tm, HALO, FFN_DIM), F32)
        u_spec = pl.BlockSpec((1, HALO, TF), lambda i, f: (i, 0, f))
    else:
        cs_spec = pl.BlockSpec((tm, TF), lambda i, f: (i, f))
        halo_spec = pl.BlockSpec((HALO, D_MODEL), lambda i, f: (0, 0))
        u_shape = jax.ShapeDtypeStruct((rows, FFN_DIM), F32)
        u_spec = pl.BlockSpec((tm, TF), lambda i, f: (i, f))
    row = lambda: pl.BlockSpec((tm, D_MODEL), lambda i, f: (i, 0))
    vec = lambda: pl.BlockSpec((1, D_MODEL), lambda i, f: (0, 0))
    return pl.pallas_call(
        functools.partial(_ffn_kernel, seq_rows=seq_rows),
        grid=(rows // tm, N_TF),
        in_specs=[row(), halo_spec, row(),
                  pl.BlockSpec((D_MODEL, TF), lambda i, f: (0, f)),
                  pl.BlockSpec((D_MODEL, TF), lambda i, f: (0, N_TF + f)),
                  pl.BlockSpec((3, TF), lambda i, f: (0, f)),
                  pl.BlockSpec((1, TF), lambda i, f: (0, f)),
                  pl.BlockSpec((TF, D_MODEL), lambda i, f: (f, 0)),
                  vec(), vec(), cs_spec, cs_spec],
        out_specs=[row(), u_spec],
        out_shape=[jax.ShapeDtypeStruct((rows, D_MODEL), F32), u_shape],
        scratch_shapes=[pltpu.VMEM((tm, D_MODEL), F32)],
        compiler_params=_params(2),
        name="conv_ffn_" + tag,
    )(h_bf, h_bf, h, wu_bf, wu_bf, cw, cb, wd_bf, g, b, prev2, prev1)


def _rowdot(mat, vec):
    return jnp.sum(mat * vec, axis=1, keepdims=True)


def _moba_sample_kernel(pt_ref, slopes_ref, q_ref, kn_ref, vn_ref, *refs):
    del pt_ref
    k_pages, v_pages, o_ref = refs[:N_PAGES], refs[N_PAGES:2 * N_PAGES], refs[2 * N_PAGES]
    nb = PAST_LEN // MOBA_BLOCK
    ppb = MOBA_BLOCK // PAGE
    off = lax.broadcasted_iota(jnp.int32, (MOBA_BLOCK, 1), 0)
    for h in range(MOBA_HEADS):
        sl = slice(h * HD, (h + 1) * HD)
        q = q_ref[0, :, sl]
        block = lambda pages, j: jnp.concatenate([pages[j * ppb + i][h] for i in range(ppb)], axis=0)
        k_blk = [block(k_pages, j) for j in range(nb)]
        sc = [_rowdot(jnp.sum(kb, axis=0, keepdims=True) / MOBA_BLOCK, q) for kb in k_blk]
        s_own = _rowdot(kn_ref[0, :, sl], q) * SCALE
        s_blk, m = [], s_own
        for j in range(nb):
            beats = [(sc[i] > sc[j]) | ((sc[i] == sc[j]) & (i < j)) for i in range(nb) if i != j]
            rank = sum(b.astype(F32) for b in beats)
            dist = (PAST_LEN - (j * MOBA_BLOCK + off)).astype(F32)
            s = _rowdot(k_blk[j], q) * SCALE - slopes_ref[h] * dist
            s = jnp.where(rank < MOBA_TOPK, s, -jnp.inf)
            s_blk.append(s)
            m = jnp.maximum(m, jnp.max(s, axis=0, keepdims=True))
        p_own = jnp.exp(s_own - m)
        den = p_own
        acc = p_own * vn_ref[0, :, sl]
        for j in range(nb):
            p = jnp.exp(s_blk[j] - m)
            den = den + jnp.sum(p, axis=0, keepdims=True)
            acc = acc + jnp.sum(p * block(v_pages, j), axis=0, keepdims=True)
        o_ref[0, :, sl] = acc / den


def _moba_sample(page_table, slopes, q, k_new, v_new, k_pool, v_pool):
    row = lambda: pl.BlockSpec((1, 1, MOBA_W), lambda n, pt: (n, 0, 0))
    page = lambda p: pl.BlockSpec((None, MOBA_HEADS, PAGE, HD), lambda n, pt: (pt[n, p], 0, 0, 0))
    return pl.pallas_call(
        _moba_sample_kernel,
        grid_spec=pltpu.PrefetchScalarGridSpec(
            num_scalar_prefetch=1, grid=(DEC_BATCH,),
            in_specs=[pl.BlockSpec(memory_space=pltpu.SMEM), row(), row(), row()]
            + [page(p) for p in range(N_PAGES)] * 2,
            out_specs=row()),
        out_shape=jax.ShapeDtypeStruct((DEC_BATCH, 1, MOBA_W), F32),
        compiler_params=_params(1),
        name="moba_sample",
    )(page_table, slopes, q, k_new, v_new, *([k_pool] * N_PAGES), *([v_pool] * N_PAGES))


NS = 8
ROWS_PER_PAGE = PAGE * NSA_G
N_SEL = NSA_TOPN - 1
N_CAND_S = PAST_LEN // SEL_BLOCK


def _nsa_sample_cmp_kernel(pt_ref, slopes_ref, kcn_ref, vcn_ref, q_ref, pe_ref, w1_ref, w2_ref, cmat_ref, tri_ref,
                           *refs):
    del pt_ref
    k_pages, v_pages = refs[:N_PAGES], refs[N_PAGES:2 * N_PAGES]
    ocmp_ref, idx_ref, stage_ref = refs[2 * N_PAGES:2 * N_PAGES + 3]
    s_in = pl.program_id(1)
    unit = N_CHUNK
    chunks_per_page = PAGE // CMP_STRIDE

    rg = CMP_STRIDE * NSA_G
    for which, pages in enumerate((k_pages, v_pages)):
        for pair in range(N_PAGES // 2):
            slabs = [jnp.swapaxes(pages[2 * pair + i][...].reshape(chunks_per_page, rg, HD), 0, 1) for i in range(2)]
            for g in range(NSA_G):
                base = pl.multiple_of(s_in * (NSA_G * unit) + g * unit, unit) + pair * 2 * chunks_per_page
                for r in range(CMP_STRIDE):
                    x = jnp.concatenate([slabs[0][NSA_G * r + g], slabs[1][NSA_G * r + g]], axis=0)
                    stage_ref[which, pl.ds(base, 2 * chunks_per_page), r * HD:(r + 1) * HD] = x.astype(BF)

    @pl.when(s_in == NS - 1)
    def _():
        n_unit = NS * NSA_G
        rows = n_unit * unit
        row = lax.broadcasted_iota(jnp.int32, (rows, HD), 0)
        toks = []
        for which, new_ref in enumerate((kcn_ref, vcn_ref)):
            w1 = w1_ref.at[which]
            acc = _dot(stage_ref[which], w1[...])
            top, bot = acc[:, :HD], acc[:, HD:]
            nxt = pltpu.roll(bot, shift=rows - 1, axis=0)
            new_rows = jnp.concatenate([new_ref[s:s + 1, g * HD:(g + 1) * HD]
                                        for s in range(NS) for g in range(NSA_G)], axis=0)
            b_new = _dot(new_rows.astype(BF), w1[0:HD, HD:])
            b_last = jnp.concatenate([jnp.broadcast_to(b_new[u:u + 1], (unit, HD)) for u in range(n_unit)], axis=0)
            c_row = jnp.zeros((1, HD), F32)
            for r in range(CMP_LEN):
                half, rr = divmod(r, CMP_STRIDE)
                c_row = c_row + _dot(pe_ref[which, :, r * HD:(r + 1) * HD],
                                     w1[rr * HD:(rr + 1) * HD, half * HD:(half + 1) * HD])
            pre = top + jnp.where(row % unit == unit - 1, b_last, nxt) + c_row
            toks.append(_dot(_gelu_tanh(pre).astype(BF), w2_ref[which]))
        ck_all, cv_all = toks

        lane = lax.broadcasted_iota(jnp.int32, (8, HD), 1)
        row8 = lax.broadcasted_iota(jnp.int32, (8, HD), 0)
        dist = PAST_LEN - (lane * CMP_STRIDE + CMP_LEN - 1)
        p_rows = []
        for s in range(NS):
            for g in range(NSA_G):
                u = s * NSA_G + g
                ck_hi, ck_lo = _split_bf16(ck_all[u * unit:(u + 1) * unit])
                cv_bf = cv_all[u * unit:(u + 1) * unit].astype(BF)
                q8 = jnp.concatenate([q_ref[s:s + 1, (g * NSA_R + r) * HD:(g * NSA_R + r + 1) * HD]
                                      for r in range(NSA_R)] + [jnp.zeros((8 - NSA_R, HD), F32)], axis=0)
                slope = jnp.where(row8[:, 0:1] == 0, slopes_ref[g * NSA_R],
                                  jnp.where(row8[:, 0:1] == 1, slopes_ref[g * NSA_R + 1], slopes_ref[g * NSA_R + 2]))
                o8, p8 = _cmp_attend(q8, ck_hi, ck_lo, cv_bf, slope, dist)
                ocmp_ref[s, g] = o8
                p_rows.append(jnp.sum(jnp.where(row8 < NSA_R, p8, 0.0), axis=0, keepdims=True))
        p_grp = jnp.concatenate(p_rows + [jnp.zeros((HD - n_unit, HD), F32)], axis=0)
        member = _select_blocks(p_grp, cmat_ref[...], jnp.full((EXT_MASK, HD), N_CAND_S, jnp.int32), N_CAND_S, N_SEL)
        member_pad = jnp.concatenate([member, jnp.zeros((HD - EXT_MASK, HD), F32)], axis=0)
        rank = _dot(tri_ref[...], member_pad.astype(BF))[:EXT_MASK]
        cand = lax.broadcasted_iota(jnp.int32, (EXT_MASK, HD), 0).astype(F32)
        picks = [jnp.sum(jnp.where((member > 0.5) & (rank == c), cand, 0.0), axis=0, keepdims=True)
                 for c in range(N_SEL)]
        idx_ref[0] = jnp.concatenate(picks + [jnp.zeros((1, HD), F32)], axis=0).astype(jnp.int32)


def _nsa_sample_cmp(page_table, slopes, kc_new, vc_new, q, pe_bf, w1_flat, w2_bf, cmat, tri, ck_pool, cv_pool):
    page = lambda p: pl.BlockSpec((ROWS_PER_PAGE, HD), lambda i, s, pt: (pt[i * NS + s, p], 0))
    full = lambda a: pl.BlockSpec(a.shape, lambda i, s, pt: (0,) * a.ndim)
    rows = lambda w: pl.BlockSpec((NS, w), lambda i, s, pt: (i, 0))
    out = lambda: pl.BlockSpec((NS, NSA_G, 8, HD), lambda i, s, pt: (i, 0, 0, 0))
    return pl.pallas_call(
        _nsa_sample_cmp_kernel,
        grid_spec=pltpu.PrefetchScalarGridSpec(
            num_scalar_prefetch=1, grid=(DEC_BATCH // NS, NS),
            in_specs=[pl.BlockSpec(memory_space=pltpu.SMEM), rows(KV_W), rows(KV_W), rows(NSA_W),
                      full(pe_bf), full(w1_flat), full(w2_bf), full(cmat), full(tri)]
            + [page(p) for p in range(N_PAGES)] * 2,
            out_specs=[out(), pl.BlockSpec((1, N_SEL + 1, HD), lambda i, s, pt: (i, 0, 0))],
            scratch_shapes=[pltpu.VMEM((2, NS * NSA_G * N_CHUNK, CMP_STRIDE * HD), BF)]),
        out_shape=[jax.ShapeDtypeStruct((DEC_BATCH, NSA_G, 8, HD), F32),
                   jax.ShapeDtypeStruct((DEC_BATCH // NS, N_SEL + 1, HD), jnp.int32)],
        compiler_params=_params(2),
        name="nsa_sample_cmp",
    )(page_table, slopes, kc_new, vc_new, q, pe_bf, w1_flat, w2_bf, cmat, tri,
      *([ck_pool] * N_PAGES), *([cv_pool] * N_PAGES))


def _attend_rows(q8, slope, k, v, dist, k_own, v_own):
    s = _dot_nt(q8.astype(BF), k.astype(BF)) * SCALE - slope * dist
    s_own = _rowdot(q8, k_own) * SCALE
    m = jnp.maximum(jnp.max(s, axis=1, keepdims=True), s_own)
    p = jnp.exp(s - m)
    p_own = jnp.exp(s_own - m)
    den = jnp.sum(p, axis=1, keepdims=True) + p_own
    return (_dot(p.astype(BF), v.astype(BF)) + p_own * v_own) / den


BLK_ROWS = SEL_BLOCK * NSA_G


def _nsa_sample_attend_kernel(pt_ref, ix_ref, slopes_ref, q_ref, gn_ref, ksn_ref, vsn_ref, kwn_ref, vwn_ref,
                              ocmp_ref, wk_ref, wv_ref, sk_hbm, sv_hbm, o_ref, wko_ref, wvo_ref, kbuf, vbuf, sem):
    n = pl.program_id(0)
    halves = PAGE // SEL_BLOCK

    def block_copies(sample, slot):
        copies = []
        for g in range(NSA_G):
            for c in range(N_SEL):
                j = ix_ref[(sample * NSA_G + g) * N_SEL + c]
                row0 = pl.multiple_of((pt_ref[sample, j // halves] * halves + j % halves) * BLK_ROWS, BLK_ROWS)
                dst = pl.ds((g * N_SEL + c) * BLK_ROWS, BLK_ROWS)
                for pool, buf in ((sk_hbm, kbuf), (sv_hbm, vbuf)):
                    copies.append(pltpu.make_async_copy(pool.at[pl.ds(row0, BLK_ROWS), :], buf.at[slot, dst, :],
                                                        sem.at[slot]))
        return copies

    @pl.when(n == 0)
    def _():
        for cp in block_copies(0, 0):
            cp.start()

    @pl.when(n + 1 < DEC_BATCH)
    def _():
        for cp in block_copies(n + 1, (n + 1) % 2):
            cp.start()

    slot = n % 2
    for cp in block_copies(n, slot):
        cp.wait()

    wl = wk_ref.shape[0] // NSA_G
    off = lax.broadcasted_iota(jnp.int32, (1, SEL_BLOCK), 1)
    dist_w = (wl - lax.broadcasted_iota(jnp.int32, (1, wl), 1)).astype(F32)
    row8 = lax.broadcasted_iota(jnp.int32, (8, 1), 0)
    gates = jax.nn.sigmoid(gn_ref[0])
    for g in range(NSA_G):
        gsl = slice(g * HD, (g + 1) * HD)
        heads = [g * NSA_R + r for r in range(NSA_R)]
        q8 = jnp.concatenate([q_ref[0, :, hd * HD:(hd + 1) * HD] for hd in heads]
                             + [jnp.zeros((8 - NSA_R, HD), F32)], axis=0)
        slope = jnp.where(row8 == 0, slopes_ref[heads[0]],
                          jnp.where(row8 == 1, slopes_ref[heads[1]], slopes_ref[heads[2]]))
        sel_rows = pl.ds(g * N_SEL * BLK_ROWS + g, N_SEL * SEL_BLOCK, stride=NSA_G)
        pos = jnp.concatenate([ix_ref[(n * NSA_G + g) * N_SEL + c] * SEL_BLOCK + off for c in range(N_SEL)], axis=1)
        o_sel = _attend_rows(q8, slope, kbuf[slot, sel_rows, :], vbuf[slot, sel_rows, :],
                             (PAST_LEN - pos).astype(F32), ksn_ref[0, :, gsl], vsn_ref[0, :, gsl])
        o_win = _attend_rows(q8, slope, wk_ref[pl.ds(g, wl, stride=NSA_G), :], wv_ref[pl.ds(g, wl, stride=NSA_G), :],
                             dist_w, kwn_ref[0, :, gsl], vwn_ref[0, :, gsl])
        for r, hd in enumerate(heads):
            o_ref[0, :, hd * HD:(hd + 1) * HD] = (
                gates[:, 3 * hd:3 * hd + 1] * ocmp_ref[0, g, r:r + 1, :]
                + gates[:, 3 * hd + 1:3 * hd + 2] * o_sel[r:r + 1] + gates[:, 3 * hd + 2:3 * hd + 3] * o_win[r:r + 1])

    for cache, new, out in ((wk_ref, kwn_ref, wko_ref), (wv_ref, vwn_ref, wvo_ref)):
        out[...] = pltpu.roll(cache[...], shift=(wl - 1) * NSA_G, axis=0)
        for g in range(NSA_G):
            out[pl.ds((wl - 1) * NSA_G + g, 1), :] = new[0, :, g * HD:(g + 1) * HD]


def _nsa_sample_attend(page_table, sel_idx, slopes, q, gn, ks_new, vs_new, kw_new, vw_new, o_cmp,
                       win_k, win_v, sk_pool, sv_pool):
    row = lambda w: pl.BlockSpec((1, 1, w), lambda n, pt, ix: (n, 0, 0))
    win_rows = win_k.shape[0] // DEC_BATCH
    win = lambda: pl.BlockSpec((win_rows, HD), lambda n, pt, ix: (n, 0))
    any_spec = pl.BlockSpec(memory_space=pl.ANY)
    buf = pltpu.VMEM((2, NSA_G * N_SEL * BLK_ROWS, HD), F32)
    return pl.pallas_call(
        _nsa_sample_attend_kernel,
        grid_spec=pltpu.PrefetchScalarGridSpec(
            num_scalar_prefetch=2, grid=(DEC_BATCH,),
            in_specs=[pl.BlockSpec(memory_space=pltpu.SMEM), row(NSA_W), row(HD), row(KV_W), row(KV_W), row(KV_W),
                      row(KV_W), pl.BlockSpec((1, NSA_G, 8, HD), lambda n, pt, ix: (n, 0, 0, 0)), win(), win(),
                      any_spec, any_spec],
            out_specs=[row(NSA_W), win(), win()],
            scratch_shapes=[buf, buf, pltpu.SemaphoreType.DMA((2,))]),
        out_shape=[jax.ShapeDtypeStruct((DEC_BATCH, 1, NSA_W), F32), jax.ShapeDtypeStruct(win_k.shape, F32),
                   jax.ShapeDtypeStruct(win_v.shape, F32)],
        compiler_params=_params(1),
        name="nsa_sample_attend",
    )(page_table, sel_idx, slopes, q, gn, ks_new, vs_new, kw_new, vw_new, o_cmp, win_k, win_v, sk_pool, sv_pool)


def _mem_sample_kernel(q_ref, k_ref, v_ref, o_ref):
    for h in range(MEM_HEADS):
        sl = slice(h * HD, (h + 1) * HD)
        q = q_ref[0, :, sl]
        k = k_ref[pl.ds(h, MEM_LEN, stride=MEM_HEADS), :]
        v = v_ref[pl.ds(h, MEM_LEN, stride=MEM_HEADS), :]
        s = _rowdot(k, q) * SCALE
        p = jnp.exp(s - jnp.max(s, axis=0, keepdims=True))
        o_ref[0, :, sl] = jnp.sum(p * v, axis=0, keepdims=True) / jnp.sum(p, axis=0, keepdims=True)


def _mem_sample(q, mem_k, mem_v):
    rows = MEM_LEN * MEM_HEADS
    row = lambda: pl.BlockSpec((1, 1, MEM_W), lambda n: (n, 0, 0))
    kv = lambda: pl.BlockSpec((rows, HD), lambda n: (n, 0))
    return pl.pallas_call(
        _mem_sample_kernel,
        grid=(DEC_BATCH,),
        in_specs=[row(), kv(), kv()],
        out_specs=row(),
        out_shape=jax.ShapeDtypeStruct((DEC_BATCH, 1, MEM_W), F32),
        compiler_params=_params(1),
        name="mem_attend_sample",
    )(q, mem_k, mem_v)


def _project_all(x_bf, w, tm, paged, tag):
    kv_kind = "pages" if paged else "flat"
    qa, ka, va, qb = _proj(x_bf, w["wa"], w["ba"], MOBA_W, ["flat", kv_kind, kv_kind, "flat"], tm,
                           "proj_moba_q_" + tag, True)
    kc, vc, ks, vs, kw, vw = _proj(x_bf, w["wb"], w["bb"], KV_W, ["flat"] * 6, tm, "proj_nsa_kv_" + tag, True)
    (qm,) = _proj(x_bf, w["wm"], w["bm"], MEM_W, ["flat"], tm, "proj_mem_q_" + tag, True)
    (gn,) = _proj(x_bf, w["wg"], w["bg"], HD, ["flat"], tm, "proj_nsa_gate_" + tag, True)
    gbr = _matmul(x_bf, w["wd"], w["bd"], tm, 768, "proj_branch_gate_" + tag)
    return qa, ka, va, qb, kc, vc, ks, vs, kw, vw, qm, gn, gbr


def _strict_lower_ones():
    return jnp.asarray(np.tril(np.ones((HD, HD), np.float32), -1), BF)


def kernel(x_prompt, x_sample, cache_mem_k, cache_mem_v, cache_moba_k, cache_moba_v, cache_nsa_cmp_k, cache_nsa_cmp_v, cache_nsa_sel_k, cache_nsa_sel_v, cache_nsa_win_k, cache_nsa_win_v, cache_ffn_conv, page_table, mem_prompt, w_in, b_in, w_mem_kv, cmp_pe, cmp_w1, cmp_w2, p_moba, p_nsa, p_mem, w_o, ln1_g, ln1_b, w_up, conv_w, conv_b, w_down, ln2_g, ln2_b):
    slopes = _alibi_slopes(MOBA_HEADS)
    w_bf = w_in[0].T.astype(BF)
    b0 = b_in[0][None, :]
    c0, c1, c2, c3 = 4 * MOBA_W, 4 * MOBA_W + 6 * KV_W, 4 * MOBA_W + 6 * KV_W + N_GATE, 4 * MOBA_W + 6 * KV_W + N_GATE + MEM_W
    w = {"wa": w_bf[:c0], "ba": b0[:, :c0], "wb": w_bf[c0:c1], "bb": b0[:, c0:c1],
         "wg": jnp.pad(w_bf[c1:c2], ((0, HD - N_GATE), (0, 0))), "bg": jnp.pad(b0[:, c1:c2], ((0, 0), (0, HD - N_GATE))),
         "wm": w_bf[c2:c3], "bm": b0[:, c2:c3], "wd": w_bf[c3:], "bd": b0[:, c3:]}
    pe_bf, w1_bf, w2_bf = _pack_cmp_weights(cmp_pe[0], cmp_w1[0], cmp_w2[0])
    pa_bf, pb_bf, pm_bf, wo_bf = p_moba[0].astype(BF), p_nsa[0].astype(BF), p_mem[0].astype(BF), w_o[0].astype(BF)
    wu_bf, wd_bf = w_up[0].astype(BF), w_down[0].astype(BF)

    xp = x_prompt.reshape(BATCH * SEQ, D_MODEL)
    xp_bf = xp.astype(BF)
    qa, ka, va, qb, kc, vc, ks, vs, kw, vw, qm, gn, gbr = _project_all(xp_bf, w, 1024, True, "prompt")
    mem_k, mem_v = _proj(mem_prompt.reshape(BATCH * MEM_LEN, D_MODEL).astype(BF), w_mem_kv[0].astype(BF),
                         jnp.zeros((1, 2 * MEM_W), F32), MEM_W, ["flat", "flat"], 512, "proj_mem_kv")
    slope_rows = _slope_rows(MOBA_HEADS)
    o_a = _moba_prompt(slope_rows, qa, ka, va)
    ck, cv = _compress_prompt(kc, vc, pe_bf, w1_bf, w2_bf)
    cmat = _cmp_to_sel_matrix(N_CHUNK - 1, SEQ // SEL_BLOCK - 1)
    o_b = _nsa_prompt(slopes, slope_rows, qb, gn, ks, vs, kw, vw, ck, cv, cmat)
    o_m = _mem_attend_prompt(qm, mem_k, mem_v)
    merged = _merge(o_a, o_b, o_m, gbr, pa_bf, pb_bf, pm_bf, 256, "prompt")
    h, h_bf = _out_ln(xp, merged, wo_bf, ln1_g, ln1_b, 512, "prompt")
    y_p, u_tail = _ffn(h, h_bf, wu_bf, conv_w[0], conv_b, wd_bf, ln2_g, ln2_b, None, None, 512, SEQ, "prompt")

    pages = lambda a: a.reshape(1, BATCH, SEQ // PAGE, PAGE, NSA_G, HD)
    head_pages = lambda a: a.transpose(0, 1, 3, 2, 4)[None]
    last = lambda a: a.reshape(BATCH, SEQ, NSA_G, HD)[None, :, SEQ - WINDOW:]
    tiles_per_seq = SEQ // 512
    conv_p = u_tail.reshape(BATCH, tiles_per_seq, HALO, FFN_DIM)[:, -1, HALO - 2:][None]
    prompt_out = (y_p.reshape(BATCH, SEQ, D_MODEL),
                  mem_k.reshape(1, BATCH, MEM_LEN, MEM_HEADS, HD), mem_v.reshape(1, BATCH, MEM_LEN, MEM_HEADS, HD),
                  head_pages(ka), head_pages(va), pages(kc), pages(vc), pages(ks), pages(vs), last(kw), last(vw), conv_p)

    n = DEC_BATCH
    xs = x_sample.reshape(n, D_MODEL)
    qa, ka, va, qb, kc, vc, ks, vs, kw, vw, qm, gn, gbr = _project_all(xs.astype(BF), w, n, False, "sample")
    r3 = lambda a: a.reshape(n, 1, a.shape[-1])
    moba_pool = lambda c: c[0].transpose(0, 2, 1, 3)
    rows2d = lambda c: c.reshape(-1, HD)
    o_a = _moba_sample(page_table, slopes, r3(qa), r3(ka), r3(va), moba_pool(cache_moba_k), moba_pool(cache_moba_v))
    w1_flat = w1_bf.reshape(2, CMP_STRIDE * HD, 2 * HD)
    cmat_s = _cmp_to_sel_matrix(N_CHUNK, N_CAND_S)
    o_cmp, sel = _nsa_sample_cmp(page_table, slopes, kc, vc, qb, pe_bf, w1_flat, w2_bf, cmat_s, _strict_lower_ones(),
                                 rows2d(cache_nsa_cmp_k), rows2d(cache_nsa_cmp_v))
    sel_idx = sel[:, :N_SEL, :NS * NSA_G].transpose(0, 2, 1).reshape(-1)
    o_b, win_k_new, win_v_new = _nsa_sample_attend(
        page_table, sel_idx, slopes, r3(qb), r3(gn), r3(ks), r3(vs), r3(kw), r3(vw), o_cmp,
        rows2d(cache_nsa_win_k), rows2d(cache_nsa_win_v), rows2d(cache_nsa_sel_k), rows2d(cache_nsa_sel_v))
    o_m = _mem_sample(r3(qm), rows2d(cache_mem_k), rows2d(cache_mem_v))
    flat_bf = lambda a: a.reshape(n, a.shape[-1]).astype(BF)
    merged = _merge(flat_bf(o_a), flat_bf(o_b), flat_bf(o_m), gbr, pa_bf, pb_bf, pm_bf, n, "sample")
    h, h_bf = _out_ln(xs, merged, wo_bf, ln1_g, ln1_b, n, "sample")
    conv_old = cache_ffn_conv[0]
    y_s, u_new = _ffn(h, h_bf, wu_bf, conv_w[0], conv_b, wd_bf, ln2_g, ln2_b, conv_old[:, 0], conv_old[:, 1], n, 0,
                      "sample")

    heads = lambda a, nh: a.reshape(1, n, 1, nh, HD)
    sample_out = (heads(ka, MOBA_HEADS), heads(va, MOBA_HEADS), heads(kc, NSA_G), heads(vc, NSA_G),
                  heads(ks, NSA_G), heads(vs, NSA_G), win_k_new.reshape(cache_nsa_win_k.shape),
                  win_v_new.reshape(cache_nsa_win_v.shape),
                  jnp.stack([conv_old[:, 1], u_new], axis=1)[None])
    return (prompt_out[0], y_s.reshape(n, 1, D_MODEL)) + prompt_out[1:] + sample_out
```

```python
import functools

import numpy as np
import jax
import jax.numpy as jnp
from jax import lax
from jax.experimental import pallas as pl
from jax.experimental.pallas import tpu as pltpu

D_MODEL = 2048
BATCH = 4
SEQ = 2048
DEC_BATCH = 128
PAST_LEN = 2048
PAGE = 128
N_PAGES = PAST_LEN // PAGE
HD = 128
MOBA_HEADS = 6
MOBA_BLOCK = 256
MOBA_TOPK = 3
NSA_HEADS = 6
NSA_G = 2
NSA_R = 3
CMP_LEN = 32
CMP_STRIDE = 16
SEL_BLOCK = 64
NSA_TOPN = 16
WINDOW = 512
MEM_HEADS = 4
MEM_LEN = 256
FFN_DIM = 5632
DN_ALPHA = 2.0 ** 0.25
LN_EPS = 1e-5
SCALE = HD ** -0.5
MOBA_W = MOBA_HEADS * HD
NSA_W = NSA_HEADS * HD
KV_W = NSA_G * HD
MEM_W = MEM_HEADS * HD
N_GATE = NSA_HEADS * 3

F32 = jnp.float32
BF = jnp.bfloat16
NEG_BIG = -(2.0 ** 100)
VMEM_LIMIT = 48 * 1024 * 1024
TQ = 256


def _dot(a, b):
    return jnp.dot(a, b, preferred_element_type=F32)


def _dot_nt(a, b):
    return lax.dot_general(a, b, (((1,), (1,)), ((), ())), preferred_element_type=F32)


def _split_bf16(a):
    hi = a.astype(BF)
    lo = (a - hi.astype(F32)).astype(BF)
    return hi, lo


def _alibi_slopes(n):
    return jnp.asarray(2.0 ** (-8.0 * np.arange(1, n + 1) / n), F32)


def _gelu_tanh(x):
    return 0.5 * x * (1.0 + jnp.tanh(np.sqrt(2.0 / np.pi).astype(np.float32) * (x + 0.044715 * (x * x * x))))


def _params(n_axes):
    return pltpu.CompilerParams(dimension_semantics=("arbitrary",) * n_axes, vmem_limit_bytes=VMEM_LIMIT)


def _proj_kernel(x_ref, w_ref, b_ref, *out_refs, kinds, w_rows):
    j = pl.program_id(1)
    acc = (_dot_nt if w_rows else _dot)(x_ref[...], w_ref[...]) + b_ref[...]
    for idx, (kind, o_ref) in enumerate(zip(kinds, out_refs)):
        @pl.when(j == idx)
        def _(kind=kind, o_ref=o_ref):
            if kind == "flat":
                o_ref[...] = acc
            else:
                for p in range(o_ref.shape[1]):
                    for h in range(o_ref.shape[2]):
                        o_ref[0, p, h] = acc[p * PAGE:(p + 1) * PAGE, h * HD:(h + 1) * HD]


def _proj(x_bf, w_bf, b, tn, kinds, tm, name, w_rows=False):
    rows, k = x_bf.shape
    n_out = len(kinds)
    assert w_bf.shape == ((n_out * tn, k) if w_rows else (k, n_out * tn)) and rows % tm == 0
    w_spec = pl.BlockSpec((tn, k), lambda i, j: (j, 0)) if w_rows else pl.BlockSpec((k, tn), lambda i, j: (0, j))
    tiles_per_batch = SEQ // tm if tm <= SEQ else 1
    out_shape, out_specs = [], []
    for idx, kind in enumerate(kinds):
        if kind == "flat":
            out_shape.append(jax.ShapeDtypeStruct((rows, tn), F32))
            out_specs.append(pl.BlockSpec((tm, tn), lambda i, j: (i, 0)))
        else:
            heads = tn // HD
            out_shape.append(jax.ShapeDtypeStruct((rows // SEQ, SEQ // PAGE, heads, PAGE, HD), F32))
            out_specs.append(pl.BlockSpec((1, tm // PAGE, heads, PAGE, HD),
                                          lambda i, j: (i // tiles_per_batch, i % tiles_per_batch, 0, 0, 0)))
    return pl.pallas_call(
        functools.partial(_proj_kernel, kinds=tuple(kinds), w_rows=w_rows),
        grid=(rows // tm, n_out),
        in_specs=[pl.BlockSpec((tm, k), lambda i, j: (i, 0)), w_spec,
                  pl.BlockSpec((1, tn), lambda i, j: (0, j))],
        out_specs=out_specs,
        out_shape=out_shape,
        compiler_params=_params(2),
        name=name,
    )(x_bf, w_bf, b)


def _matmul_kernel(x_ref, w_ref, b_ref, o_ref):
    o_ref[...] = _dot_nt(x_ref[...], w_ref[...]) + b_ref[...]


def _matmul(x_bf, w_bf, b, tm, tn, name):
    rows, k = x_bf.shape
    n = w_bf.shape[0]
    return pl.pallas_call(
        _matmul_kernel,
        grid=(rows // tm, n // tn),
        in_specs=[pl.BlockSpec((tm, k), lambda i, j: (i, 0)),
                  pl.BlockSpec((tn, k), lambda i, j: (j, 0)),
                  pl.BlockSpec((1, tn), lambda i, j: (0, j))],
        out_specs=pl.BlockSpec((tm, tn), lambda i, j: (i, j)),
        out_shape=jax.ShapeDtypeStruct((rows, n), F32),
        compiler_params=_params(2),
        name=name,
    )(x_bf, w_bf, b)


def _topk_member(v, cand, n_cand, k):
    member = jnp.zeros(v.shape, F32)
    for j in range(n_cand):
        vj = v[j:j + 1, :]
        beats = (v > vj) | ((v == vj) & (cand < j))
        cnt = jnp.sum(beats.astype(F32), axis=0, keepdims=True)
        member = jnp.where((cand == j) & (cnt < k), 1.0, member)
    return member


def _to_rows(member_t):
    pad = jnp.zeros((HD - member_t.shape[0], member_t.shape[1]), F32)
    return jnp.transpose(jnp.concatenate([member_t, pad], axis=0))


EXT_MASK = 32
N_PIECE = 3
POS_SPLIT = 64
EXP2_SCALE = float(SCALE * np.log2(np.e))


def _slope_rows(n_heads):
    slope = np.asarray(2.0 ** (-8.0 * np.arange(1, n_heads + 1) / n_heads), np.float32) / np.float32(SCALE)
    rows = np.zeros((8, HD), np.float32)
    rest = slope.astype(np.float32)
    for i in range(N_PIECE):
        piece = rest.astype(BF).astype(np.float32)
        rows[:n_heads, EXT_MASK + i] = piece * POS_SPLIT
        rows[:n_heads, EXT_MASK + N_PIECE + i] = piece
        rest = rest - piece
    return jnp.asarray(rows)


def _key_ext(n_keys, block):
    key = lax.broadcasted_iota(jnp.int32, (n_keys, HD), 0)
    lane = lax.broadcasted_iota(jnp.int32, (n_keys, HD), 1)
    hot = jnp.where(key // block == lane, 1.0, 0.0)
    pos = jnp.where(lane < EXT_MASK + N_PIECE, (key // POS_SPLIT).astype(F32), (key % POS_SPLIT).astype(F32))
    return jnp.where(lane < EXT_MASK, hot, jnp.where(lane < EXT_MASK + 2 * N_PIECE, pos, 0.0)).astype(BF)


def _query_ext(allowed, slope_row, lane):
    mask = 0.0 if allowed is None else jnp.where(allowed, 0.0, NEG_BIG)
    return jnp.where(lane < EXT_MASK, mask, slope_row).astype(BF)


def _flash_init(qx, kx, v_bf, keep):
    s = jnp.where(keep, _dot_nt(qx, kx), NEG_BIG)
    m = jnp.max(s, axis=1, keepdims=True)
    p = jnp.exp2((s - m) * EXP2_SCALE)
    return m, jnp.sum(p, axis=1, keepdims=True), _dot(p.astype(BF), v_bf)


def _flash_step(state, qx, kx, v_bf, keep=None):
    m_old, l_old, acc_old = state
    s = _dot_nt(qx, kx)
    if keep is not None:
        s = jnp.where(keep, s, NEG_BIG)
    m_new = jnp.maximum(m_old, jnp.max(s, axis=1, keepdims=True))
    alpha = jnp.exp2((m_old - m_new) * EXP2_SCALE)
    p = jnp.exp2((s - m_new) * EXP2_SCALE)
    return (m_new, alpha * l_old + jnp.sum(p, axis=1, keepdims=True),
            alpha * acc_old + _dot(p.astype(BF), v_bf))


MOBA_HB = 3


def _moba_prompt_kernel(srow_ref, q_ref, k_ref, v_ref, o_ref, kx_ref, vb_ref, kmh_ref, kml_ref):
    hg = pl.program_id(1)
    qi = pl.program_id(2)
    nb = SEQ // MOBA_BLOCK

    @pl.when(qi == 0)
    def _():
        ext = _key_ext(SEQ, MOBA_BLOCK)
        for hh in range(MOBA_HB):
            k = k_ref[:, hh].reshape(SEQ, HD)
            kx_ref[hh, :, :HD] = k.astype(BF)
            kx_ref[hh, :, HD:] = ext
            vb_ref[hh] = v_ref[:, hh].reshape(SEQ, HD).astype(BF)
            km = jnp.sum(k.reshape(nb, MOBA_BLOCK, HD), axis=1) / MOBA_BLOCK
            hi, lo = _split_bf16(jnp.concatenate([km, jnp.zeros((HD - nb, HD), F32)], axis=0))
            kmh_ref[hh] = hi
            kml_ref[hh] = lo

    lane = lax.broadcasted_iota(jnp.int32, (TQ, HD), 1)
    cand = lax.broadcasted_iota(jnp.int32, (nb, TQ), 0)
    rr = lax.broadcasted_iota(jnp.int32, (TQ, MOBA_BLOCK), 0)
    cc = lax.broadcasted_iota(jnp.int32, (TQ, MOBA_BLOCK), 1)
    own = pl.multiple_of(qi * MOBA_BLOCK, MOBA_BLOCK)
    qx, states = [], []
    for hh in range(MOBA_HB):
        q = q_ref[:, hh * HD:(hh + 1) * HD]
        qh, ql = _split_bf16(q)
        sc = (_dot_nt(kmh_ref[hh], qh) + _dot_nt(kml_ref[hh], qh) + _dot_nt(kmh_ref[hh], ql))[:nb]
        sc = jnp.where(cand < qi, sc, -jnp.inf)
        member = _topk_member(sc, cand, nb - 1, MOBA_TOPK)
        allowed = (_to_rows(member) > 0.5) | (lane == qi)
        ext = _query_ext(allowed, srow_ref[pl.ds(hg * MOBA_HB + hh, 1), :], lane)
        qx.append(jnp.concatenate([qh, ext], axis=1))
        states.append(_flash_init(qx[hh], kx_ref[hh, pl.ds(own, MOBA_BLOCK), :],
                                  vb_ref[hh, pl.ds(own, MOBA_BLOCK), :], cc <= rr))

    def body(c, states):
        rows = pl.ds(pl.multiple_of(c * MOBA_BLOCK, MOBA_BLOCK), MOBA_BLOCK)
        return tuple(_flash_step(states[hh], qx[hh], kx_ref[hh, rows, :], vb_ref[hh, rows, :])
                     for hh in range(MOBA_HB))

    states = lax.fori_loop(0, qi, body, tuple(states))
    for hh in range(MOBA_HB):
        m, l, acc = states[hh]
        o_ref[:, hh * HD:(hh + 1) * HD] = (acc / l).astype(BF)


def _moba_prompt(slope_rows, q, k_pages, v_pages):
    nq = SEQ // TQ
    kv_spec = pl.BlockSpec((None, SEQ // PAGE, MOBA_HB, PAGE, HD), lambda b, h, qi: (b, 0, h, 0, 0))
    q_spec = pl.BlockSpec((TQ, MOBA_HB * HD), lambda b, h, qi: (b * nq + qi, h))
    return pl.pallas_call(
        _moba_prompt_kernel,
        grid=(BATCH, MOBA_HEADS // MOBA_HB, nq),
        in_specs=[pl.BlockSpec((8, HD), lambda b, h, qi: (0, 0)), q_spec, kv_spec, kv_spec],
        out_specs=q_spec,
        out_shape=jax.ShapeDtypeStruct((BATCH * SEQ, MOBA_W), BF),
        scratch_shapes=[pltpu.VMEM((MOBA_HB, SEQ, 2 * HD), BF), pltpu.VMEM((MOBA_HB, SEQ, HD), BF),
                        pltpu.VMEM((MOBA_HB, HD, HD), BF), pltpu.VMEM((MOBA_HB, HD, HD), BF)],
        compiler_params=_params(3),
        name="moba_prompt",
    )(slope_rows, q, k_pages, v_pages)


N_CHUNK = SEQ // CMP_STRIDE


def _compress_tokens(get_x, w1_ref, w2_ref, c_row, b_last):
    acc = jnp.zeros((N_CHUNK, 2 * HD), F32)
    for r in range(CMP_STRIDE):
        acc = acc + _dot(get_x(r).astype(BF), w1_ref[r])
    top, bot = acc[:, :HD], acc[:, HD:]
    nxt = pltpu.roll(bot, shift=N_CHUNK - 1, axis=0)
    row = lax.broadcasted_iota(jnp.int32, (N_CHUNK, HD), 0)
    pre = top + jnp.where(row == N_CHUNK - 1, b_last, nxt) + c_row
    return _dot(_gelu_tanh(pre).astype(BF), w2_ref[...])


def _compress_prompt_kernel(kc_ref, vc_ref, pe_ref, w1_ref, w2_ref, ck_ref, cv_ref):
    for which, (src, dst) in enumerate(((kc_ref, ck_ref), (vc_ref, cv_ref))):
        w1 = w1_ref.at[which]
        c_row = _pe_term(pe_ref, w1, which)
        get_x = lambda r, src=src: src[pl.ds(r, N_CHUNK, stride=CMP_STRIDE), :]
        dst[0, 0] = _compress_tokens(get_x, w1, w2_ref.at[which], c_row, jnp.zeros((1, HD), F32))


def _pe_term(pe_ref, w1, which):
    c_row = jnp.zeros((1, HD), F32)
    for r in range(CMP_LEN):
        half, rr = divmod(r, CMP_STRIDE)
        c_row = c_row + _dot(pe_ref[which, :, r * HD:(r + 1) * HD], w1[rr, :, half * HD:(half + 1) * HD])
    return c_row


def _pack_cmp_weights(cmp_pe, cmp_w1, cmp_w2):
    w1 = cmp_w1.reshape(2, 2, CMP_STRIDE, HD, HD).transpose(0, 2, 3, 1, 4).reshape(2, CMP_STRIDE, HD, 2 * HD)
    return cmp_pe.reshape(2, 1, CMP_LEN * HD).astype(BF), w1.astype(BF), cmp_w2.astype(BF)


def _compress_prompt(kc, vc, pe_bf, w1_bf, w2_bf):
    full = lambda a: pl.BlockSpec(a.shape, lambda b, g: (0,) * a.ndim)
    return pl.pallas_call(
        _compress_prompt_kernel,
        grid=(BATCH, NSA_G),
        in_specs=[pl.BlockSpec((SEQ, HD), lambda b, g: (b, g)), pl.BlockSpec((SEQ, HD), lambda b, g: (b, g)),
                  full(pe_bf), full(w1_bf), full(w2_bf)],
        out_specs=[pl.BlockSpec((1, 1, N_CHUNK, HD), lambda b, g: (b, g, 0, 0))] * 2,
        out_shape=[jax.ShapeDtypeStruct((BATCH, NSA_G, N_CHUNK, HD), F32)] * 2,
        compiler_params=_params(2),
        name="nsa_compress_prompt",
    )(kc, vc, pe_bf, w1_bf, w2_bf)


def _cmp_to_sel_matrix(n_cmp, n_blocks):
    start = np.arange(n_cmp)[:, None] * CMP_STRIDE
    j0 = np.arange(n_blocks)[None, :] * SEL_BLOCK
    ov = np.minimum(start + CMP_LEN, j0 + SEL_BLOCK) - np.maximum(start, j0)
    m = np.zeros((HD, HD), np.float32)
    m[:n_blocks, :n_cmp] = (np.clip(ov, 0, None) / CMP_LEN).T
    return jnp.asarray(m, BF)


def _cmp_attend(q, ck_hi, ck_lo, cv_bf, slope, dist):
    qh, ql = _split_bf16(q)
    s = (_dot_nt(qh, ck_hi) + _dot_nt(qh, ck_lo) + _dot_nt(ql, ck_hi)) * SCALE - slope * dist.astype(F32)
    s = jnp.where(dist >= 0, s, -jnp.inf)
    m = jnp.max(s, axis=1, keepdims=True)
    m = jnp.where(m > -jnp.inf, m, 0.0)
    p = jnp.exp(s - m)
    p = p / jnp.maximum(jnp.sum(p, axis=1, keepdims=True), 1e-30)
    return _dot(p.astype(BF), cv_bf), p


def _select_blocks(p_grp, cmat_bf, cur, n_cand, k):
    ph, plo = _split_bf16(p_grp)
    imp = (_dot_nt(cmat_bf, ph) + _dot_nt(cmat_bf, plo))[:EXT_MASK]
    cand = lax.broadcasted_iota(jnp.int32, imp.shape, 0)
    valid = (cand < cur) & (cand < n_cand)
    forced = valid & ((cand == 0) | (cand == cur - 1))
    v = jnp.where(forced, jnp.inf, jnp.where(valid, imp, -jnp.inf))
    return jnp.where(valid, _topk_member(v, cand, n_cand, k), 0.0)


def _nsa_prompt_kernel(slopes_ref, srow_ref, q_ref, gn_ref, ks_ref, vs_ref, kw_ref, vw_ref, ck_ref, cv_ref, cmat_ref,
                       o_ref, kx_ref, kwx_ref, vsb_ref, vwb_ref):
    g = pl.program_id(1)
    qi = pl.program_id(2)
    chunk = TQ
    rows = NSA_R * TQ

    @pl.when(qi == 0)
    def _():
        ext = _key_ext(SEQ, SEL_BLOCK)
        kx_ref[:, :HD] = ks_ref[...].astype(BF)
        kx_ref[:, HD:] = ext
        kwx_ref[:, :HD] = kw_ref[...].astype(BF)
        kwx_ref[:, HD:] = ext
        vsb_ref[...] = vs_ref[...].astype(BF)
        vwb_ref[...] = vw_ref[...].astype(BF)

    lane = lax.broadcasted_iota(jnp.int32, (TQ, HD), 1)
    t = qi * TQ + lax.broadcasted_iota(jnp.int32, (TQ, HD), 0)
    stack = lambda parts: jnp.concatenate(parts, axis=0)
    q_all = stack([q_ref[:, r * HD:(r + 1) * HD] for r in range(NSA_R)])

    ck_hi, ck_lo = _split_bf16(ck_ref[0, 0])
    slope_col = stack([jnp.full((TQ, 1), slopes_ref[g * NSA_R + r], F32) for r in range(NSA_R)])
    dist_c = stack([t - (lane * CMP_STRIDE + CMP_LEN - 1)] * NSA_R)
    o_cmp, p = _cmp_attend(q_all, ck_hi, ck_lo, cv_ref[0, 0].astype(BF), slope_col, dist_c)
    p_grp = p[0:TQ] + p[TQ:2 * TQ] + p[2 * TQ:3 * TQ]
    cur_t = (qi * TQ + lax.broadcasted_iota(jnp.int32, (EXT_MASK, TQ), 1)) // SEL_BLOCK
    member = _select_blocks(p_grp, cmat_ref[...], cur_t, SEQ // SEL_BLOCK - 1, NSA_TOPN - 1)
    allowed = (_to_rows(member) > 0.5) | (lane == t // SEL_BLOCK)

    srow = [srow_ref[pl.ds(g * NSA_R + r, 1), :] for r in range(NSA_R)]
    q_bf = q_all.astype(BF)
    qx = jnp.concatenate([q_bf, stack([_query_ext(allowed, srow[r], lane) for r in range(NSA_R)])], axis=1)
    qw = jnp.concatenate([q_bf, stack([_query_ext(None, srow[r], lane) for r in range(NSA_R)])], axis=1)
    rr = lax.broadcasted_iota(jnp.int32, (rows, chunk), 0) % TQ
    cc = lax.broadcasted_iota(jnp.int32, (rows, chunk), 1)
    own = pl.ds(pl.multiple_of(qi * chunk, chunk), chunk)
    past = lambda c: pl.ds(pl.multiple_of(c * chunk, chunk), chunk)

    st = _flash_init(qx, kx_ref[own, :], vsb_ref[own, :], cc <= rr)
    st = lax.fori_loop(0, qi, lambda c, st: _flash_step(st, qx, kx_ref[past(c), :], vsb_ref[past(c), :]), st)
    o_sel = st[2] / st[1]

    st = _flash_init(qw, kwx_ref[own, :], vwb_ref[own, :], cc <= rr)

    def win_step(c, st):
        slack = jnp.where(c == qi - WINDOW // chunk, 0, chunk)
        return _flash_step(st, qw, kwx_ref[past(c), :], vwb_ref[past(c), :], rr - cc <= slack)

    st = lax.fori_loop(jnp.maximum(qi - WINDOW // chunk, 0), qi, win_step, st)
    o_win = st[2] / st[1]

    gates = jax.nn.sigmoid(jnp.where(g == 0, gn_ref[:, 0:NSA_R * 3], gn_ref[:, NSA_R * 3:2 * NSA_R * 3]))
    for r in range(NSA_R):
        sl = slice(r * TQ, (r + 1) * TQ)
        o = (gates[:, 3 * r:3 * r + 1] * o_cmp[sl] + gates[:, 3 * r + 1:3 * r + 2] * o_sel[sl]
             + gates[:, 3 * r + 2:3 * r + 3] * o_win[sl])
        o_ref[:, r * HD:(r + 1) * HD] = o.astype(BF)


def _nsa_prompt(slopes, slope_rows, qb, gn, ks, vs, kw, vw, ck, cv, cmat):
    nq = SEQ // TQ
    kv_spec = lambda: pl.BlockSpec((SEQ, HD), lambda b, g, qi: (b, g))
    c_spec = lambda: pl.BlockSpec((1, 1, N_CHUNK, HD), lambda b, g, qi: (b, g, 0, 0))
    return pl.pallas_call(
        _nsa_prompt_kernel,
        grid=(BATCH, NSA_G, nq),
        in_specs=[pl.BlockSpec(memory_space=pltpu.SMEM),
                  pl.BlockSpec((8, HD), lambda b, g, qi: (0, 0)),
                  pl.BlockSpec((TQ, NSA_R * HD), lambda b, g, qi: (b * nq + qi, g)),
                  pl.BlockSpec((TQ, HD), lambda b, g, qi: (b * nq + qi, 0)),
                  kv_spec(), kv_spec(), kv_spec(), kv_spec(), c_spec(), c_spec(),
                  pl.BlockSpec((HD, HD), lambda b, g, qi: (0, 0))],
        out_specs=pl.BlockSpec((TQ, NSA_R * HD), lambda b, g, qi: (b * nq + qi, g)),
        out_shape=jax.ShapeDtypeStruct((BATCH * SEQ, NSA_W), BF),
        scratch_shapes=[pltpu.VMEM((SEQ, 2 * HD), BF), pltpu.VMEM((SEQ, 2 * HD), BF),
                        pltpu.VMEM((SEQ, HD), BF), pltpu.VMEM((SEQ, HD), BF)],
        compiler_params=_params(3),
        name="nsa_prompt",
    )(slopes, slope_rows, qb, gn, ks, vs, kw, vw, ck, cv, cmat)


def _mem_attend_kernel(q_ref, k_ref, v_ref, o_ref):
    for h in range(MEM_HEADS):
        sl = slice(h * HD, (h + 1) * HD)
        s = _dot_nt(q_ref[:, sl].astype(BF), k_ref[:, sl].astype(BF)) * SCALE
        m = jnp.max(s, axis=1, keepdims=True)
        p = jnp.exp(s - m)
        p = p / jnp.sum(p, axis=1, keepdims=True)
        o_ref[:, sl] = _dot(p.astype(BF), v_ref[:, sl].astype(BF)).astype(BF)


def _mem_attend_prompt(qm, mem_k, mem_v):
    tq = 512
    nq = SEQ // tq
    return pl.pallas_call(
        _mem_attend_kernel,
        grid=(BATCH, nq),
        in_specs=[pl.BlockSpec((tq, MEM_W), lambda b, qi: (b * nq + qi, 0)),
                  pl.BlockSpec((MEM_LEN, MEM_W), lambda b, qi: (b, 0)),
                  pl.BlockSpec((MEM_LEN, MEM_W), lambda b, qi: (b, 0))],
        out_specs=pl.BlockSpec((tq, MEM_W), lambda b, qi: (b * nq + qi, 0)),
        out_shape=jax.ShapeDtypeStruct((BATCH * SEQ, MEM_W), BF),
        compiler_params=_params(2),
        name="mem_attend_prompt",
    )(qm, mem_k, mem_v)


def _merge_kernel(oa_ref, ob_ref, om_ref, ga_ref, gb_ref, gm_ref, pa_ref, pb_ref, pm_ref, o_ref):
    merged = (jax.nn.sigmoid(ga_ref[...]) * _dot(oa_ref[...], pa_ref[...])
              + jax.nn.sigmoid(gb_ref[...]) * _dot(ob_ref[...], pb_ref[...])
              + jax.nn.sigmoid(gm_ref[...]) * _dot(om_ref[...], pm_ref[...]))
    o_ref[...] = merged.astype(BF)


def _merge(oa, ob, om, gbr, pa, pb, pm, tm, tag):
    rows = oa.shape[0]
    tn = 512
    n_col = D_MODEL // tn
    row = lambda w: pl.BlockSpec((tm, w), lambda i, j: (i, 0))
    gate = lambda b: pl.BlockSpec((tm, tn), lambda i, j: (i, b * n_col + j))
    proj = lambda w: pl.BlockSpec((w, tn), lambda i, j: (0, j))
    return pl.pallas_call(
        _merge_kernel,
        grid=(rows // tm, n_col),
        in_specs=[row(MOBA_W), row(NSA_W), row(MEM_W), gate(0), gate(1), gate(2),
                  proj(MOBA_W), proj(NSA_W), proj(MEM_W)],
        out_specs=pl.BlockSpec((tm, tn), lambda i, j: (i, j)),
        out_shape=jax.ShapeDtypeStruct((rows, D_MODEL), BF),
        compiler_params=_params(2),
        name="branch_merge_" + tag,
    )(oa, ob, om, gbr, gbr, gbr, pa, pb, pm)


def _layer_norm(x, g, b):
    mu = jnp.mean(x, axis=-1, keepdims=True)
    xc = x - mu
    var = jnp.mean(xc * xc, axis=-1, keepdims=True)
    return xc * lax.rsqrt(var + LN_EPS) * g + b


def _out_ln_kernel(x_ref, m_ref, wo_ref, g_ref, b_ref, h_ref, hb_ref):
    h = _layer_norm(DN_ALPHA * x_ref[...] + _dot(m_ref[...], wo_ref[...]), g_ref[...], b_ref[...])
    h_ref[...] = h
    hb_ref[...] = h.astype(BF)


def _out_ln(x, merged, wo, g, b, tm, tag):
    rows = x.shape[0]
    row = lambda: pl.BlockSpec((tm, D_MODEL), lambda i: (i, 0))
    vec = lambda: pl.BlockSpec((1, D_MODEL), lambda i: (0, 0))
    return pl.pallas_call(
        _out_ln_kernel,
        grid=(rows // tm,),
        in_specs=[row(), row(), pl.BlockSpec((D_MODEL, D_MODEL), lambda i: (0, 0)), vec(), vec()],
        out_specs=[row(), row()],
        out_shape=[jax.ShapeDtypeStruct((rows, D_MODEL), F32), jax.ShapeDtypeStruct((rows, D_MODEL), BF)],
        compiler_params=_params(1),
        name="out_proj_ln_" + tag,
    )(x, merged, wo, g, b)


TF = 512
N_TF = FFN_DIM // TF
HALO = 8
FFN_TM = 1024
FFN_VMEM_LIMIT = 58 * 1024 * 1024


def _ffn_kernel(hb_ref, halo_ref, h_ref, wu_ref, wg_ref, cw_ref, cb_ref, wd_ref, g_ref, b_ref, p2_ref, p1_ref,
                y_ref, u_ref, *, seq_rows):
    i = pl.program_id(0)
    f = pl.program_id(1)
    tm = hb_ref.shape[0]
    u = _dot(hb_ref[...], wu_ref[...])
    gate = _dot(hb_ref[...], wg_ref[...])
    if seq_rows:
        tiles_per_seq = seq_rows // tm
        live = jnp.where(i % tiles_per_seq == 0, 0.0, 1.0)
        u_halo = _dot(halo_ref[...], wu_ref[...]) * live
        row = lax.broadcasted_iota(jnp.int32, (HALO, TF), 0)
        prev1 = pltpu.roll(u, shift=1, axis=0)
        prev2 = pltpu.roll(u, shift=2, axis=0)
        head1 = jnp.where(row < 1, pltpu.roll(u_halo, shift=1, axis=0), prev1[:HALO])
        head2 = jnp.where(row < 2, pltpu.roll(u_halo, shift=2, axis=0), prev2[:HALO])
        prev1 = jnp.concatenate([head1, prev1[HALO:]], axis=0)
        prev2 = jnp.concatenate([head2, prev2[HALO:]], axis=0)
        u_ref[0] = u[tm - HALO:]
    else:
        prev2 = p2_ref[...]
        prev1 = p1_ref[...]
        u_ref[...] = u
    uc = cb_ref[...] + cw_ref[0:1, :] * prev2 + cw_ref[1:2, :] * prev1 + cw_ref[2:3, :] * u
    act = (_gelu_tanh(uc) * gate).astype(BF)

    @pl.when(f == 0)
    def _():
        y_ref[...] = jnp.zeros(y_ref.shape, F32)

    y_ref[...] += _dot(act, wd_ref[...])

    @pl.when(f == N_TF - 1)
    def _():
        y_ref[...] = _layer_norm(DN_ALPHA * h_ref[...] + y_ref[...], g_ref[...], b_ref[...])


def _ffn(h, h_bf, wu_bf, cw, cb, wd_bf, g, b, prev2, prev1, tm, seq_rows, tag):
    rows = h.shape[0]
    n_halo = tm // HALO
    if seq_rows:
        prev2 = prev1 = jnp.zeros((HALO, TF), F32)
        cs_spec = pl.BlockSpec((HALO, TF), lambda i, f: (0, 0))
        halo_spec = pl.BlockSpec((HALO, D_MODEL), lambda i, f: (jnp.maximum(i * n_halo - 1, 0), 0))
        u_shape = jax.ShapeDtypeStruct((rows // tm, HALO, FFN_DIM), F32)
        u_spec = pl.BlockSpec((1, HALO, TF), lambda i, f: (i, 0, f))
    else:
        cs_spec = pl.BlockSpec((tm, TF), lambda i, f: (i, f))
        halo_spec = pl.BlockSpec((HALO, D_MODEL), lambda i, f: (0, 0))
        u_shape = jax.ShapeDtypeStruct((rows, FFN_DIM), F32)
        u_spec = pl.BlockSpec((tm, TF), lambda i, f: (i, f))
    row = lambda: pl.BlockSpec((tm, D_MODEL), lambda i, f: (i, 0))
    row_once = lambda: pl.BlockSpec((tm, D_MODEL), lambda i, f: (i, 0), pipeline_mode=pl.Buffered(1))
    vec = lambda: pl.BlockSpec((1, D_MODEL), lambda i, f: (0, 0))
    return pl.pallas_call(
        functools.partial(_ffn_kernel, seq_rows=seq_rows),
        grid=(rows // tm, N_TF),
        in_specs=[row_once(), halo_spec, row_once(),
                  pl.BlockSpec((D_MODEL, TF), lambda i, f: (0, f)),
                  pl.BlockSpec((D_MODEL, TF), lambda i, f: (0, N_TF + f)),
                  pl.BlockSpec((3, TF), lambda i, f: (0, f)),
                  pl.BlockSpec((1, TF), lambda i, f: (0, f)),
                  pl.BlockSpec((TF, D_MODEL), lambda i, f: (f, 0)),
                  vec(), vec(), cs_spec, cs_spec],
        out_specs=[row(), u_spec],
        out_shape=[jax.ShapeDtypeStruct((rows, D_MODEL), F32), u_shape],
        compiler_params=pltpu.CompilerParams(dimension_semantics=("arbitrary", "arbitrary"),
                                             vmem_limit_bytes=FFN_VMEM_LIMIT),
        name="conv_ffn_" + tag,
    )(h_bf, h_bf, h, wu_bf, wu_bf, cw, cb, wd_bf, g, b, prev2, prev1)


def _rowdot(mat, vec):
    return jnp.sum(mat * vec, axis=1, keepdims=True)


def _moba_sample_kernel(pt_ref, slopes_ref, q_ref, kn_ref, vn_ref, *refs):
    del pt_ref
    k_pages, v_pages, o_ref = refs[:N_PAGES], refs[N_PAGES:2 * N_PAGES], refs[2 * N_PAGES]
    nb = PAST_LEN // MOBA_BLOCK
    ppb = MOBA_BLOCK // PAGE
    pos = lax.broadcasted_iota(jnp.int32, (1, PAST_LEN), 1)
    dist = (PAST_LEN - pos).astype(F32)
    blk = pos // MOBA_BLOCK
    for h in range(MOBA_HEADS):
        sl = slice(h * HD, (h + 1) * HD)
        q = q_ref[0, :, sl]
        k_pg = [p[h] for p in k_pages]
        km = [sum(jnp.sum(k_pg[j * ppb + i], axis=0, keepdims=True) for i in range(ppb)) / MOBA_BLOCK
              for j in range(nb)]
        sc = [_rowdot(kmj, q) for kmj in km]
        allowed = jnp.zeros((1, PAST_LEN), F32)
        for j in range(nb):
            beats = [(sc[i] > sc[j]) | ((sc[i] == sc[j]) & (i < j)) for i in range(nb) if i != j]
            rank = sum(b.astype(F32) for b in beats)
            allowed = jnp.where(blk == j, jnp.where(rank < MOBA_TOPK, 1.0, 0.0), allowed)
        q8 = jnp.concatenate([q, jnp.zeros((7, HD), F32)], axis=0)
        k_bf = jnp.concatenate([kp.astype(BF) for kp in k_pg], axis=0)
        v_bf = jnp.concatenate([p[h].astype(BF) for p in v_pages], axis=0)
        s = _dot_nt(q8.astype(BF), k_bf) * SCALE - slopes_ref[h] * dist
        s = jnp.where(allowed > 0.5, s, -jnp.inf)
        s_own = _rowdot(q8, kn_ref[0, :, sl]) * SCALE
        m = jnp.maximum(jnp.max(s, axis=1, keepdims=True), s_own)
        p = jnp.exp(s - m)
        p_own = jnp.exp(s_own - m)
        den = jnp.sum(p, axis=1, keepdims=True) + p_own
        o = (_dot(p.astype(BF), v_bf) + p_own * vn_ref[0, :, sl]) / den
        o_ref[0, :, sl] = o[0:1]


def _moba_sample(page_table, slopes, q, k_new, v_new, k_pool, v_pool):
    row = lambda: pl.BlockSpec((1, 1, MOBA_W), lambda n, pt: (n, 0, 0))
    page = lambda p: pl.BlockSpec((None, MOBA_HEADS, PAGE, HD), lambda n, pt: (pt[n, p], 0, 0, 0))
    return pl.pallas_call(
        _moba_sample_kernel,
        grid_spec=pltpu.PrefetchScalarGridSpec(
            num_scalar_prefetch=1, grid=(DEC_BATCH,),
            in_specs=[pl.BlockSpec(memory_space=pltpu.SMEM), row(), row(), row()]
            + [page(p) for p in range(N_PAGES)] * 2,
            out_specs=row()),
        out_shape=jax.ShapeDtypeStruct((DEC_BATCH, 1, MOBA_W), F32),
        compiler_params=_params(1),
        name="moba_sample",
    )(page_table, slopes, q, k_new, v_new, *([k_pool] * N_PAGES), *([v_pool] * N_PAGES))


NS = 8
ROWS_PER_PAGE = PAGE * NSA_G
N_SEL = NSA_TOPN - 1
N_CAND_S = PAST_LEN // SEL_BLOCK


def _nsa_sample_cmp_kernel(pt_ref, slopes_ref, kcn_ref, vcn_ref, q_ref, pe_ref, w1_ref, w2_ref, cmat_ref, tri_ref,
                           *refs):
    del pt_ref
    k_pages, v_pages = refs[:N_PAGES], refs[N_PAGES:2 * N_PAGES]
    ocmp_ref, idx_ref, stage_ref = refs[2 * N_PAGES:2 * N_PAGES + 3]
    s_in = pl.program_id(1)
    unit = N_CHUNK
    chunks_per_page = PAGE // CMP_STRIDE

    rg = CMP_STRIDE * NSA_G
    for which, pages in enumerate((k_pages, v_pages)):
        for pair in range(N_PAGES // 2):
            slabs = [jnp.swapaxes(pages[2 * pair + i][...].reshape(chunks_per_page, rg, HD), 0, 1) for i in range(2)]
            for g in range(NSA_G):
                base = pl.multiple_of(s_in * (NSA_G * unit) + g * unit, unit) + pair * 2 * chunks_per_page
                for r in range(CMP_STRIDE):
                    x = jnp.concatenate([slabs[0][NSA_G * r + g], slabs[1][NSA_G * r + g]], axis=0)
                    stage_ref[which, pl.ds(base, 2 * chunks_per_page), r * HD:(r + 1) * HD] = x.astype(BF)

    @pl.when(s_in == NS - 1)
    def _():
        n_unit = NS * NSA_G
        rows = n_unit * unit
        row = lax.broadcasted_iota(jnp.int32, (rows, HD), 0)
        toks = []
        for which, new_ref in enumerate((kcn_ref, vcn_ref)):
            w1 = w1_ref.at[which]
            acc = _dot(stage_ref[which], w1[...])
            top, bot = acc[:, :HD], acc[:, HD:]
            nxt = pltpu.roll(bot, shift=rows - 1, axis=0)
            new_rows = jnp.concatenate([new_ref[s:s + 1, g * HD:(g + 1) * HD]
                                        for s in range(NS) for g in range(NSA_G)], axis=0)
            b_new = _dot(new_rows.astype(BF), w1[0:HD, HD:])
            b_last = jnp.concatenate([jnp.broadcast_to(b_new[u:u + 1], (unit, HD)) for u in range(n_unit)], axis=0)
            c_row = jnp.zeros((1, HD), F32)
            for r in range(CMP_LEN):
                half, rr = divmod(r, CMP_STRIDE)
                c_row = c_row + _dot(pe_ref[which, :, r * HD:(r + 1) * HD],
                                     w1[rr * HD:(rr + 1) * HD, half * HD:(half + 1) * HD])
            pre = top + jnp.where(row % unit == unit - 1, b_last, nxt) + c_row
            toks.append(_dot(_gelu_tanh(pre).astype(BF), w2_ref[which]))
        ck_all, cv_all = toks

        lane = lax.broadcasted_iota(jnp.int32, (8, HD), 1)
        row8 = lax.broadcasted_iota(jnp.int32, (8, HD), 0)
        dist = PAST_LEN - (lane * CMP_STRIDE + CMP_LEN - 1)
        p_rows = []
        for s in range(NS):
            for g in range(NSA_G):
                u = s * NSA_G + g
                ck_hi, ck_lo = _split_bf16(ck_all[u * unit:(u + 1) * unit])
                cv_bf = cv_all[u * unit:(u + 1) * unit].astype(BF)
                q8 = jnp.concatenate([q_ref[s:s + 1, (g * NSA_R + r) * HD:(g * NSA_R + r + 1) * HD]
                                      for r in range(NSA_R)] + [jnp.zeros((8 - NSA_R, HD), F32)], axis=0)
                slope = jnp.where(row8[:, 0:1] == 0, slopes_ref[g * NSA_R],
                                  jnp.where(row8[:, 0:1] == 1, slopes_ref[g * NSA_R + 1], slopes_ref[g * NSA_R + 2]))
                o8, p8 = _cmp_attend(q8, ck_hi, ck_lo, cv_bf, slope, dist)
                ocmp_ref[s, g] = o8
                p_rows.append(jnp.sum(jnp.where(row8 < NSA_R, p8, 0.0), axis=0, keepdims=True))
        p_grp = jnp.concatenate(p_rows + [jnp.zeros((HD - n_unit, HD), F32)], axis=0)
        member = _select_blocks(p_grp, cmat_ref[...], jnp.full((EXT_MASK, HD), N_CAND_S, jnp.int32), N_CAND_S, N_SEL)
        member_pad = jnp.concatenate([member, jnp.zeros((HD - EXT_MASK, HD), F32)], axis=0)
        rank = _dot(tri_ref[...], member_pad.astype(BF))[:EXT_MASK]
        cand = lax.broadcasted_iota(jnp.int32, (EXT_MASK, HD), 0).astype(F32)
        picks = [jnp.sum(jnp.where((member > 0.5) & (rank == c), cand, 0.0), axis=0, keepdims=True)
                 for c in range(N_SEL)]
        idx_ref[0] = jnp.concatenate(picks + [jnp.zeros((1, HD), F32)], axis=0).astype(jnp.int32)


def _nsa_sample_cmp(page_table, slopes, kc_new, vc_new, q, pe_bf, w1_flat, w2_bf, cmat, tri, ck_pool, cv_pool):
    page = lambda p: pl.BlockSpec((ROWS_PER_PAGE, HD), lambda i, s, pt: (pt[i * NS + s, p], 0))
    full = lambda a: pl.BlockSpec(a.shape, lambda i, s, pt: (0,) * a.ndim)
    rows = lambda w: pl.BlockSpec((NS, w), lambda i, s, pt: (i, 0))
    out = lambda: pl.BlockSpec((NS, NSA_G, 8, HD), lambda i, s, pt: (i, 0, 0, 0))
    return pl.pallas_call(
        _nsa_sample_cmp_kernel,
        grid_spec=pltpu.PrefetchScalarGridSpec(
            num_scalar_prefetch=1, grid=(DEC_BATCH // NS, NS),
            in_specs=[pl.BlockSpec(memory_space=pltpu.SMEM), rows(KV_W), rows(KV_W), rows(NSA_W),
                      full(pe_bf), full(w1_flat), full(w2_bf), full(cmat), full(tri)]
            + [page(p) for p in range(N_PAGES)] * 2,
            out_specs=[out(), pl.BlockSpec((1, N_SEL + 1, HD), lambda i, s, pt: (i, 0, 0))],
            scratch_shapes=[pltpu.VMEM((2, NS * NSA_G * N_CHUNK, CMP_STRIDE * HD), BF)]),
        out_shape=[jax.ShapeDtypeStruct((DEC_BATCH, NSA_G, 8, HD), F32),
                   jax.ShapeDtypeStruct((DEC_BATCH // NS, N_SEL + 1, HD), jnp.int32)],
        compiler_params=_params(2),
        name="nsa_sample_cmp",
    )(page_table, slopes, kc_new, vc_new, q, pe_bf, w1_flat, w2_bf, cmat, tri,
      *([ck_pool] * N_PAGES), *([cv_pool] * N_PAGES))


def _attend_rows(q8, slope, k, v, dist, k_own, v_own):
    s = _dot_nt(q8.astype(BF), k.astype(BF)) * SCALE - slope * dist
    s_own = _rowdot(q8, k_own) * SCALE
    m = jnp.maximum(jnp.max(s, axis=1, keepdims=True), s_own)
    p = jnp.exp(s - m)
    p_own = jnp.exp(s_own - m)
    den = jnp.sum(p, axis=1, keepdims=True) + p_own
    return (_dot(p.astype(BF), v.astype(BF)) + p_own * v_own) / den


BLK_ROWS = SEL_BLOCK * NSA_G


def _nsa_sample_attend_kernel(pt_ref, ix_ref, slopes_ref, q_ref, gn_ref, ksn_ref, vsn_ref, kwn_ref, vwn_ref,
                              ocmp_ref, wk_ref, wv_ref, sk_hbm, sv_hbm, o_ref, wko_ref, wvo_ref, kbuf, vbuf, sem):
    n = pl.program_id(0)
    halves = PAGE // SEL_BLOCK

    def block_copies(sample, slot):
        copies = []
        for g in range(NSA_G):
            for c in range(N_SEL):
                j = ix_ref[(sample * NSA_G + g) * N_SEL + c]
                row0 = pl.multiple_of((pt_ref[sample, j // halves] * halves + j % halves) * BLK_ROWS, BLK_ROWS)
                dst = pl.ds((g * N_SEL + c) * BLK_ROWS, BLK_ROWS)
                for pool, buf in ((sk_hbm, kbuf), (sv_hbm, vbuf)):
                    copies.append(pltpu.make_async_copy(pool.at[pl.ds(row0, BLK_ROWS), :], buf.at[slot, dst, :],
                                                        sem.at[slot]))
        return copies

    @pl.when(n == 0)
    def _():
        for cp in block_copies(0, 0):
            cp.start()

    @pl.when(n + 1 < DEC_BATCH)
    def _():
        for cp in block_copies(n + 1, (n + 1) % 2):
            cp.start()

    slot = n % 2
    for cp in block_copies(n, slot):
        cp.wait()

    wl = wk_ref.shape[0] // NSA_G
    off = lax.broadcasted_iota(jnp.int32, (1, SEL_BLOCK), 1)
    dist_w = (wl - lax.broadcasted_iota(jnp.int32, (1, wl), 1)).astype(F32)
    row8 = lax.broadcasted_iota(jnp.int32, (8, 1), 0)
    gates = jax.nn.sigmoid(gn_ref[0])
    for g in range(NSA_G):
        gsl = slice(g * HD, (g + 1) * HD)
        heads = [g * NSA_R + r for r in range(NSA_R)]
        q8 = jnp.concatenate([q_ref[0, :, hd * HD:(hd + 1) * HD] for hd in heads]
                             + [jnp.zeros((8 - NSA_R, HD), F32)], axis=0)
        slope = jnp.where(row8 == 0, slopes_ref[heads[0]],
                          jnp.where(row8 == 1, slopes_ref[heads[1]], slopes_ref[heads[2]]))
        sel_rows = pl.ds(g * N_SEL * BLK_ROWS + g, N_SEL * SEL_BLOCK, stride=NSA_G)
        pos = jnp.concatenate([ix_ref[(n * NSA_G + g) * N_SEL + c] * SEL_BLOCK + off for c in range(N_SEL)], axis=1)
        o_sel = _attend_rows(q8, slope, kbuf[slot, sel_rows, :], vbuf[slot, sel_rows, :],
                             (PAST_LEN - pos).astype(F32), ksn_ref[0, :, gsl], vsn_ref[0, :, gsl])
        o_win = _attend_rows(q8, slope, wk_ref[pl.ds(g, wl, stride=NSA_G), :], wv_ref[pl.ds(g, wl, stride=NSA_G), :],
                             dist_w, kwn_ref[0, :, gsl], vwn_ref[0, :, gsl])
        for r, hd in enumerate(heads):
            o_ref[0, :, hd * HD:(hd + 1) * HD] = (
                gates[:, 3 * hd:3 * hd + 1] * ocmp_ref[0, g, r:r + 1, :]
                + gates[:, 3 * hd + 1:3 * hd + 2] * o_sel[r:r + 1] + gates[:, 3 * hd + 2:3 * hd + 3] * o_win[r:r + 1])

    for cache, new, out in ((wk_ref, kwn_ref, wko_ref), (wv_ref, vwn_ref, wvo_ref)):
        out[...] = pltpu.roll(cache[...], shift=(wl - 1) * NSA_G, axis=0)
        for g in range(NSA_G):
            out[pl.ds((wl - 1) * NSA_G + g, 1), :] = new[0, :, g * HD:(g + 1) * HD]


def _nsa_sample_attend(page_table, sel_idx, slopes, q, gn, ks_new, vs_new, kw_new, vw_new, o_cmp,
                       win_k, win_v, sk_pool, sv_pool):
    row = lambda w: pl.BlockSpec((1, 1, w), lambda n, pt, ix: (n, 0, 0))
    win_rows = win_k.shape[0] // DEC_BATCH
    win = lambda: pl.BlockSpec((win_rows, HD), lambda n, pt, ix: (n, 0))
    any_spec = pl.BlockSpec(memory_space=pl.ANY)
    buf = pltpu.VMEM((2, NSA_G * N_SEL * BLK_ROWS, HD), F32)
    return pl.pallas_call(
        _nsa_sample_attend_kernel,
        grid_spec=pltpu.PrefetchScalarGridSpec(
            num_scalar_prefetch=2, grid=(DEC_BATCH,),
            in_specs=[pl.BlockSpec(memory_space=pltpu.SMEM), row(NSA_W), row(HD), row(KV_W), row(KV_W), row(KV_W),
                      row(KV_W), pl.BlockSpec((1, NSA_G, 8, HD), lambda n, pt, ix: (n, 0, 0, 0)), win(), win(),
                      any_spec, any_spec],
            out_specs=[row(NSA_W), win(), win()],
            scratch_shapes=[buf, buf, pltpu.SemaphoreType.DMA((2,))]),
        out_shape=[jax.ShapeDtypeStruct((DEC_BATCH, 1, NSA_W), F32), jax.ShapeDtypeStruct(win_k.shape, F32),
                   jax.ShapeDtypeStruct(win_v.shape, F32)],
        compiler_params=_params(1),
        name="nsa_sample_attend",
    )(page_table, sel_idx, slopes, q, gn, ks_new, vs_new, kw_new, vw_new, o_cmp, win_k, win_v, sk_pool, sv_pool)


def _mem_sample_kernel(q_ref, k_ref, v_ref, o_ref):
    for h in range(MEM_HEADS):
        sl = slice(h * HD, (h + 1) * HD)
        q = q_ref[0, :, sl]
        k = k_ref[pl.ds(h, MEM_LEN, stride=MEM_HEADS), :]
        v = v_ref[pl.ds(h, MEM_LEN, stride=MEM_HEADS), :]
        s = _rowdot(k, q) * SCALE
        p = jnp.exp(s - jnp.max(s, axis=0, keepdims=True))
        o_ref[0, :, sl] = jnp.sum(p * v, axis=0, keepdims=True) / jnp.sum(p, axis=0, keepdims=True)


def _mem_sample(q, mem_k, mem_v):
    rows = MEM_LEN * MEM_HEADS
    row = lambda: pl.BlockSpec((1, 1, MEM_W), lambda n: (n, 0, 0))
    kv = lambda: pl.BlockSpec((rows, HD), lambda n: (n, 0))
    return pl.pallas_call(
        _mem_sample_kernel,
        grid=(DEC_BATCH,),
        in_specs=[row(), kv(), kv()],
        out_specs=row(),
        out_shape=jax.ShapeDtypeStruct((DEC_BATCH, 1, MEM_W), F32),
        compiler_params=_params(1),
        name="mem_attend_sample",
    )(q, mem_k, mem_v)


def _project_all(x_bf, w, tm, paged, tag):
    kv_kind = "pages" if paged else "flat"
    qa, ka, va, qb = _proj(x_bf, w["wa"], w["ba"], MOBA_W, ["flat", kv_kind, kv_kind, "flat"], tm,
                           "proj_moba_q_" + tag, True)
    kc, vc, ks, vs, kw, vw = _proj(x_bf, w["wb"], w["bb"], KV_W, ["flat"] * 6, tm, "proj_nsa_kv_" + tag, True)
    (qm,) = _proj(x_bf, w["wm"], w["bm"], MEM_W, ["flat"], tm, "proj_mem_q_" + tag, True)
    (gn,) = _proj(x_bf, w["wg"], w["bg"], HD, ["flat"], tm, "proj_nsa_gate_" + tag, True)
    gbr = _matmul(x_bf, w["wd"], w["bd"], tm, 768, "proj_branch_gate_" + tag)
    return qa, ka, va, qb, kc, vc, ks, vs, kw, vw, qm, gn, gbr


def _strict_lower_ones():
    return jnp.asarray(np.tril(np.ones((HD, HD), np.float32), -1), BF)


def kernel(x_prompt, x_sample, cache_mem_k, cache_mem_v, cache_moba_k, cache_moba_v, cache_nsa_cmp_k, cache_nsa_cmp_v, cache_nsa_sel_k, cache_nsa_sel_v, cache_nsa_win_k, cache_nsa_win_v, cache_ffn_conv, page_table, mem_prompt, w_in, b_in, w_mem_kv, cmp_pe, cmp_w1, cmp_w2, p_moba, p_nsa, p_mem, w_o, ln1_g, ln1_b, w_up, conv_w, conv_b, w_down, ln2_g, ln2_b):
    slopes = _alibi_slopes(MOBA_HEADS)
    w_bf = w_in[0].T.astype(BF)
    b0 = b_in[0][None, :]
    c0, c1, c2, c3 = 4 * MOBA_W, 4 * MOBA_W + 6 * KV_W, 4 * MOBA_W + 6 * KV_W + N_GATE, 4 * MOBA_W + 6 * KV_W + N_GATE + MEM_W
    w = {"wa": w_bf[:c0], "ba": b0[:, :c0], "wb": w_bf[c0:c1], "bb": b0[:, c0:c1],
         "wg": jnp.pad(w_bf[c1:c2], ((0, HD - N_GATE), (0, 0))), "bg": jnp.pad(b0[:, c1:c2], ((0, 0), (0, HD - N_GATE))),
         "wm": w_bf[c2:c3], "bm": b0[:, c2:c3], "wd": w_bf[c3:], "bd": b0[:, c3:]}
    pe_bf, w1_bf, w2_bf = _pack_cmp_weights(cmp_pe[0], cmp_w1[0], cmp_w2[0])
    pa_bf, pb_bf, pm_bf, wo_bf = p_moba[0].astype(BF), p_nsa[0].astype(BF), p_mem[0].astype(BF), w_o[0].astype(BF)
    wu_bf, wd_bf = w_up[0].astype(BF), w_down[0].astype(BF)

    xp = x_prompt.reshape(BATCH * SEQ, D_MODEL)
    xp_bf = xp.astype(BF)
    qa, ka, va, qb, kc, vc, ks, vs, kw, vw, qm, gn, gbr = _project_all(xp_bf, w, 1024, True, "prompt")
    mem_k, mem_v = _proj(mem_prompt.reshape(BATCH * MEM_LEN, D_MODEL).astype(BF), w_mem_kv[0].astype(BF),
                         jnp.zeros((1, 2 * MEM_W), F32), MEM_W, ["flat", "flat"], 512, "proj_mem_kv")
    slope_rows = _slope_rows(MOBA_HEADS)
    o_a = _moba_prompt(slope_rows, qa, ka, va)
    ck, cv = _compress_prompt(kc, vc, pe_bf, w1_bf, w2_bf)
    cmat = _cmp_to_sel_matrix(N_CHUNK - 1, SEQ // SEL_BLOCK - 1)
    o_b = _nsa_prompt(slopes, slope_rows, qb, gn, ks, vs, kw, vw, ck, cv, cmat)
    o_m = _mem_attend_prompt(qm, mem_k, mem_v)
    merged = _merge(o_a, o_b, o_m, gbr, pa_bf, pb_bf, pm_bf, 1024, "prompt")
    h, h_bf = _out_ln(xp, merged, wo_bf, ln1_g, ln1_b, 512, "prompt")
    y_p, u_tail = _ffn(h, h_bf, wu_bf, conv_w[0], conv_b, wd_bf, ln2_g, ln2_b, None, None, FFN_TM, SEQ, "prompt")

    pages = lambda a: a.reshape(1, BATCH, SEQ // PAGE, PAGE, NSA_G, HD)
    head_pages = lambda a: a.transpose(0, 1, 3, 2, 4)[None]
    last = lambda a: a.reshape(BATCH, SEQ, NSA_G, HD)[None, :, SEQ - WINDOW:]
    tiles_per_seq = SEQ // FFN_TM
    conv_p = u_tail.reshape(BATCH, tiles_per_seq, HALO, FFN_DIM)[:, -1, HALO - 2:][None]
    prompt_out = (y_p.reshape(BATCH, SEQ, D_MODEL),
                  mem_k.reshape(1, BATCH, MEM_LEN, MEM_HEADS, HD), mem_v.reshape(1, BATCH, MEM_LEN, MEM_HEADS, HD),
                  head_pages(ka), head_pages(va), pages(kc), pages(vc), pages(ks), pages(vs), last(kw), last(vw), conv_p)

    n = DEC_BATCH
    xs = x_sample.reshape(n, D_MODEL)
    qa, ka, va, qb, kc, vc, ks, vs, kw, vw, qm, gn, gbr = _project_all(xs.astype(BF), w, n, False, "sample")
    r3 = lambda a: a.reshape(n, 1, a.shape[-1])
    moba_pool = lambda c: c[0].transpose(0, 2, 1, 3)
    rows2d = lambda c: c.reshape(-1, HD)
    o_a = _moba_sample(page_table, slopes, r3(qa), r3(ka), r3(va), moba_pool(cache_moba_k), moba_pool(cache_moba_v))
    w1_flat = w1_bf.reshape(2, CMP_STRIDE * HD, 2 * HD)
    cmat_s = _cmp_to_sel_matrix(N_CHUNK, N_CAND_S)
    o_cmp, sel = _nsa_sample_cmp(page_table, slopes, kc, vc, qb, pe_bf, w1_flat, w2_bf, cmat_s, _strict_lower_ones(),
                                 rows2d(cache_nsa_cmp_k), rows2d(cache_nsa_cmp_v))
    sel_idx = sel[:, :N_SEL, :NS * NSA_G].transpose(0, 2, 1).reshape(-1)
    o_b, win_k_new, win_v_new = _nsa_sample_attend(
        page_table, sel_idx, slopes, r3(qb), r3(gn), r3(ks), r3(vs), r3(kw), r3(vw), o_cmp,
        rows2d(cache_nsa_win_k), rows2d(cache_nsa_win_v), rows2d(cache_nsa_sel_k), rows2d(cache_nsa_sel_v))
    o_m = _mem_sample(r3(qm), rows2d(cache_mem_k), rows2d(cache_mem_v))
    flat_bf = lambda a: a.reshape(n, a.shape[-1]).astype(BF)
    merged = _merge(flat_bf(o_a), flat_bf(o_b), flat_bf(o_m), gbr, pa_bf, pb_bf, pm_bf, n, "sample")
    h, h_bf = _out_ln(xs, merged, wo_bf, ln1_g, ln1_b, n, "sample")
    conv_old = cache_ffn_conv[0]
    y_s, u_new = _ffn(h, h_bf, wu_bf, conv_w[0], conv_b, wd_bf, ln2_g, ln2_b, conv_old[:, 0], conv_old[:, 1], n, 0,
                      "sample")

    heads = lambda a, nh: a.reshape(1, n, 1, nh, HD)
    sample_out = (heads(ka, MOBA_HEADS), heads(va, MOBA_HEADS), heads(kc, NSA_G), heads(vc, NSA_G),
                  heads(ks, NSA_G), heads(vs, NSA_G), win_k_new.reshape(cache_nsa_win_k.shape),
                  win_v_new.reshape(cache_nsa_win_v.shape),
                  jnp.stack([conv_old[:, 1], u_new], axis=1)[None])
    return (prompt_out[0], y_s.reshape(n, 1, D_MODEL)) + prompt_out[1:] + sample_out
```

```python
import functools

import numpy as np
import jax
import jax.numpy as jnp
from jax import lax
from jax.experimental import pallas as pl
from jax.experimental.pallas import tpu as pltpu

D_MODEL = 2048
BATCH = 4
SEQ = 2048
DEC_BATCH = 128
PAST_LEN = 2048
PAGE = 128
N_PAGES = PAST_LEN // PAGE
HD = 128
MOBA_HEADS = 6
MOBA_BLOCK = 256
MOBA_TOPK = 3
NSA_HEADS = 6
NSA_G = 2
NSA_R = 3
CMP_LEN = 32
CMP_STRIDE = 16
SEL_BLOCK = 64
NSA_TOPN = 16
WINDOW = 512
MEM_HEADS = 4
MEM_LEN = 256
FFN_DIM = 5632
DN_ALPHA = 2.0 ** 0.25
LN_EPS = 1e-5
SCALE = HD ** -0.5
MOBA_W = MOBA_HEADS * HD
NSA_W = NSA_HEADS * HD
KV_W = NSA_G * HD
MEM_W = MEM_HEADS * HD
N_GATE = NSA_HEADS * 3

F32 = jnp.float32
BF = jnp.bfloat16
NEG_BIG = -(2.0 ** 100)
VMEM_LIMIT = 48 * 1024 * 1024
TQ = 256


def _dot(a, b):
    return jnp.dot(a, b, preferred_element_type=F32)


def _dot_nt(a, b):
    return lax.dot_general(a, b, (((1,), (1,)), ((), ())), preferred_element_type=F32)


def _split_bf16(a):
    hi = a.astype(BF)
    lo = (a - hi.astype(F32)).astype(BF)
    return hi, lo


def _alibi_slopes(n):
    return jnp.asarray(2.0 ** (-8.0 * np.arange(1, n + 1) / n), F32)


def _gelu_tanh(x):
    return 0.5 * x * (1.0 + jnp.tanh(np.sqrt(2.0 / np.pi).astype(np.float32) * (x + 0.044715 * (x * x * x))))


def _params(n_axes):
    return pltpu.CompilerParams(dimension_semantics=("arbitrary",) * n_axes, vmem_limit_bytes=VMEM_LIMIT)


def _proj_kernel(x_ref, w_ref, b_ref, *out_refs, kinds, w_rows):
    j = pl.program_id(1)
    acc = (_dot_nt if w_rows else _dot)(x_ref[...], w_ref[...]) + b_ref[...]
    for idx, (kind, o_ref) in enumerate(zip(kinds, out_refs)):
        @pl.when(j == idx)
        def _(kind=kind, o_ref=o_ref):
            if kind == "flat":
                o_ref[...] = acc
            else:
                for p in range(o_ref.shape[1]):
                    for h in range(o_ref.shape[2]):
                        o_ref[0, p, h] = acc[p * PAGE:(p + 1) * PAGE, h * HD:(h + 1) * HD]


def _proj(x_bf, w_bf, b, tn, kinds, tm, name, w_rows=False):
    rows, k = x_bf.shape
    n_out = len(kinds)
    assert w_bf.shape == ((n_out * tn, k) if w_rows else (k, n_out * tn)) and rows % tm == 0
    w_spec = pl.BlockSpec((tn, k), lambda i, j: (j, 0)) if w_rows else pl.BlockSpec((k, tn), lambda i, j: (0, j))
    tiles_per_batch = SEQ // tm if tm <= SEQ else 1
    out_shape, out_specs = [], []
    for idx, kind in enumerate(kinds):
        if kind == "flat":
            out_shape.append(jax.ShapeDtypeStruct((rows, tn), F32))
            out_specs.append(pl.BlockSpec((tm, tn), lambda i, j: (i, 0)))
        else:
            heads = tn // HD
            out_shape.append(jax.ShapeDtypeStruct((rows // SEQ, SEQ // PAGE, heads, PAGE, HD), F32))
            out_specs.append(pl.BlockSpec((1, tm // PAGE, heads, PAGE, HD),
                                          lambda i, j: (i // tiles_per_batch, i % tiles_per_batch, 0, 0, 0)))
    return pl.pallas_call(
        functools.partial(_proj_kernel, kinds=tuple(kinds), w_rows=w_rows),
        grid=(rows // tm, n_out),
        in_specs=[pl.BlockSpec((tm, k), lambda i, j: (i, 0)), w_spec,
                  pl.BlockSpec((1, tn), lambda i, j: (0, j))],
        out_specs=out_specs,
        out_shape=out_shape,
        compiler_params=_params(2),
        name=name,
    )(x_bf, w_bf, b)


def _matmul_kernel(x_ref, w_ref, b_ref, o_ref):
    o_ref[...] = _dot_nt(x_ref[...], w_ref[...]) + b_ref[...]


def _matmul(x_bf, w_bf, b, tm, tn, name):
    rows, k = x_bf.shape
    n = w_bf.shape[0]
    return pl.pallas_call(
        _matmul_kernel,
        grid=(rows // tm, n // tn),
        in_specs=[pl.BlockSpec((tm, k), lambda i, j: (i, 0)),
                  pl.BlockSpec((tn, k), lambda i, j: (j, 0)),
                  pl.BlockSpec((1, tn), lambda i, j: (0, j))],
        out_specs=pl.BlockSpec((tm, tn), lambda i, j: (i, j)),
        out_shape=jax.ShapeDtypeStruct((rows, n), F32),
        compiler_params=_params(2),
        name=name,
    )(x_bf, w_bf, b)


def _topk_member(v, cand, n_cand, k):
    member = jnp.zeros(v.shape, F32)
    for j in range(n_cand):
        vj = v[j:j + 1, :]
        beats = (v > vj) | ((v == vj) & (cand < j))
        cnt = jnp.sum(beats.astype(F32), axis=0, keepdims=True)
        member = jnp.where((cand == j) & (cnt < k), 1.0, member)
    return member


def _to_rows(member_t):
    pad = jnp.zeros((HD - member_t.shape[0], member_t.shape[1]), F32)
    return jnp.transpose(jnp.concatenate([member_t, pad], axis=0))


EXT_MASK = 32
N_PIECE = 3
POS_SPLIT = 64
EXP2_SCALE = float(SCALE * np.log2(np.e))


def _slope_rows(n_heads):
    slope = np.asarray(2.0 ** (-8.0 * np.arange(1, n_heads + 1) / n_heads), np.float32) / np.float32(SCALE)
    rows = np.zeros((8, HD), np.float32)
    rest = slope.astype(np.float32)
    for i in range(N_PIECE):
        piece = rest.astype(BF).astype(np.float32)
        rows[:n_heads, EXT_MASK + i] = piece * POS_SPLIT
        rows[:n_heads, EXT_MASK + N_PIECE + i] = piece
        rest = rest - piece
    return jnp.asarray(rows)


def _key_ext(n_keys, block):
    key = lax.broadcasted_iota(jnp.int32, (n_keys, HD), 0)
    lane = lax.broadcasted_iota(jnp.int32, (n_keys, HD), 1)
    hot = jnp.where(key // block == lane, 1.0, 0.0)
    pos = jnp.where(lane < EXT_MASK + N_PIECE, (key // POS_SPLIT).astype(F32), (key % POS_SPLIT).astype(F32))
    return jnp.where(lane < EXT_MASK, hot, jnp.where(lane < EXT_MASK + 2 * N_PIECE, pos, 0.0)).astype(BF)


def _query_ext(allowed, slope_row, lane):
    mask = 0.0 if allowed is None else jnp.where(allowed, 0.0, NEG_BIG)
    return jnp.where(lane < EXT_MASK, mask, slope_row).astype(BF)


def _flash_init(qx, kx, v_bf, keep):
    s = jnp.where(keep, _dot_nt(qx, kx), NEG_BIG)
    m = jnp.max(s, axis=1, keepdims=True)
    p = jnp.exp2((s - m) * EXP2_SCALE)
    return m, jnp.sum(p, axis=1, keepdims=True), _dot(p.astype(BF), v_bf)


def _flash_step(state, qx, kx, v_bf, keep=None):
    m_old, l_old, acc_old = state
    s = _dot_nt(qx, kx)
    if keep is not None:
        s = jnp.where(keep, s, NEG_BIG)
    m_new = jnp.maximum(m_old, jnp.max(s, axis=1, keepdims=True))
    alpha = jnp.exp2((m_old - m_new) * EXP2_SCALE)
    p = jnp.exp2((s - m_new) * EXP2_SCALE)
    return (m_new, alpha * l_old + jnp.sum(p, axis=1, keepdims=True),
            alpha * acc_old + _dot(p.astype(BF), v_bf))


MOBA_HB = 6


def _moba_prompt_kernel(srow_ref, q_ref, k_ref, v_ref, o_ref, kx_ref, vb_ref, kmh_ref, kml_ref):
    hg = pl.program_id(1)
    qi = pl.program_id(2)
    nb = SEQ // MOBA_BLOCK

    @pl.when(qi == 0)
    def _():
        ext = _key_ext(SEQ, MOBA_BLOCK)
        for hh in range(MOBA_HB):
            k = k_ref[:, hh].reshape(SEQ, HD)
            kx_ref[hh, :, :HD] = k.astype(BF)
            kx_ref[hh, :, HD:] = ext
            vb_ref[hh] = v_ref[:, hh].reshape(SEQ, HD).astype(BF)
            km = jnp.sum(k.reshape(nb, MOBA_BLOCK, HD), axis=1) / MOBA_BLOCK
            hi, lo = _split_bf16(jnp.concatenate([km, jnp.zeros((HD - nb, HD), F32)], axis=0))
            kmh_ref[hh] = hi
            kml_ref[hh] = lo

    lane = lax.broadcasted_iota(jnp.int32, (TQ, HD), 1)
    cand = lax.broadcasted_iota(jnp.int32, (nb, TQ), 0)
    rr = lax.broadcasted_iota(jnp.int32, (TQ, MOBA_BLOCK), 0)
    cc = lax.broadcasted_iota(jnp.int32, (TQ, MOBA_BLOCK), 1)
    own = pl.multiple_of(qi * MOBA_BLOCK, MOBA_BLOCK)
    qx, states = [], []
    for hh in range(MOBA_HB):
        q = q_ref[:, hh * HD:(hh + 1) * HD]
        qh, ql = _split_bf16(q)
        sc = (_dot_nt(kmh_ref[hh], qh) + _dot_nt(kml_ref[hh], qh) + _dot_nt(kmh_ref[hh], ql))[:nb]
        sc = jnp.where(cand < qi, sc, -jnp.inf)
        member = _topk_member(sc, cand, nb - 1, MOBA_TOPK)
        allowed = (_to_rows(member) > 0.5) | (lane == qi)
        ext = _query_ext(allowed, srow_ref[pl.ds(hg * MOBA_HB + hh, 1), :], lane)
        qx.append(jnp.concatenate([qh, ext], axis=1))
        states.append(_flash_init(qx[hh], kx_ref[hh, pl.ds(own, MOBA_BLOCK), :],
                                  vb_ref[hh, pl.ds(own, MOBA_BLOCK), :], cc <= rr))

    def body(c, states):
        rows = pl.ds(pl.multiple_of(c * MOBA_BLOCK, MOBA_BLOCK), MOBA_BLOCK)
        return tuple(_flash_step(states[hh], qx[hh], kx_ref[hh, rows, :], vb_ref[hh, rows, :])
                     for hh in range(MOBA_HB))

    states = lax.fori_loop(0, qi, body, tuple(states))
    for hh in range(MOBA_HB):
        m, l, acc = states[hh]
        o_ref[:, hh * HD:(hh + 1) * HD] = (acc / l).astype(BF)


def _moba_prompt(slope_rows, q, k_pages, v_pages):
    nq = SEQ // TQ
    kv_spec = pl.BlockSpec((None, SEQ // PAGE, MOBA_HB, PAGE, HD), lambda b, h, qi: (b, 0, h, 0, 0))
    q_spec = pl.BlockSpec((TQ, MOBA_HB * HD), lambda b, h, qi: (b * nq + qi, h))
    return pl.pallas_call(
        _moba_prompt_kernel,
        grid=(BATCH, MOBA_HEADS // MOBA_HB, nq),
        in_specs=[pl.BlockSpec((8, HD), lambda b, h, qi: (0, 0)), q_spec, kv_spec, kv_spec],
        out_specs=q_spec,
        out_shape=jax.ShapeDtypeStruct((BATCH * SEQ, MOBA_W), BF),
        scratch_shapes=[pltpu.VMEM((MOBA_HB, SEQ, 2 * HD), BF), pltpu.VMEM((MOBA_HB, SEQ, HD), BF),
                        pltpu.VMEM((MOBA_HB, HD, HD), BF), pltpu.VMEM((MOBA_HB, HD, HD), BF)],
        compiler_params=_params(3),
        name="moba_prompt",
    )(slope_rows, q, k_pages, v_pages)


N_CHUNK = SEQ // CMP_STRIDE


def _compress_tokens(get_x, w1_ref, w2_ref, c_row, b_last):
    acc = jnp.zeros((N_CHUNK, 2 * HD), F32)
    for r in range(CMP_STRIDE):
        acc = acc + _dot(get_x(r).astype(BF), w1_ref[r])
    top, bot = acc[:, :HD], acc[:, HD:]
    nxt = pltpu.roll(bot, shift=N_CHUNK - 1, axis=0)
    row = lax.broadcasted_iota(jnp.int32, (N_CHUNK, HD), 0)
    pre = top + jnp.where(row == N_CHUNK - 1, b_last, nxt) + c_row
    return _dot(_gelu_tanh(pre).astype(BF), w2_ref[...])


def _compress_prompt_kernel(kc_ref, vc_ref, pe_ref, w1_ref, w2_ref, ck_ref, cv_ref):
    for which, (src, dst) in enumerate(((kc_ref, ck_ref), (vc_ref, cv_ref))):
        w1 = w1_ref.at[which]
        c_row = _pe_term(pe_ref, w1, which)
        get_x = lambda r, src=src: src[pl.ds(r, N_CHUNK, stride=CMP_STRIDE), :]
        dst[0, 0] = _compress_tokens(get_x, w1, w2_ref.at[which], c_row, jnp.zeros((1, HD), F32))


def _pe_term(pe_ref, w1, which):
    c_row = jnp.zeros((1, HD), F32)
    for r in range(CMP_LEN):
        half, rr = divmod(r, CMP_STRIDE)
        c_row = c_row + _dot(pe_ref[which, :, r * HD:(r + 1) * HD], w1[rr, :, half * HD:(half + 1) * HD])
    return c_row


def _pack_cmp_weights(cmp_pe, cmp_w1, cmp_w2):
    w1 = cmp_w1.reshape(2, 2, CMP_STRIDE, HD, HD).transpose(0, 2, 3, 1, 4).reshape(2, CMP_STRIDE, HD, 2 * HD)
    return cmp_pe.reshape(2, 1, CMP_LEN * HD).astype(BF), w1.astype(BF), cmp_w2.astype(BF)


def _compress_prompt(kc, vc, pe_bf, w1_bf, w2_bf):
    full = lambda a: pl.BlockSpec(a.shape, lambda b, g: (0,) * a.ndim)
    return pl.pallas_call(
        _compress_prompt_kernel,
        grid=(BATCH, NSA_G),
        in_specs=[pl.BlockSpec((SEQ, HD), lambda b, g: (b, g)), pl.BlockSpec((SEQ, HD), lambda b, g: (b, g)),
                  full(pe_bf), full(w1_bf), full(w2_bf)],
        out_specs=[pl.BlockSpec((1, 1, N_CHUNK, HD), lambda b, g: (b, g, 0, 0))] * 2,
        out_shape=[jax.ShapeDtypeStruct((BATCH, NSA_G, N_CHUNK, HD), F32)] * 2,
        compiler_params=_params(2),
        name="nsa_compress_prompt",
    )(kc, vc, pe_bf, w1_bf, w2_bf)


def _cmp_to_sel_matrix(n_cmp, n_blocks):
    start = np.arange(n_cmp)[:, None] * CMP_STRIDE
    j0 = np.arange(n_blocks)[None, :] * SEL_BLOCK
    ov = np.minimum(start + CMP_LEN, j0 + SEL_BLOCK) - np.maximum(start, j0)
    m = np.zeros((HD, HD), np.float32)
    m[:n_blocks, :n_cmp] = (np.clip(ov, 0, None) / CMP_LEN).T
    return jnp.asarray(m, BF)


def _cmp_attend(q, ck_hi, ck_lo, cv_bf, slope, dist):
    qh, ql = _split_bf16(q)
    s = (_dot_nt(qh, ck_hi) + _dot_nt(qh, ck_lo) + _dot_nt(ql, ck_hi)) * SCALE - slope * dist.astype(F32)
    s = jnp.where(dist >= 0, s, -jnp.inf)
    m = jnp.max(s, axis=1, keepdims=True)
    m = jnp.where(m > -jnp.inf, m, 0.0)
    p = jnp.exp(s - m)
    p = p / jnp.maximum(jnp.sum(p, axis=1, keepdims=True), 1e-30)
    return _dot(p.astype(BF), cv_bf), p


def _select_blocks(p_grp, cmat_bf, cur, n_cand, k):
    ph, plo = _split_bf16(p_grp)
    imp = (_dot_nt(cmat_bf, ph) + _dot_nt(cmat_bf, plo))[:EXT_MASK]
    cand = lax.broadcasted_iota(jnp.int32, imp.shape, 0)
    valid = (cand < cur) & (cand < n_cand)
    forced = valid & ((cand == 0) | (cand == cur - 1))
    v = jnp.where(forced, jnp.inf, jnp.where(valid, imp, -jnp.inf))
    return jnp.where(valid, _topk_member(v, cand, n_cand, k), 0.0)


def _nsa_prompt_kernel(slopes_ref, srow_ref, q_ref, gn_ref, ks_ref, vs_ref, kw_ref, vw_ref, ck_ref, cv_ref, cmat_ref,
                       o_ref, kx_ref, kwx_ref, vsb_ref, vwb_ref):
    qi = pl.program_id(1)
    chunk = TQ
    rows = NSA_R * TQ
    groups = range(NSA_G)

    @pl.when(qi == 0)
    def _():
        ext = _key_ext(SEQ, SEL_BLOCK)
        for g in groups:
            gsl = slice(g * HD, (g + 1) * HD)
            kx_ref[g, :, :HD] = ks_ref[:, gsl].astype(BF)
            kx_ref[g, :, HD:] = ext
            kwx_ref[g, :, :HD] = kw_ref[:, gsl].astype(BF)
            kwx_ref[g, :, HD:] = ext
            vsb_ref[g] = vs_ref[:, gsl].astype(BF)
            vwb_ref[g] = vw_ref[:, gsl].astype(BF)

    lane = lax.broadcasted_iota(jnp.int32, (TQ, HD), 1)
    t = qi * TQ + lax.broadcasted_iota(jnp.int32, (TQ, HD), 0)
    stack = lambda parts: jnp.concatenate(parts, axis=0)
    dist_c = stack([t - (lane * CMP_STRIDE + CMP_LEN - 1)] * NSA_R)
    cur_t = (qi * TQ + lax.broadcasted_iota(jnp.int32, (EXT_MASK, TQ), 1)) // SEL_BLOCK

    qx, qw, o_cmp = [], [], []
    for g in groups:
        q_all = stack([q_ref[:, (g * NSA_R + r) * HD:(g * NSA_R + r + 1) * HD] for r in range(NSA_R)])
        ck_hi, ck_lo = _split_bf16(ck_ref[0, g])
        slope_col = stack([jnp.full((TQ, 1), slopes_ref[g * NSA_R + r], F32) for r in range(NSA_R)])
        o_c, p = _cmp_attend(q_all, ck_hi, ck_lo, cv_ref[0, g].astype(BF), slope_col, dist_c)
        o_cmp.append(o_c)
        p_grp = p[0:TQ] + p[TQ:2 * TQ] + p[2 * TQ:3 * TQ]
        member = _select_blocks(p_grp, cmat_ref[...], cur_t, SEQ // SEL_BLOCK - 1, NSA_TOPN - 1)
        allowed = (_to_rows(member) > 0.5) | (lane == t // SEL_BLOCK)
        srow = [srow_ref[g * NSA_R + r:g * NSA_R + r + 1, :] for r in range(NSA_R)]
        q_bf = q_all.astype(BF)
        qx.append(jnp.concatenate([q_bf, stack([_query_ext(allowed, srow[r], lane) for r in range(NSA_R)])], axis=1))
        qw.append(jnp.concatenate([q_bf, stack([_query_ext(None, srow[r], lane) for r in range(NSA_R)])], axis=1))

    rr = lax.broadcasted_iota(jnp.int32, (rows, chunk), 0) % TQ
    cc = lax.broadcasted_iota(jnp.int32, (rows, chunk), 1)
    own = pl.ds(pl.multiple_of(qi * chunk, chunk), chunk)
    past = lambda c: pl.ds(pl.multiple_of(c * chunk, chunk), chunk)

    st = tuple(_flash_init(qx[g], kx_ref[g, own, :], vsb_ref[g, own, :], cc <= rr) for g in groups)
    st = lax.fori_loop(0, qi, lambda c, st: tuple(
        _flash_step(st[g], qx[g], kx_ref[g, past(c), :], vsb_ref[g, past(c), :]) for g in groups), st)
    o_sel = [st[g][2] / st[g][1] for g in groups]

    st = tuple(_flash_init(qw[g], kwx_ref[g, own, :], vwb_ref[g, own, :], cc <= rr) for g in groups)

    def win_step(c, st):
        slack = jnp.where(c == qi - WINDOW // chunk, 0, chunk)
        return tuple(_flash_step(st[g], qw[g], kwx_ref[g, past(c), :], vwb_ref[g, past(c), :], rr - cc <= slack)
                     for g in groups)

    st = lax.fori_loop(jnp.maximum(qi - WINDOW // chunk, 0), qi, win_step, st)
    o_win = [st[g][2] / st[g][1] for g in groups]

    gates = jax.nn.sigmoid(gn_ref[:, 0:NSA_HEADS * 3])
    for g in groups:
        for r in range(NSA_R):
            sl = slice(r * TQ, (r + 1) * TQ)
            hd = g * NSA_R + r
            o = (gates[:, 3 * hd:3 * hd + 1] * o_cmp[g][sl] + gates[:, 3 * hd + 1:3 * hd + 2] * o_sel[g][sl]
                 + gates[:, 3 * hd + 2:3 * hd + 3] * o_win[g][sl])
            o_ref[:, hd * HD:(hd + 1) * HD] = o.astype(BF)


def _nsa_prompt(slopes, slope_rows, qb, gn, ks, vs, kw, vw, ck, cv, cmat):
    nq = SEQ // TQ
    kv_spec = lambda: pl.BlockSpec((SEQ, KV_W), lambda b, qi: (b, 0))
    c_spec = lambda: pl.BlockSpec((1, NSA_G, N_CHUNK, HD), lambda b, qi: (b, 0, 0, 0))
    return pl.pallas_call(
        _nsa_prompt_kernel,
        grid=(BATCH, nq),
        in_specs=[pl.BlockSpec(memory_space=pltpu.SMEM),
                  pl.BlockSpec((8, HD), lambda b, qi: (0, 0)),
                  pl.BlockSpec((TQ, NSA_W), lambda b, qi: (b * nq + qi, 0)),
                  pl.BlockSpec((TQ, HD), lambda b, qi: (b * nq + qi, 0)),
                  kv_spec(), kv_spec(), kv_spec(), kv_spec(), c_spec(), c_spec(),
                  pl.BlockSpec((HD, HD), lambda b, qi: (0, 0))],
        out_specs=pl.BlockSpec((TQ, NSA_W), lambda b, qi: (b * nq + qi, 0)),
        out_shape=jax.ShapeDtypeStruct((BATCH * SEQ, NSA_W), BF),
        scratch_shapes=[pltpu.VMEM((NSA_G, SEQ, 2 * HD), BF), pltpu.VMEM((NSA_G, SEQ, 2 * HD), BF),
                        pltpu.VMEM((NSA_G, SEQ, HD), BF), pltpu.VMEM((NSA_G, SEQ, HD), BF)],
        compiler_params=_params(2),
        name="nsa_prompt",
    )(slopes, slope_rows, qb, gn, ks, vs, kw, vw, ck, cv, cmat)


def _mem_attend_kernel(q_ref, k_ref, v_ref, o_ref):
    for h in range(MEM_HEADS):
        sl = slice(h * HD, (h + 1) * HD)
        s = _dot_nt(q_ref[:, sl].astype(BF), k_ref[:, sl].astype(BF)) * SCALE
        m = jnp.max(s, axis=1, keepdims=True)
        p = jnp.exp(s - m)
        p = p / jnp.sum(p, axis=1, keepdims=True)
        o_ref[:, sl] = _dot(p.astype(BF), v_ref[:, sl].astype(BF)).astype(BF)


def _mem_attend_prompt(qm, mem_k, mem_v):
    tq = 512
    nq = SEQ // tq
    return pl.pallas_call(
        _mem_attend_kernel,
        grid=(BATCH, nq),
        in_specs=[pl.BlockSpec((tq, MEM_W), lambda b, qi: (b * nq + qi, 0)),
                  pl.BlockSpec((MEM_LEN, MEM_W), lambda b, qi: (b, 0)),
                  pl.BlockSpec((MEM_LEN, MEM_W), lambda b, qi: (b, 0))],
        out_specs=pl.BlockSpec((tq, MEM_W), lambda b, qi: (b * nq + qi, 0)),
        out_shape=jax.ShapeDtypeStruct((BATCH * SEQ, MEM_W), BF),
        compiler_params=_params(2),
        name="mem_attend_prompt",
    )(qm, mem_k, mem_v)


def _merge_kernel(oa_ref, ob_ref, om_ref, ga_ref, gb_ref, gm_ref, pa_ref, pb_ref, pm_ref, o_ref):
    merged = (jax.nn.sigmoid(ga_ref[...]) * _dot(oa_ref[...], pa_ref[...])
              + jax.nn.sigmoid(gb_ref[...]) * _dot(ob_ref[...], pb_ref[...])
              + jax.nn.sigmoid(gm_ref[...]) * _dot(om_ref[...], pm_ref[...]))
    o_ref[...] = merged.astype(BF)


def _merge(oa, ob, om, gbr, pa, pb, pm, tm, tag):
    rows = oa.shape[0]
    tn = 512
    n_col = D_MODEL // tn
    row = lambda w: pl.BlockSpec((tm, w), lambda i, j: (i, 0))
    gate = lambda b: pl.BlockSpec((tm, tn), lambda i, j: (i, b * n_col + j))
    proj = lambda w: pl.BlockSpec((w, tn), lambda i, j: (0, j))
    return pl.pallas_call(
        _merge_kernel,
        grid=(rows // tm, n_col),
        in_specs=[row(MOBA_W), row(NSA_W), row(MEM_W), gate(0), gate(1), gate(2),
                  proj(MOBA_W), proj(NSA_W), proj(MEM_W)],
        out_specs=pl.BlockSpec((tm, tn), lambda i, j: (i, j)),
        out_shape=jax.ShapeDtypeStruct((rows, D_MODEL), BF),
        compiler_params=_params(2),
        name="branch_merge_" + tag,
    )(oa, ob, om, gbr, gbr, gbr, pa, pb, pm)


def _layer_norm(x, g, b):
    mu = jnp.mean(x, axis=-1, keepdims=True)
    xc = x - mu
    var = jnp.mean(xc * xc, axis=-1, keepdims=True)
    return xc * lax.rsqrt(var + LN_EPS) * g + b


def _out_ln_kernel(x_ref, m_ref, wo_ref, g_ref, b_ref, h_ref, hb_ref):
    h = _layer_norm(DN_ALPHA * x_ref[...] + _dot(m_ref[...], wo_ref[...]), g_ref[...], b_ref[...])
    h_ref[...] = h
    hb_ref[...] = h.astype(BF)


def _out_ln(x, merged, wo, g, b, tm, tag):
    rows = x.shape[0]
    row = lambda: pl.BlockSpec((tm, D_MODEL), lambda i: (i, 0))
    vec = lambda: pl.BlockSpec((1, D_MODEL), lambda i: (0, 0))
    return pl.pallas_call(
        _out_ln_kernel,
        grid=(rows // tm,),
        in_specs=[row(), row(), pl.BlockSpec((D_MODEL, D_MODEL), lambda i: (0, 0)), vec(), vec()],
        out_specs=[row(), row()],
        out_shape=[jax.ShapeDtypeStruct((rows, D_MODEL), F32), jax.ShapeDtypeStruct((rows, D_MODEL), BF)],
        compiler_params=_params(1),
        name="out_proj_ln_" + tag,
    )(x, merged, wo, g, b)


TF = 512
N_TF = FFN_DIM // TF
HALO = 8
FFN_TM = 1024
FFN_VMEM_LIMIT = 58 * 1024 * 1024


def _ffn_kernel(hb_ref, halo_ref, h_ref, wu_ref, wg_ref, cw_ref, cb_ref, wd_ref, g_ref, b_ref, p2_ref, p1_ref,
                y_ref, u_ref, *, seq_rows):
    i = pl.program_id(0)
    f = pl.program_id(1)
    tm = hb_ref.shape[0]
    u = _dot(hb_ref[...], wu_ref[...])
    gate = _dot(hb_ref[...], wg_ref[...])
    if seq_rows:
        tiles_per_seq = seq_rows // tm
        live = jnp.where(i % tiles_per_seq == 0, 0.0, 1.0)
        u_halo = _dot(halo_ref[...], wu_ref[...]) * live
        row = lax.broadcasted_iota(jnp.int32, (HALO, TF), 0)
        prev1 = pltpu.roll(u, shift=1, axis=0)
        prev2 = pltpu.roll(u, shift=2, axis=0)
        head1 = jnp.where(row < 1, pltpu.roll(u_halo, shift=1, axis=0), prev1[:HALO])
        head2 = jnp.where(row < 2, pltpu.roll(u_halo, shift=2, axis=0), prev2[:HALO])
        prev1 = jnp.concatenate([head1, prev1[HALO:]], axis=0)
        prev2 = jnp.concatenate([head2, prev2[HALO:]], axis=0)
        u_ref[0] = u[tm - HALO:]
    else:
        prev2 = p2_ref[...]
        prev1 = p1_ref[...]
        u_ref[...] = u
    uc = cb_ref[...] + cw_ref[0:1, :] * prev2 + cw_ref[1:2, :] * prev1 + cw_ref[2:3, :] * u
    act = (_gelu_tanh(uc) * gate).astype(BF)

    @pl.when(f == 0)
    def _():
        y_ref[...] = jnp.zeros(y_ref.shape, F32)

    y_ref[...] += _dot(act, wd_ref[...])

    @pl.when(f == N_TF - 1)
    def _():
        y_ref[...] = _layer_norm(DN_ALPHA * h_ref[...] + y_ref[...], g_ref[...], b_ref[...])


def _ffn(h, h_bf, wu_bf, cw, cb, wd_bf, g, b, prev2, prev1, tm, seq_rows, tag):
    rows = h.shape[0]
    n_halo = tm // HALO
    if seq_rows:
        prev2 = prev1 = jnp.zeros((HALO, TF), F32)
        cs_spec = pl.BlockSpec((HALO, TF), lambda i, f: (0, 0))
        halo_spec = pl.BlockSpec((HALO, D_MODEL), lambda i, f: (jnp.maximum(i * n_halo - 1, 0), 0))
        u_shape = jax.ShapeDtypeStruct((rows // tm, HALO, FFN_DIM), F32)
        u_spec = pl.BlockSpec((1, HALO, TF), lambda i, f: (i, 0, f))
    else:
        cs_spec = pl.BlockSpec((tm, TF), lambda i, f: (i, f))
        halo_spec = pl.BlockSpec((HALO, D_MODEL), lambda i, f: (0, 0))
        u_shape = jax.ShapeDtypeStruct((rows, FFN_DIM), F32)
        u_spec = pl.BlockSpec((tm, TF), lambda i, f: (i, f))
    row = lambda: pl.BlockSpec((tm, D_MODEL), lambda i, f: (i, 0))
    row_once = lambda: pl.BlockSpec((tm, D_MODEL), lambda i, f: (i, 0), pipeline_mode=pl.Buffered(1))
    vec = lambda: pl.BlockSpec((1, D_MODEL), lambda i, f: (0, 0))
    return pl.pallas_call(
        functools.partial(_ffn_kernel, seq_rows=seq_rows),
        grid=(rows // tm, N_TF),
        in_specs=[row_once(), halo_spec, row_once(),
                  pl.BlockSpec((D_MODEL, TF), lambda i, f: (0, f)),
                  pl.BlockSpec((D_MODEL, TF), lambda i, f: (0, N_TF + f)),
                  pl.BlockSpec((3, TF), lambda i, f: (0, f)),
                  pl.BlockSpec((1, TF), lambda i, f: (0, f)),
                  pl.BlockSpec((TF, D_MODEL), lambda i, f: (f, 0)),
                  vec(), vec(), cs_spec, cs_spec],
        out_specs=[row(), u_spec],
        out_shape=[jax.ShapeDtypeStruct((rows, D_MODEL), F32), u_shape],
        compiler_params=pltpu.CompilerParams(dimension_semantics=("arbitrary", "arbitrary"),
                                             vmem_limit_bytes=FFN_VMEM_LIMIT),
        name="conv_ffn_" + tag,
    )(h_bf, h_bf, h, wu_bf, wu_bf, cw, cb, wd_bf, g, b, prev2, prev1)


def _rowdot(mat, vec):
    return jnp.sum(mat * vec, axis=1, keepdims=True)


def _moba_sample_kernel(pt_ref, slopes_ref, q_ref, kn_ref, vn_ref, *refs):
    del pt_ref
    k_pages, v_pages, o_ref = refs[:N_PAGES], refs[N_PAGES:2 * N_PAGES], refs[2 * N_PAGES]
    nb = PAST_LEN // MOBA_BLOCK
    ppb = MOBA_BLOCK // PAGE
    pos = lax.broadcasted_iota(jnp.int32, (1, PAST_LEN), 1)
    dist = (PAST_LEN - pos).astype(F32)
    blk = pos // MOBA_BLOCK
    for h in range(MOBA_HEADS):
        sl = slice(h * HD, (h + 1) * HD)
        q = q_ref[0, :, sl]
        k_pg = [p[h] for p in k_pages]
        km = [sum(jnp.sum(k_pg[j * ppb + i], axis=0, keepdims=True) for i in range(ppb)) / MOBA_BLOCK
              for j in range(nb)]
        sc = [_rowdot(kmj, q) for kmj in km]
        allowed = jnp.zeros((1, PAST_LEN), F32)
        for j in range(nb):
            beats = [(sc[i] > sc[j]) | ((sc[i] == sc[j]) & (i < j)) for i in range(nb) if i != j]
            rank = sum(b.astype(F32) for b in beats)
            allowed = jnp.where(blk == j, jnp.where(rank < MOBA_TOPK, 1.0, 0.0), allowed)
        q8 = jnp.concatenate([q, jnp.zeros((7, HD), F32)], axis=0)
        k_bf = jnp.concatenate([kp.astype(BF) for kp in k_pg], axis=0)
        v_bf = jnp.concatenate([p[h].astype(BF) for p in v_pages], axis=0)
        s = _dot_nt(q8.astype(BF), k_bf) * SCALE - slopes_ref[h] * dist
        s = jnp.where(allowed > 0.5, s, -jnp.inf)
        s_own = _rowdot(q8, kn_ref[0, :, sl]) * SCALE
        m = jnp.maximum(jnp.max(s, axis=1, keepdims=True), s_own)
        p = jnp.exp(s - m)
        p_own = jnp.exp(s_own - m)
        den = jnp.sum(p, axis=1, keepdims=True) + p_own
        o = (_dot(p.astype(BF), v_bf) + p_own * vn_ref[0, :, sl]) / den
        o_ref[0, :, sl] = o[0:1]


def _moba_sample(page_table, slopes, q, k_new, v_new, k_pool, v_pool):
    row = lambda: pl.BlockSpec((1, 1, MOBA_W), lambda n, pt: (n, 0, 0))
    page = lambda p: pl.BlockSpec((None, MOBA_HEADS, PAGE, HD), lambda n, pt: (pt[n, p], 0, 0, 0))
    return pl.pallas_call(
        _moba_sample_kernel,
        grid_spec=pltpu.PrefetchScalarGridSpec(
            num_scalar_prefetch=1, grid=(DEC_BATCH,),
            in_specs=[pl.BlockSpec(memory_space=pltpu.SMEM), row(), row(), row()]
            + [page(p) for p in range(N_PAGES)] * 2,
            out_specs=row()),
        out_shape=jax.ShapeDtypeStruct((DEC_BATCH, 1, MOBA_W), F32),
        compiler_params=_params(1),
        name="moba_sample",
    )(page_table, slopes, q, k_new, v_new, *([k_pool] * N_PAGES), *([v_pool] * N_PAGES))


NS = 8
ROWS_PER_PAGE = PAGE * NSA_G
N_SEL = NSA_TOPN - 1
N_CAND_S = PAST_LEN // SEL_BLOCK


def _nsa_sample_cmp_kernel(pt_ref, slopes_ref, kcn_ref, vcn_ref, q_ref, pe_ref, w1_ref, w2_ref, cmat_ref, tri_ref,
                           *refs):
    del pt_ref
    k_pages, v_pages = refs[:N_PAGES], refs[N_PAGES:2 * N_PAGES]
    ocmp_ref, idx_ref, stage_ref = refs[2 * N_PAGES:2 * N_PAGES + 3]
    s_in = pl.program_id(1)
    unit = N_CHUNK
    chunks_per_page = PAGE // CMP_STRIDE

    rg = CMP_STRIDE * NSA_G
    for which, pages in enumerate((k_pages, v_pages)):
        for pair in range(N_PAGES // 2):
            slabs = [jnp.swapaxes(pages[2 * pair + i][...].reshape(chunks_per_page, rg, HD), 0, 1) for i in range(2)]
            for g in range(NSA_G):
                base = pl.multiple_of(s_in * (NSA_G * unit) + g * unit, unit) + pair * 2 * chunks_per_page
                for r in range(CMP_STRIDE):
                    x = jnp.concatenate([slabs[0][NSA_G * r + g], slabs[1][NSA_G * r + g]], axis=0)
                    stage_ref[which, pl.ds(base, 2 * chunks_per_page), r * HD:(r + 1) * HD] = x.astype(BF)

    @pl.when(s_in == NS - 1)
    def _():
        n_unit = NS * NSA_G
        rows = n_unit * unit
        row = lax.broadcasted_iota(jnp.int32, (rows, HD), 0)
        toks = []
        for which, new_ref in enumerate((kcn_ref, vcn_ref)):
            w1 = w1_ref.at[which]
            acc = _dot(stage_ref[which], w1[...])
            top, bot = acc[:, :HD], acc[:, HD:]
            nxt = pltpu.roll(bot, shift=rows - 1, axis=0)
            new_rows = jnp.concatenate([new_ref[s:s + 1, g * HD:(g + 1) * HD]
                                        for s in range(NS) for g in range(NSA_G)], axis=0)
            b_new = _dot(new_rows.astype(BF), w1[0:HD, HD:])
            b_last = jnp.concatenate([jnp.broadcast_to(b_new[u:u + 1], (unit, HD)) for u in range(n_unit)], axis=0)
            c_row = jnp.zeros((1, HD), F32)
            for r in range(CMP_LEN):
                half, rr = divmod(r, CMP_STRIDE)
                c_row = c_row + _dot(pe_ref[which, :, r * HD:(r + 1) * HD],
                                     w1[rr * HD:(rr + 1) * HD, half * HD:(half + 1) * HD])
            pre = top + jnp.where(row % unit == unit - 1, b_last, nxt) + c_row
            toks.append(_dot(_gelu_tanh(pre).astype(BF), w2_ref[which]))
        ck_all, cv_all = toks

        lane = lax.broadcasted_iota(jnp.int32, (8, HD), 1)
        row8 = lax.broadcasted_iota(jnp.int32, (8, HD), 0)
        dist = PAST_LEN - (lane * CMP_STRIDE + CMP_LEN - 1)
        p_rows = []
        for s in range(NS):
            for g in range(NSA_G):
                u = s * NSA_G + g
                ck_hi, ck_lo = _split_bf16(ck_all[u * unit:(u + 1) * unit])
                cv_bf = cv_all[u * unit:(u + 1) * unit].astype(BF)
                q8 = jnp.concatenate([q_ref[s:s + 1, (g * NSA_R + r) * HD:(g * NSA_R + r + 1) * HD]
                                      for r in range(NSA_R)] + [jnp.zeros((8 - NSA_R, HD), F32)], axis=0)
                slope = jnp.where(row8[:, 0:1] == 0, slopes_ref[g * NSA_R],
                                  jnp.where(row8[:, 0:1] == 1, slopes_ref[g * NSA_R + 1], slopes_ref[g * NSA_R + 2]))
                o8, p8 = _cmp_attend(q8, ck_hi, ck_lo, cv_bf, slope, dist)
                ocmp_ref[s, g] = o8
                p_rows.append(jnp.sum(jnp.where(row8 < NSA_R, p8, 0.0), axis=0, keepdims=True))
        p_grp = jnp.concatenate(p_rows + [jnp.zeros((HD - n_unit, HD), F32)], axis=0)
        member = _select_blocks(p_grp, cmat_ref[...], jnp.full((EXT_MASK, HD), N_CAND_S, jnp.int32), N_CAND_S, N_SEL)
        member_pad = jnp.concatenate([member, jnp.zeros((HD - EXT_MASK, HD), F32)], axis=0)
        rank = _dot(tri_ref[...], member_pad.astype(BF))[:EXT_MASK]
        cand = lax.broadcasted_iota(jnp.int32, (EXT_MASK, HD), 0).astype(F32)
        picks = [jnp.sum(jnp.where((member > 0.5) & (rank == c), cand, 0.0), axis=0, keepdims=True)
                 for c in range(N_SEL)]
        idx_ref[0] = jnp.concatenate(picks + [jnp.zeros((1, HD), F32)], axis=0).astype(jnp.int32)


def _nsa_sample_cmp(page_table, slopes, kc_new, vc_new, q, pe_bf, w1_flat, w2_bf, cmat, tri, ck_pool, cv_pool):
    page = lambda p: pl.BlockSpec((ROWS_PER_PAGE, HD), lambda i, s, pt: (pt[i * NS + s, p], 0))
    full = lambda a: pl.BlockSpec(a.shape, lambda i, s, pt: (0,) * a.ndim)
    rows = lambda w: pl.BlockSpec((NS, w), lambda i, s, pt: (i, 0))
    out = lambda: pl.BlockSpec((NS, NSA_G, 8, HD), lambda i, s, pt: (i, 0, 0, 0))
    return pl.pallas_call(
        _nsa_sample_cmp_kernel,
        grid_spec=pltpu.PrefetchScalarGridSpec(
            num_scalar_prefetch=1, grid=(DEC_BATCH // NS, NS),
            in_specs=[pl.BlockSpec(memory_space=pltpu.SMEM), rows(KV_W), rows(KV_W), rows(NSA_W),
                      full(pe_bf), full(w1_flat), full(w2_bf), full(cmat), full(tri)]
            + [page(p) for p in range(N_PAGES)] * 2,
            out_specs=[out(), pl.BlockSpec((1, N_SEL + 1, HD), lambda i, s, pt: (i, 0, 0))],
            scratch_shapes=[pltpu.VMEM((2, NS * NSA_G * N_CHUNK, CMP_STRIDE * HD), BF)]),
        out_shape=[jax.ShapeDtypeStruct((DEC_BATCH, NSA_G, 8, HD), F32),
                   jax.ShapeDtypeStruct((DEC_BATCH // NS, N_SEL + 1, HD), jnp.int32)],
        compiler_params=_params(2),
        name="nsa_sample_cmp",
    )(page_table, slopes, kc_new, vc_new, q, pe_bf, w1_flat, w2_bf, cmat, tri,
      *([ck_pool] * N_PAGES), *([cv_pool] * N_PAGES))


def _attend_rows(q8, slope, k, v, dist, k_own, v_own):
    s = _dot_nt(q8.astype(BF), k.astype(BF)) * SCALE - slope * dist
    s_own = _rowdot(q8, k_own) * SCALE
    m = jnp.maximum(jnp.max(s, axis=1, keepdims=True), s_own)
    p = jnp.exp(s - m)
    p_own = jnp.exp(s_own - m)
    den = jnp.sum(p, axis=1, keepdims=True) + p_own
    return (_dot(p.astype(BF), v.astype(BF)) + p_own * v_own) / den


BLK_ROWS = SEL_BLOCK * NSA_G


def _nsa_sample_attend_kernel(pt_ref, ix_ref, slopes_ref, q_ref, gn_ref, ksn_ref, vsn_ref, kwn_ref, vwn_ref,
                              ocmp_ref, wk_ref, wv_ref, sk_hbm, sv_hbm, o_ref, wko_ref, wvo_ref, kbuf, vbuf, sem):
    n = pl.program_id(0)
    halves = PAGE // SEL_BLOCK

    def block_copies(sample, slot):
        copies = []
        for g in range(NSA_G):
            for c in range(N_SEL):
                j = ix_ref[(sample * NSA_G + g) * N_SEL + c]
                row0 = pl.multiple_of((pt_ref[sample, j // halves] * halves + j % halves) * BLK_ROWS, BLK_ROWS)
                dst = pl.ds((g * N_SEL + c) * BLK_ROWS, BLK_ROWS)
                for pool, buf in ((sk_hbm, kbuf), (sv_hbm, vbuf)):
                    copies.append(pltpu.make_async_copy(pool.at[pl.ds(row0, BLK_ROWS), :], buf.at[slot, dst, :],
                                                        sem.at[slot]))
        return copies

    @pl.when(n == 0)
    def _():
        for cp in block_copies(0, 0):
            cp.start()

    @pl.when(n + 1 < DEC_BATCH)
    def _():
        for cp in block_copies(n + 1, (n + 1) % 2):
            cp.start()

    slot = n % 2
    for cp in block_copies(n, slot):
        cp.wait()

    wl = wk_ref.shape[0] // NSA_G
    off = lax.broadcasted_iota(jnp.int32, (1, SEL_BLOCK), 1)
    dist_w = (wl - lax.broadcasted_iota(jnp.int32, (1, wl), 1)).astype(F32)
    row8 = lax.broadcasted_iota(jnp.int32, (8, 1), 0)
    gates = jax.nn.sigmoid(gn_ref[0])
    for g in range(NSA_G):
        gsl = slice(g * HD, (g + 1) * HD)
        heads = [g * NSA_R + r for r in range(NSA_R)]
        q8 = jnp.concatenate([q_ref[0, :, hd * HD:(hd + 1) * HD] for hd in heads]
                             + [jnp.zeros((8 - NSA_R, HD), F32)], axis=0)
        slope = jnp.where(row8 == 0, slopes_ref[heads[0]],
                          jnp.where(row8 == 1, slopes_ref[heads[1]], slopes_ref[heads[2]]))
        sel_rows = pl.ds(g * N_SEL * BLK_ROWS + g, N_SEL * SEL_BLOCK, stride=NSA_G)
        pos = jnp.concatenate([ix_ref[(n * NSA_G + g) * N_SEL + c] * SEL_BLOCK + off for c in range(N_SEL)], axis=1)
        o_sel = _attend_rows(q8, slope, kbuf[slot, sel_rows, :], vbuf[slot, sel_rows, :],
                             (PAST_LEN - pos).astype(F32), ksn_ref[0, :, gsl], vsn_ref[0, :, gsl])
        o_win = _attend_rows(q8, slope, wk_ref[pl.ds(g, wl, stride=NSA_G), :], wv_ref[pl.ds(g, wl, stride=NSA_G), :],
                             dist_w, kwn_ref[0, :, gsl], vwn_ref[0, :, gsl])
        for r, hd in enumerate(heads):
            o_ref[0, :, hd * HD:(hd + 1) * HD] = (
                gates[:, 3 * hd:3 * hd + 1] * ocmp_ref[0, g, r:r + 1, :]
                + gates[:, 3 * hd + 1:3 * hd + 2] * o_sel[r:r + 1] + gates[:, 3 * hd + 2:3 * hd + 3] * o_win[r:r + 1])

    for cache, new, out in ((wk_ref, kwn_ref, wko_ref), (wv_ref, vwn_ref, wvo_ref)):
        out[...] = pltpu.roll(cache[...], shift=(wl - 1) * NSA_G, axis=0)
        for g in range(NSA_G):
            out[pl.ds((wl - 1) * NSA_G + g, 1), :] = new[0, :, g * HD:(g + 1) * HD]


def _nsa_sample_attend(page_table, sel_idx, slopes, q, gn, ks_new, vs_new, kw_new, vw_new, o_cmp,
                       win_k, win_v, sk_pool, sv_pool):
    row = lambda w: pl.BlockSpec((1, 1, w), lambda n, pt, ix: (n, 0, 0))
    win_rows = win_k.shape[0] // DEC_BATCH
    win = lambda: pl.BlockSpec((win_rows, HD), lambda n, pt, ix: (n, 0))
    any_spec = pl.BlockSpec(memory_space=pl.ANY)
    buf = pltpu.VMEM((2, NSA_G * N_SEL * BLK_ROWS, HD), F32)
    return pl.pallas_call(
        _nsa_sample_attend_kernel,
        grid_spec=pltpu.PrefetchScalarGridSpec(
            num_scalar_prefetch=2, grid=(DEC_BATCH,),
            in_specs=[pl.BlockSpec(memory_space=pltpu.SMEM), row(NSA_W), row(HD), row(KV_W), row(KV_W), row(KV_W),
                      row(KV_W), pl.BlockSpec((1, NSA_G, 8, HD), lambda n, pt, ix: (n, 0, 0, 0)), win(), win(),
                      any_spec, any_spec],
            out_specs=[row(NSA_W), win(), win()],
            scratch_shapes=[buf, buf, pltpu.SemaphoreType.DMA((2,))]),
        out_shape=[jax.ShapeDtypeStruct((DEC_BATCH, 1, NSA_W), F32), jax.ShapeDtypeStruct(win_k.shape, F32),
                   jax.ShapeDtypeStruct(win_v.shape, F32)],
        compiler_params=_params(1),
        name="nsa_sample_attend",
    )(page_table, sel_idx, slopes, q, gn, ks_new, vs_new, kw_new, vw_new, o_cmp, win_k, win_v, sk_pool, sv_pool)


def _mem_sample_kernel(q_ref, k_ref, v_ref, o_ref):
    for h in range(MEM_HEADS):
        sl = slice(h * HD, (h + 1) * HD)
        q = q_ref[0, :, sl]
        k = k_ref[pl.ds(h, MEM_LEN, stride=MEM_HEADS), :]
        v = v_ref[pl.ds(h, MEM_LEN, stride=MEM_HEADS), :]
        s = _rowdot(k, q) * SCALE
        p = jnp.exp(s - jnp.max(s, axis=0, keepdims=True))
        o_ref[0, :, sl] = jnp.sum(p * v, axis=0, keepdims=True) / jnp.sum(p, axis=0, keepdims=True)


def _mem_sample(q, mem_k, mem_v):
    rows = MEM_LEN * MEM_HEADS
    row = lambda: pl.BlockSpec((1, 1, MEM_W), lambda n: (n, 0, 0))
    kv = lambda: pl.BlockSpec((rows, HD), lambda n: (n, 0))
    return pl.pallas_call(
        _mem_sample_kernel,
        grid=(DEC_BATCH,),
        in_specs=[row(), kv(), kv()],
        out_specs=row(),
        out_shape=jax.ShapeDtypeStruct((DEC_BATCH, 1, MEM_W), F32),
        compiler_params=_params(1),
        name="mem_attend_sample",
    )(q, mem_k, mem_v)


def _project_all(x_bf, w, tm, paged, tag):
    kv_kind = "pages" if paged else "flat"
    qa, ka, va, qb = _proj(x_bf, w["wa"], w["ba"], MOBA_W, ["flat", kv_kind, kv_kind, "flat"], tm,
                           "proj_moba_q_" + tag, True)
    kc, vc, ks, vs, kw, vw = _proj(x_bf, w["wb"], w["bb"], KV_W, ["flat"] * 6, tm, "proj_nsa_kv_" + tag, True)
    (qm,) = _proj(x_bf, w["wm"], w["bm"], MEM_W, ["flat"], tm, "proj_mem_q_" + tag, True)
    (gn,) = _proj(x_bf, w["wg"], w["bg"], HD, ["flat"], tm, "proj_nsa_gate_" + tag, True)
    gbr = _matmul(x_bf, w["wd"], w["bd"], tm, 768, "proj_branch_gate_" + tag)
    return qa, ka, va, qb, kc, vc, ks, vs, kw, vw, qm, gn, gbr


def _strict_lower_ones():
    return jnp.asarray(np.tril(np.ones((HD, HD), np.float32), -1), BF)


def kernel(x_prompt, x_sample, cache_mem_k, cache_mem_v, cache_moba_k, cache_moba_v, cache_nsa_cmp_k, cache_nsa_cmp_v, cache_nsa_sel_k, cache_nsa_sel_v, cache_nsa_win_k, cache_nsa_win_v, cache_ffn_conv, page_table, mem_prompt, w_in, b_in, w_mem_kv, cmp_pe, cmp_w1, cmp_w2, p_moba, p_nsa, p_mem, w_o, ln1_g, ln1_b, w_up, conv_w, conv_b, w_down, ln2_g, ln2_b):
    slopes = _alibi_slopes(MOBA_HEADS)
    w_bf = w_in[0].T.astype(BF)
    b0 = b_in[0][None, :]
    c0, c1, c2, c3 = 4 * MOBA_W, 4 * MOBA_W + 6 * KV_W, 4 * MOBA_W + 6 * KV_W + N_GATE, 4 * MOBA_W + 6 * KV_W + N_GATE + MEM_W
    w = {"wa": w_bf[:c0], "ba": b0[:, :c0], "wb": w_bf[c0:c1], "bb": b0[:, c0:c1],
         "wg": jnp.pad(w_bf[c1:c2], ((0, HD - N_GATE), (0, 0))), "bg": jnp.pad(b0[:, c1:c2], ((0, 0), (0, HD - N_GATE))),
         "wm": w_bf[c2:c3], "bm": b0[:, c2:c3], "wd": w_bf[c3:], "bd": b0[:, c3:]}
    pe_bf, w1_bf, w2_bf = _pack_cmp_weights(cmp_pe[0], cmp_w1[0], cmp_w2[0])
    pa_bf, pb_bf, pm_bf, wo_bf = p_moba[0].astype(BF), p_nsa[0].astype(BF), p_mem[0].astype(BF), w_o[0].astype(BF)
    wu_bf, wd_bf = w_up[0].astype(BF), w_down[0].astype(BF)

    xp = x_prompt.reshape(BATCH * SEQ, D_MODEL)
    xp_bf = xp.astype(BF)
    qa, ka, va, qb, kc, vc, ks, vs, kw, vw, qm, gn, gbr = _project_all(xp_bf, w, 1024, True, "prompt")
    mem_k, mem_v = _proj(mem_prompt.reshape(BATCH * MEM_LEN, D_MODEL).astype(BF), w_mem_kv[0].astype(BF),
                         jnp.zeros((1, 2 * MEM_W), F32), MEM_W, ["flat", "flat"], 512, "proj_mem_kv")
    slope_rows = _slope_rows(MOBA_HEADS)
    o_a = _moba_prompt(slope_rows, qa, ka, va)
    ck, cv = _compress_prompt(kc, vc, pe_bf, w1_bf, w2_bf)
    cmat = _cmp_to_sel_matrix(N_CHUNK - 1, SEQ // SEL_BLOCK - 1)
    o_b = _nsa_prompt(slopes, slope_rows, qb, gn, ks, vs, kw, vw, ck, cv, cmat)
    o_m = _mem_attend_prompt(qm, mem_k, mem_v)
    merged = _merge(o_a, o_b, o_m, gbr, pa_bf, pb_bf, pm_bf, 1024, "prompt")
    h, h_bf = _out_ln(xp, merged, wo_bf, ln1_g, ln1_b, 512, "prompt")
    y_p, u_tail = _ffn(h, h_bf, wu_bf, conv_w[0], conv_b, wd_bf, ln2_g, ln2_b, None, None, FFN_TM, SEQ, "prompt")

    pages = lambda a: a.reshape(1, BATCH, SEQ // PAGE, PAGE, NSA_G, HD)
    head_pages = lambda a: a.transpose(0, 1, 3, 2, 4)[None]
    last = lambda a: a.reshape(BATCH, SEQ, NSA_G, HD)[None, :, SEQ - WINDOW:]
    tiles_per_seq = SEQ // FFN_TM
    conv_p = u_tail.reshape(BATCH, tiles_per_seq, HALO, FFN_DIM)[:, -1, HALO - 2:][None]
    prompt_out = (y_p.reshape(BATCH, SEQ, D_MODEL),
                  mem_k.reshape(1, BATCH, MEM_LEN, MEM_HEADS, HD), mem_v.reshape(1, BATCH, MEM_LEN, MEM_HEADS, HD),
                  head_pages(ka), head_pages(va), pages(kc), pages(vc), pages(ks), pages(vs), last(kw), last(vw), conv_p)

    n = DEC_BATCH
    xs = x_sample.reshape(n, D_MODEL)
    qa, ka, va, qb, kc, vc, ks, vs, kw, vw, qm, gn, gbr = _project_all(xs.astype(BF), w, n, False, "sample")
    r3 = lambda a: a.reshape(n, 1, a.shape[-1])
    moba_pool = lambda c: c[0].transpose(0, 2, 1, 3)
    rows2d = lambda c: c.reshape(-1, HD)
    o_a = _moba_sample(page_table, slopes, r3(qa), r3(ka), r3(va), moba_pool(cache_moba_k), moba_pool(cache_moba_v))
    w1_flat = w1_bf.reshape(2, CMP_STRIDE * HD, 2 * HD)
    cmat_s = _cmp_to_sel_matrix(N_CHUNK, N_CAND_S)
    o_cmp, sel = _nsa_sample_cmp(page_table, slopes, kc, vc, qb, pe_bf, w1_flat, w2_bf, cmat_s, _strict_lower_ones(),
                                 rows2d(cache_nsa_cmp_k), rows2d(cache_nsa_cmp_v))
    sel_idx = sel[:, :N_SEL, :NS * NSA_G].transpose(0, 2, 1).reshape(-1)
    o_b, win_k_new, win_v_new = _nsa_sample_attend(
        page_table, sel_idx, slopes, r3(qb), r3(gn), r3(ks), r3(vs), r3(kw), r3(vw), o_cmp,
        rows2d(cache_nsa_win_k), rows2d(cache_nsa_win_v), rows2d(cache_nsa_sel_k), rows2d(cache_nsa_sel_v))
    o_m = _mem_sample(r3(qm), rows2d(cache_mem_k), rows2d(cache_mem_v))
    flat_bf = lambda a: a.reshape(n, a.shape[-1]).astype(BF)
    merged = _merge(flat_bf(o_a), flat_bf(o_b), flat_bf(o_m), gbr, pa_bf, pb_bf, pm_bf, n, "sample")
    h, h_bf = _out_ln(xs, merged, wo_bf, ln1_g, ln1_b, n, "sample")
    conv_old = cache_ffn_conv[0]
    y_s, u_new = _ffn(h, h_bf, wu_bf, conv_w[0], conv_b, wd_bf, ln2_g, ln2_b, conv_old[:, 0], conv_old[:, 1], n, 0,
                      "sample")

    heads = lambda a, nh: a.reshape(1, n, 1, nh, HD)
    sample_out = (heads(ka, MOBA_HEADS), heads(va, MOBA_HEADS), heads(kc, NSA_G), heads(vc, NSA_G),
                  heads(ks, NSA_G), heads(vs, NSA_G), win_k_new.reshape(cache_nsa_win_k.shape),
                  win_v_new.reshape(cache_nsa_win_v.shape),
                  jnp.stack([conv_old[:, 1], u_new], axis=1)[None])
    return (prompt_out[0], y_s.reshape(n, 1, D_MODEL)) + prompt_out[1:] + sample_out
```

```python
import functools

import numpy as np
import jax
import jax.numpy as jnp
from jax import lax
from jax.experimental import pallas as pl
from jax.experimental.pallas import tpu as pltpu

D_MODEL = 2048
BATCH = 4
SEQ = 2048
DEC_BATCH = 128
PAST_LEN = 2048
PAGE = 128
N_PAGES = PAST_LEN // PAGE
HD = 128
MOBA_HEADS = 6
MOBA_BLOCK = 256
MOBA_TOPK = 3
NSA_HEADS = 6
NSA_G = 2
NSA_R = 3
CMP_LEN = 32
CMP_STRIDE = 16
SEL_BLOCK = 64
NSA_TOPN = 16
WINDOW = 512
MEM_HEADS = 4
MEM_LEN = 256
FFN_DIM = 5632
DN_ALPHA = 2.0 ** 0.25
LN_EPS = 1e-5
SCALE = HD ** -0.5
MOBA_W = MOBA_HEADS * HD
NSA_W = NSA_HEADS * HD
KV_W = NSA_G * HD
MEM_W = MEM_HEADS * HD
N_GATE = NSA_HEADS * 3

F32 = jnp.float32
BF = jnp.bfloat16
NEG_BIG = -(2.0 ** 100)
VMEM_LIMIT = 48 * 1024 * 1024
TQ = 256


def _dot(a, b):
    return jnp.dot(a, b, preferred_element_type=F32)


def _dot_nt(a, b):
    return lax.dot_general(a, b, (((1,), (1,)), ((), ())), preferred_element_type=F32)


def _split_bf16(a):
    hi = a.astype(BF)
    lo = (a - hi.astype(F32)).astype(BF)
    return hi, lo


def _alibi_slopes(n):
    return jnp.asarray(2.0 ** (-8.0 * np.arange(1, n + 1) / n), F32)


def _gelu_tanh(x):
    return 0.5 * x * (1.0 + jnp.tanh(np.sqrt(2.0 / np.pi).astype(np.float32) * (x + 0.044715 * (x * x * x))))


def _params(n_axes):
    return pltpu.CompilerParams(dimension_semantics=("arbitrary",) * n_axes, vmem_limit_bytes=VMEM_LIMIT)


def _proj_kernel(x_ref, w_ref, b_ref, *out_refs, kinds, w_rows):
    j = pl.program_id(1)
    acc = (_dot_nt if w_rows else _dot)(x_ref[...], w_ref[...]) + b_ref[...]
    for idx, (kind, o_ref) in enumerate(zip(kinds, out_refs)):
        @pl.when(j == idx)
        def _(kind=kind, o_ref=o_ref):
            if kind == "flat":
                o_ref[...] = acc
            else:
                for p in range(o_ref.shape[1]):
                    for h in range(o_ref.shape[2]):
                        o_ref[0, p, h] = acc[p * PAGE:(p + 1) * PAGE, h * HD:(h + 1) * HD]


def _proj(x_bf, w_bf, b, tn, kinds, tm, name, w_rows=False):
    rows, k = x_bf.shape
    n_out = len(kinds)
    assert w_bf.shape == ((n_out * tn, k) if w_rows else (k, n_out * tn)) and rows % tm == 0
    w_spec = pl.BlockSpec((tn, k), lambda i, j: (j, 0)) if w_rows else pl.BlockSpec((k, tn), lambda i, j: (0, j))
    tiles_per_batch = SEQ // tm if tm <= SEQ else 1
    out_shape, out_specs = [], []
    for idx, kind in enumerate(kinds):
        if kind == "flat":
            out_shape.append(jax.ShapeDtypeStruct((rows, tn), F32))
            out_specs.append(pl.BlockSpec((tm, tn), lambda i, j: (i, 0)))
        else:
            heads = tn // HD
            out_shape.append(jax.ShapeDtypeStruct((rows // SEQ, SEQ // PAGE, heads, PAGE, HD), F32))
            out_specs.append(pl.BlockSpec((1, tm // PAGE, heads, PAGE, HD),
                                          lambda i, j: (i // tiles_per_batch, i % tiles_per_batch, 0, 0, 0)))
    return pl.pallas_call(
        functools.partial(_proj_kernel, kinds=tuple(kinds), w_rows=w_rows),
        grid=(rows // tm, n_out),
        in_specs=[pl.BlockSpec((tm, k), lambda i, j: (i, 0)), w_spec,
                  pl.BlockSpec((1, tn), lambda i, j: (0, j))],
        out_specs=out_specs,
        out_shape=out_shape,
        compiler_params=_params(2),
        name=name,
    )(x_bf, w_bf, b)


def _matmul_kernel(x_ref, w_ref, b_ref, o_ref):
    o_ref[...] = _dot_nt(x_ref[...], w_ref[...]) + b_ref[...]


def _matmul(x_bf, w_bf, b, tm, tn, name):
    rows, k = x_bf.shape
    n = w_bf.shape[0]
    return pl.pallas_call(
        _matmul_kernel,
        grid=(rows // tm, n // tn),
        in_specs=[pl.BlockSpec((tm, k), lambda i, j: (i, 0)),
                  pl.BlockSpec((tn, k), lambda i, j: (j, 0)),
                  pl.BlockSpec((1, tn), lambda i, j: (0, j))],
        out_specs=pl.BlockSpec((tm, tn), lambda i, j: (i, j)),
        out_shape=jax.ShapeDtypeStruct((rows, n), F32),
        compiler_params=_params(2),
        name=name,
    )(x_bf, w_bf, b)


def _topk_member(v, cand, n_cand, k):
    member = jnp.zeros(v.shape, F32)
    for j in range(n_cand):
        vj = v[j:j + 1, :]
        beats = (v > vj) | ((v == vj) & (cand < j))
        cnt = jnp.sum(beats.astype(F32), axis=0, keepdims=True)
        member = jnp.where((cand == j) & (cnt < k), 1.0, member)
    return member


def _to_rows(member_t):
    pad = jnp.zeros((HD - member_t.shape[0], member_t.shape[1]), F32)
    return jnp.transpose(jnp.concatenate([member_t, pad], axis=0))


EXT_MASK = 32
N_PIECE = 3
POS_SPLIT = 64
EXP2_SCALE = float(SCALE * np.log2(np.e))


def _slope_rows(n_heads):
    slope = np.asarray(2.0 ** (-8.0 * np.arange(1, n_heads + 1) / n_heads), np.float32) / np.float32(SCALE)
    rows = np.zeros((8, HD), np.float32)
    rest = slope.astype(np.float32)
    for i in range(N_PIECE):
        piece = rest.astype(BF).astype(np.float32)
        rows[:n_heads, EXT_MASK + i] = piece * POS_SPLIT
        rows[:n_heads, EXT_MASK + N_PIECE + i] = piece
        rest = rest - piece
    return jnp.asarray(rows)


def _key_ext(n_keys, block):
    key = lax.broadcasted_iota(jnp.int32, (n_keys, HD), 0)
    lane = lax.broadcasted_iota(jnp.int32, (n_keys, HD), 1)
    hot = jnp.where(key // block == lane, 1.0, 0.0)
    pos = jnp.where(lane < EXT_MASK + N_PIECE, (key // POS_SPLIT).astype(F32), (key % POS_SPLIT).astype(F32))
    return jnp.where(lane < EXT_MASK, hot, jnp.where(lane < EXT_MASK + 2 * N_PIECE, pos, 0.0)).astype(BF)


def _query_ext(allowed, slope_row, lane):
    mask = 0.0 if allowed is None else jnp.where(allowed, 0.0, NEG_BIG)
    return jnp.where(lane < EXT_MASK, mask, slope_row).astype(BF)


def _flash_init(qx, kx, v_bf, keep):
    s = jnp.where(keep, _dot_nt(qx, kx), NEG_BIG)
    m = jnp.max(s, axis=1, keepdims=True)
    p = jnp.exp2((s - m) * EXP2_SCALE)
    return m, jnp.sum(p, axis=1, keepdims=True), _dot(p.astype(BF), v_bf)


def _flash_step(state, qx, kx, v_bf, keep=None):
    m_old, l_old, acc_old = state
    s = _dot_nt(qx, kx)
    if keep is not None:
        s = jnp.where(keep, s, NEG_BIG)
    m_new = jnp.maximum(m_old, jnp.max(s, axis=1, keepdims=True))
    alpha = jnp.exp2((m_old - m_new) * EXP2_SCALE)
    p = jnp.exp2((s - m_new) * EXP2_SCALE)
    return (m_new, alpha * l_old + jnp.sum(p, axis=1, keepdims=True),
            alpha * acc_old + _dot(p.astype(BF), v_bf))


MOBA_HB = 6


def _moba_prompt_kernel(srow_ref, q_ref, k_ref, v_ref, o_ref, kx_ref, vb_ref, kmh_ref, kml_ref):
    hg = pl.program_id(1)
    qi = pl.program_id(2)
    nb = SEQ // MOBA_BLOCK

    @pl.when(qi == 0)
    def _():
        ext = _key_ext(SEQ, MOBA_BLOCK)
        for hh in range(MOBA_HB):
            k = k_ref[:, hh].reshape(SEQ, HD)
            kx_ref[hh, :, :HD] = k.astype(BF)
            kx_ref[hh, :, HD:] = ext
            vb_ref[hh] = v_ref[:, hh].reshape(SEQ, HD).astype(BF)
            km = jnp.sum(k.reshape(nb, MOBA_BLOCK, HD), axis=1) / MOBA_BLOCK
            hi, lo = _split_bf16(jnp.concatenate([km, jnp.zeros((HD - nb, HD), F32)], axis=0))
            kmh_ref[hh] = hi
            kml_ref[hh] = lo

    lane = lax.broadcasted_iota(jnp.int32, (TQ, HD), 1)
    cand = lax.broadcasted_iota(jnp.int32, (nb, TQ), 0)
    rr = lax.broadcasted_iota(jnp.int32, (TQ, MOBA_BLOCK), 0)
    cc = lax.broadcasted_iota(jnp.int32, (TQ, MOBA_BLOCK), 1)
    own = pl.multiple_of(qi * MOBA_BLOCK, MOBA_BLOCK)
    qx, states = [], []
    for hh in range(MOBA_HB):
        q = q_ref[:, hh * HD:(hh + 1) * HD]
        qh, ql = _split_bf16(q)
        sc = (_dot_nt(kmh_ref[hh], qh) + _dot_nt(kml_ref[hh], qh) + _dot_nt(kmh_ref[hh], ql))[:nb]
        sc = jnp.where(cand < qi, sc, -jnp.inf)
        member = _topk_member(sc, cand, nb - 1, MOBA_TOPK)
        allowed = (_to_rows(member) > 0.5) | (lane == qi)
        ext = _query_ext(allowed, srow_ref[pl.ds(hg * MOBA_HB + hh, 1), :], lane)
        qx.append(jnp.concatenate([qh, ext], axis=1))
        states.append(_flash_init(qx[hh], kx_ref[hh, pl.ds(own, MOBA_BLOCK), :],
                                  vb_ref[hh, pl.ds(own, MOBA_BLOCK), :], cc <= rr))

    def body(c, states):
        rows = pl.ds(pl.multiple_of(c * MOBA_BLOCK, MOBA_BLOCK), MOBA_BLOCK)
        return tuple(_flash_step(states[hh], qx[hh], kx_ref[hh, rows, :], vb_ref[hh, rows, :])
                     for hh in range(MOBA_HB))

    states = lax.fori_loop(0, qi, body, tuple(states))
    for hh in range(MOBA_HB):
        m, l, acc = states[hh]
        o_ref[:, hh * HD:(hh + 1) * HD] = (acc / l).astype(BF)


def _moba_prompt(slope_rows, q, k_pages, v_pages):
    nq = SEQ // TQ
    kv_spec = pl.BlockSpec((None, SEQ // PAGE, MOBA_HB, PAGE, HD), lambda b, h, qi: (b, 0, h, 0, 0))
    q_spec = pl.BlockSpec((TQ, MOBA_HB * HD), lambda b, h, qi: (b * nq + qi, h))
    return pl.pallas_call(
        _moba_prompt_kernel,
        grid=(BATCH, MOBA_HEADS // MOBA_HB, nq),
        in_specs=[pl.BlockSpec((8, HD), lambda b, h, qi: (0, 0)), q_spec, kv_spec, kv_spec],
        out_specs=q_spec,
        out_shape=jax.ShapeDtypeStruct((BATCH * SEQ, MOBA_W), BF),
        scratch_shapes=[pltpu.VMEM((MOBA_HB, SEQ, 2 * HD), BF), pltpu.VMEM((MOBA_HB, SEQ, HD), BF),
                        pltpu.VMEM((MOBA_HB, HD, HD), BF), pltpu.VMEM((MOBA_HB, HD, HD), BF)],
        compiler_params=_params(3),
        name="moba_prompt",
    )(slope_rows, q, k_pages, v_pages)


N_CHUNK = SEQ // CMP_STRIDE


def _compress_tokens(get_x, w1_ref, w2_ref, c_row, b_last):
    acc = jnp.zeros((N_CHUNK, 2 * HD), F32)
    for r in range(CMP_STRIDE):
        acc = acc + _dot(get_x(r).astype(BF), w1_ref[r])
    top, bot = acc[:, :HD], acc[:, HD:]
    nxt = pltpu.roll(bot, shift=N_CHUNK - 1, axis=0)
    row = lax.broadcasted_iota(jnp.int32, (N_CHUNK, HD), 0)
    pre = top + jnp.where(row == N_CHUNK - 1, b_last, nxt) + c_row
    return _dot(_gelu_tanh(pre).astype(BF), w2_ref[...])


def _compress_prompt_kernel(kc_ref, vc_ref, pe_ref, w1_ref, w2_ref, ck_ref, cv_ref):
    for which, (src, dst) in enumerate(((kc_ref, ck_ref), (vc_ref, cv_ref))):
        w1 = w1_ref.at[which]
        c_row = _pe_term(pe_ref, w1, which)
        get_x = lambda r, src=src: src[pl.ds(r, N_CHUNK, stride=CMP_STRIDE), :]
        dst[0, 0] = _compress_tokens(get_x, w1, w2_ref.at[which], c_row, jnp.zeros((1, HD), F32))


def _pe_term(pe_ref, w1, which):
    c_row = jnp.zeros((1, HD), F32)
    for r in range(CMP_LEN):
        half, rr = divmod(r, CMP_STRIDE)
        c_row = c_row + _dot(pe_ref[which, :, r * HD:(r + 1) * HD], w1[rr, :, half * HD:(half + 1) * HD])
    return c_row


def _pack_cmp_weights(cmp_pe, cmp_w1, cmp_w2):
    w1 = cmp_w1.reshape(2, 2, CMP_STRIDE, HD, HD).transpose(0, 2, 3, 1, 4).reshape(2, CMP_STRIDE, HD, 2 * HD)
    return cmp_pe.reshape(2, 1, CMP_LEN * HD).astype(BF), w1.astype(BF), cmp_w2.astype(BF)


def _compress_prompt(kc, vc, pe_bf, w1_bf, w2_bf):
    full = lambda a: pl.BlockSpec(a.shape, lambda b, g: (0,) * a.ndim)
    return pl.pallas_call(
        _compress_prompt_kernel,
        grid=(BATCH, NSA_G),
        in_specs=[pl.BlockSpec((SEQ, HD), lambda b, g: (b, g)), pl.BlockSpec((SEQ, HD), lambda b, g: (b, g)),
                  full(pe_bf), full(w1_bf), full(w2_bf)],
        out_specs=[pl.BlockSpec((1, 1, N_CHUNK, HD), lambda b, g: (b, g, 0, 0))] * 2,
        out_shape=[jax.ShapeDtypeStruct((BATCH, NSA_G, N_CHUNK, HD), F32)] * 2,
        compiler_params=_params(2),
        name="nsa_compress_prompt",
    )(kc, vc, pe_bf, w1_bf, w2_bf)


def _cmp_to_sel_matrix(n_cmp, n_blocks):
    start = np.arange(n_cmp)[:, None] * CMP_STRIDE
    j0 = np.arange(n_blocks)[None, :] * SEL_BLOCK
    ov = np.minimum(start + CMP_LEN, j0 + SEL_BLOCK) - np.maximum(start, j0)
    m = np.zeros((HD, HD), np.float32)
    m[:n_blocks, :n_cmp] = (np.clip(ov, 0, None) / CMP_LEN).T
    return jnp.asarray(m, BF)


def _cmp_attend(q, ck_hi, ck_lo, cv_bf, slope, dist):
    qh, ql = _split_bf16(q)
    s = (_dot_nt(qh, ck_hi) + _dot_nt(qh, ck_lo) + _dot_nt(ql, ck_hi)) * SCALE - slope * dist.astype(F32)
    s = jnp.where(dist >= 0, s, -jnp.inf)
    m = jnp.max(s, axis=1, keepdims=True)
    m = jnp.where(m > -jnp.inf, m, 0.0)
    p = jnp.exp(s - m)
    p = p / jnp.maximum(jnp.sum(p, axis=1, keepdims=True), 1e-30)
    return _dot(p.astype(BF), cv_bf), p


def _select_blocks(p_grp, cmat_bf, cur, n_cand, k):
    ph, plo = _split_bf16(p_grp)
    imp = (_dot_nt(cmat_bf, ph) + _dot_nt(cmat_bf, plo))[:EXT_MASK]
    cand = lax.broadcasted_iota(jnp.int32, imp.shape, 0)
    valid = (cand < cur) & (cand < n_cand)
    forced = valid & ((cand == 0) | (cand == cur - 1))
    v = jnp.where(forced, jnp.inf, jnp.where(valid, imp, -jnp.inf))
    return jnp.where(valid, _topk_member(v, cand, n_cand, k), 0.0)


def _nsa_prompt_kernel(slopes_ref, srow_ref, q_ref, gn_ref, ks_ref, vs_ref, kw_ref, vw_ref, ck_ref, cv_ref, cmat_ref,
                       o_ref, kx_ref, kwx_ref, vsb_ref, vwb_ref):
    qi = pl.program_id(1)
    chunk = TQ
    rows = NSA_R * TQ
    groups = range(NSA_G)

    @pl.when(qi == 0)
    def _():
        ext = _key_ext(SEQ, SEL_BLOCK)
        for g in groups:
            gsl = slice(g * HD, (g + 1) * HD)
            kx_ref[g, :, :HD] = ks_ref[:, gsl].astype(BF)
            kx_ref[g, :, HD:] = ext
            kwx_ref[g, :, :HD] = kw_ref[:, gsl].astype(BF)
            kwx_ref[g, :, HD:] = ext
            vsb_ref[g] = vs_ref[:, gsl].astype(BF)
            vwb_ref[g] = vw_ref[:, gsl].astype(BF)

    lane = lax.broadcasted_iota(jnp.int32, (TQ, HD), 1)
    t = qi * TQ + lax.broadcasted_iota(jnp.int32, (TQ, HD), 0)
    stack = lambda parts: jnp.concatenate(parts, axis=0)
    dist_c = stack([t - (lane * CMP_STRIDE + CMP_LEN - 1)] * NSA_R)
    cur_t = (qi * TQ + lax.broadcasted_iota(jnp.int32, (EXT_MASK, TQ), 1)) // SEL_BLOCK

    qx, qw, o_cmp = [], [], []
    for g in groups:
        q_all = stack([q_ref[:, (g * NSA_R + r) * HD:(g * NSA_R + r + 1) * HD] for r in range(NSA_R)])
        ck_hi, ck_lo = _split_bf16(ck_ref[0, g])
        slope_col = stack([jnp.full((TQ, 1), slopes_ref[g * NSA_R + r], F32) for r in range(NSA_R)])
        o_c, p = _cmp_attend(q_all, ck_hi, ck_lo, cv_ref[0, g].astype(BF), slope_col, dist_c)
        o_cmp.append(o_c)
        p_grp = p[0:TQ] + p[TQ:2 * TQ] + p[2 * TQ:3 * TQ]
        member = _select_blocks(p_grp, cmat_ref[...], cur_t, SEQ // SEL_BLOCK - 1, NSA_TOPN - 1)
        allowed = (_to_rows(member) > 0.5) | (lane == t // SEL_BLOCK)
        srow = [srow_ref[g * NSA_R + r:g * NSA_R + r + 1, :] for r in range(NSA_R)]
        q_bf = q_all.astype(BF)
        qx.append(jnp.concatenate([q_bf, stack([_query_ext(allowed, srow[r], lane) for r in range(NSA_R)])], axis=1))
        qw.append(jnp.concatenate([q_bf, stack([_query_ext(None, srow[r], lane) for r in range(NSA_R)])], axis=1))

    rr = lax.broadcasted_iota(jnp.int32, (rows, chunk), 0) % TQ
    cc = lax.broadcasted_iota(jnp.int32, (rows, chunk), 1)
    own = pl.ds(pl.multiple_of(qi * chunk, chunk), chunk)
    past = lambda c: pl.ds(pl.multiple_of(c * chunk, chunk), chunk)

    st = tuple(_flash_init(qx[g], kx_ref[g, own, :], vsb_ref[g, own, :], cc <= rr) for g in groups)
    st = lax.fori_loop(0, qi, lambda c, st: tuple(
        _flash_step(st[g], qx[g], kx_ref[g, past(c), :], vsb_ref[g, past(c), :]) for g in groups), st)
    o_sel = [st[g][2] / st[g][1] for g in groups]

    st = tuple(_flash_init(qw[g], kwx_ref[g, own, :], vwb_ref[g, own, :], cc <= rr) for g in groups)

    def win_step(c, st):
        slack = jnp.where(c == qi - WINDOW // chunk, 0, chunk)
        return tuple(_flash_step(st[g], qw[g], kwx_ref[g, past(c), :], vwb_ref[g, past(c), :], rr - cc <= slack)
                     for g in groups)

    st = lax.fori_loop(jnp.maximum(qi - WINDOW // chunk, 0), qi, win_step, st)
    o_win = [st[g][2] / st[g][1] for g in groups]

    gates = jax.nn.sigmoid(gn_ref[:, 0:NSA_HEADS * 3])
    for g in groups:
        for r in range(NSA_R):
            sl = slice(r * TQ, (r + 1) * TQ)
            hd = g * NSA_R + r
            o = (gates[:, 3 * hd:3 * hd + 1] * o_cmp[g][sl] + gates[:, 3 * hd + 1:3 * hd + 2] * o_sel[g][sl]
                 + gates[:, 3 * hd + 2:3 * hd + 3] * o_win[g][sl])
            o_ref[:, hd * HD:(hd + 1) * HD] = o.astype(BF)


def _nsa_prompt(slopes, slope_rows, qb, gn, ks, vs, kw, vw, ck, cv, cmat):
    nq = SEQ // TQ
    kv_spec = lambda: pl.BlockSpec((SEQ, KV_W), lambda b, qi: (b, 0))
    c_spec = lambda: pl.BlockSpec((1, NSA_G, N_CHUNK, HD), lambda b, qi: (b, 0, 0, 0))
    return pl.pallas_call(
        _nsa_prompt_kernel,
        grid=(BATCH, nq),
        in_specs=[pl.BlockSpec(memory_space=pltpu.SMEM),
                  pl.BlockSpec((8, HD), lambda b, qi: (0, 0)),
                  pl.BlockSpec((TQ, NSA_W), lambda b, qi: (b * nq + qi, 0)),
                  pl.BlockSpec((TQ, HD), lambda b, qi: (b * nq + qi, 0)),
                  kv_spec(), kv_spec(), kv_spec(), kv_spec(), c_spec(), c_spec(),
                  pl.BlockSpec((HD, HD), lambda b, qi: (0, 0))],
        out_specs=pl.BlockSpec((TQ, NSA_W), lambda b, qi: (b * nq + qi, 0)),
        out_shape=jax.ShapeDtypeStruct((BATCH * SEQ, NSA_W), BF),
        scratch_shapes=[pltpu.VMEM((NSA_G, SEQ, 2 * HD), BF), pltpu.VMEM((NSA_G, SEQ, 2 * HD), BF),
                        pltpu.VMEM((NSA_G, SEQ, HD), BF), pltpu.VMEM((NSA_G, SEQ, HD), BF)],
        compiler_params=_params(2),
        name="nsa_prompt",
    )(slopes, slope_rows, qb, gn, ks, vs, kw, vw, ck, cv, cmat)


def _mem_attend_kernel(q_ref, k_ref, v_ref, o_ref):
    for h in range(MEM_HEADS):
        sl = slice(h * HD, (h + 1) * HD)
        s = _dot_nt(q_ref[:, sl].astype(BF), k_ref[:, sl].astype(BF)) * SCALE
        m = jnp.max(s, axis=1, keepdims=True)
        p = jnp.exp(s - m)
        p = p / jnp.sum(p, axis=1, keepdims=True)
        o_ref[:, sl] = _dot(p.astype(BF), v_ref[:, sl].astype(BF)).astype(BF)


def _mem_attend_prompt(qm, mem_k, mem_v):
    tq = 512
    nq = SEQ // tq
    return pl.pallas_call(
        _mem_attend_kernel,
        grid=(BATCH, nq),
        in_specs=[pl.BlockSpec((tq, MEM_W), lambda b, qi: (b * nq + qi, 0)),
                  pl.BlockSpec((MEM_LEN, MEM_W), lambda b, qi: (b, 0)),
                  pl.BlockSpec((MEM_LEN, MEM_W), lambda b, qi: (b, 0))],
        out_specs=pl.BlockSpec((tq, MEM_W), lambda b, qi: (b * nq + qi, 0)),
        out_shape=jax.ShapeDtypeStruct((BATCH * SEQ, MEM_W), BF),
        compiler_params=_params(2),
        name="mem_attend_prompt",
    )(qm, mem_k, mem_v)


def _merge_kernel(oa_ref, ob_ref, om_ref, ga_ref, gb_ref, gm_ref, pa_ref, pb_ref, pm_ref, o_ref):
    merged = (jax.nn.sigmoid(ga_ref[...]) * _dot(oa_ref[...], pa_ref[...])
              + jax.nn.sigmoid(gb_ref[...]) * _dot(ob_ref[...], pb_ref[...])
              + jax.nn.sigmoid(gm_ref[...]) * _dot(om_ref[...], pm_ref[...]))
    o_ref[...] = merged.astype(BF)


def _merge(oa, ob, om, gbr, pa, pb, pm, tm, tag):
    rows = oa.shape[0]
    tn = 512
    n_col = D_MODEL // tn
    row = lambda w: pl.BlockSpec((tm, w), lambda i, j: (i, 0))
    gate = lambda b: pl.BlockSpec((tm, tn), lambda i, j: (i, b * n_col + j))
    proj = lambda w: pl.BlockSpec((w, tn), lambda i, j: (0, j))
    return pl.pallas_call(
        _merge_kernel,
        grid=(rows // tm, n_col),
        in_specs=[row(MOBA_W), row(NSA_W), row(MEM_W), gate(0), gate(1), gate(2),
                  proj(MOBA_W), proj(NSA_W), proj(MEM_W)],
        out_specs=pl.BlockSpec((tm, tn), lambda i, j: (i, j)),
        out_shape=jax.ShapeDtypeStruct((rows, D_MODEL), BF),
        compiler_params=_params(2),
        name="branch_merge_" + tag,
    )(oa, ob, om, gbr, gbr, gbr, pa, pb, pm)


def _layer_norm(x, g, b):
    mu = jnp.mean(x, axis=-1, keepdims=True)
    xc = x - mu
    var = jnp.mean(xc * xc, axis=-1, keepdims=True)
    return xc * lax.rsqrt(var + LN_EPS) * g + b


def _out_ln_kernel(x_ref, m_ref, wo_ref, g_ref, b_ref, h_ref, hb_ref):
    h = _layer_norm(DN_ALPHA * x_ref[...] + _dot(m_ref[...], wo_ref[...]), g_ref[...], b_ref[...])
    h_ref[...] = h
    hb_ref[...] = h.astype(BF)


def _out_ln(x, merged, wo, g, b, tm, tag):
    rows = x.shape[0]
    row = lambda: pl.BlockSpec((tm, D_MODEL), lambda i: (i, 0))
    vec = lambda: pl.BlockSpec((1, D_MODEL), lambda i: (0, 0))
    return pl.pallas_call(
        _out_ln_kernel,
        grid=(rows // tm,),
        in_specs=[row(), row(), pl.BlockSpec((D_MODEL, D_MODEL), lambda i: (0, 0)), vec(), vec()],
        out_specs=[row(), row()],
        out_shape=[jax.ShapeDtypeStruct((rows, D_MODEL), F32), jax.ShapeDtypeStruct((rows, D_MODEL), BF)],
        compiler_params=_params(1),
        name="out_proj_ln_" + tag,
    )(x, merged, wo, g, b)


TF = 512
N_TF = FFN_DIM // TF
HALO = 8
FFN_TM = 1024
FFN_VMEM_LIMIT = 58 * 1024 * 1024


def _ffn_kernel(hb_ref, halo_ref, h_ref, wu_ref, wg_ref, cw_ref, cb_ref, wd_ref, g_ref, b_ref, p2_ref, p1_ref,
                y_ref, u_ref, *, seq_rows):
    i = pl.program_id(0)
    f = pl.program_id(1)
    tm = hb_ref.shape[0]
    u = _dot(hb_ref[...], wu_ref[...])
    gate = _dot(hb_ref[...], wg_ref[...])
    if seq_rows:
        tiles_per_seq = seq_rows // tm
        live = jnp.where(i % tiles_per_seq == 0, 0.0, 1.0)
        u_halo = _dot(halo_ref[...], wu_ref[...]) * live
        row = lax.broadcasted_iota(jnp.int32, (HALO, TF), 0)
        prev1 = pltpu.roll(u, shift=1, axis=0)
        prev2 = pltpu.roll(u, shift=2, axis=0)
        head1 = jnp.where(row < 1, pltpu.roll(u_halo, shift=1, axis=0), prev1[:HALO])
        head2 = jnp.where(row < 2, pltpu.roll(u_halo, shift=2, axis=0), prev2[:HALO])
        prev1 = jnp.concatenate([head1, prev1[HALO:]], axis=0)
        prev2 = jnp.concatenate([head2, prev2[HALO:]], axis=0)
        u_ref[0] = u[tm - HALO:]
    else:
        prev2 = p2_ref[...]
        prev1 = p1_ref[...]
        u_ref[...] = u
    uc = cb_ref[...] + cw_ref[0:1, :] * prev2 + cw_ref[1:2, :] * prev1 + cw_ref[2:3, :] * u
    act = (_gelu_tanh(uc) * gate).astype(BF)

    @pl.when(f == 0)
    def _():
        y_ref[...] = jnp.zeros(y_ref.shape, F32)

    y_ref[...] += _dot(act, wd_ref[...])

    @pl.when(f == N_TF - 1)
    def _():
        y_ref[...] = _layer_norm(DN_ALPHA * h_ref[...] + y_ref[...], g_ref[...], b_ref[...])


def _ffn(h, h_bf, wu_bf, cw, cb, wd_bf, g, b, prev2, prev1, tm, seq_rows, tag):
    rows = h.shape[0]
    n_halo = tm // HALO
    if seq_rows:
        prev2 = prev1 = jnp.zeros((HALO, TF), F32)
        cs_spec = pl.BlockSpec((HALO, TF), lambda i, f: (0, 0))
        halo_spec = pl.BlockSpec((HALO, D_MODEL), lambda i, f: (jnp.maximum(i * n_halo - 1, 0), 0))
        u_shape = jax.ShapeDtypeStruct((rows // tm, HALO, FFN_DIM), F32)
        u_spec = pl.BlockSpec((1, HALO, TF), lambda i, f: (i, 0, f))
    else:
        cs_spec = pl.BlockSpec((tm, TF), lambda i, f: (i, f))
        halo_spec = pl.BlockSpec((HALO, D_MODEL), lambda i, f: (0, 0))
        u_shape = jax.ShapeDtypeStruct((rows, FFN_DIM), F32)
        u_spec = pl.BlockSpec((tm, TF), lambda i, f: (i, f))
    row = lambda: pl.BlockSpec((tm, D_MODEL), lambda i, f: (i, 0))
    row_once = lambda: pl.BlockSpec((tm, D_MODEL), lambda i, f: (i, 0), pipeline_mode=pl.Buffered(1))
    vec = lambda: pl.BlockSpec((1, D_MODEL), lambda i, f: (0, 0))
    return pl.pallas_call(
        functools.partial(_ffn_kernel, seq_rows=seq_rows),
        grid=(rows // tm, N_TF),
        in_specs=[row_once(), halo_spec, row_once(),
                  pl.BlockSpec((D_MODEL, TF), lambda i, f: (0, f)),
                  pl.BlockSpec((D_MODEL, TF), lambda i, f: (0, N_TF + f)),
                  pl.BlockSpec((3, TF), lambda i, f: (0, f)),
                  pl.BlockSpec((1, TF), lambda i, f: (0, f)),
                  pl.BlockSpec((TF, D_MODEL), lambda i, f: (f, 0)),
                  vec(), vec(), cs_spec, cs_spec],
        out_specs=[row(), u_spec],
        out_shape=[jax.ShapeDtypeStruct((rows, D_MODEL), F32), u_shape],
        compiler_params=pltpu.CompilerParams(dimension_semantics=("arbitrary", "arbitrary"),
                                             vmem_limit_bytes=FFN_VMEM_LIMIT),
        name="conv_ffn_" + tag,
    )(h_bf, h_bf, h, wu_bf, wu_bf, cw, cb, wd_bf, g, b, prev2, prev1)


def _rowdot(mat, vec):
    return jnp.sum(mat * vec, axis=1, keepdims=True)


MOBA_NB = PAST_LEN // MOBA_BLOCK
MOBA_PPB = MOBA_BLOCK // PAGE


def _moba_sample_probs_kernel(pt_ref, slopes_ref, q_ref, kn_ref, *refs):
    del pt_ref
    k_pages = refs[:N_PAGES]
    p_ref, stat_ref, ids_ref = refs[N_PAGES:N_PAGES + 3]
    nb, ppb = MOBA_NB, MOBA_PPB
    pos = lax.broadcasted_iota(jnp.int32, (1, PAST_LEN), 1)
    dist = (PAST_LEN - pos).astype(F32)
    blk = pos // MOBA_BLOCK
    row8 = lax.broadcasted_iota(jnp.int32, (8, HD), 0)
    lane8 = lax.broadcasted_iota(jnp.int32, (8, HD), 1)
    stats = jnp.zeros((8, HD), F32)
    ids = jnp.zeros((8, HD), F32)
    for h in range(MOBA_HEADS):
        sl = slice(h * HD, (h + 1) * HD)
        q = q_ref[0, :, sl]
        k_pg = [p[h] for p in k_pages]
        km = [sum(jnp.sum(k_pg[j * ppb + i], axis=0, keepdims=True) for i in range(ppb)) / MOBA_BLOCK
              for j in range(nb)]
        sc = [_rowdot(kmj, q) for kmj in km]
        allowed = jnp.zeros((1, PAST_LEN), F32)
        for j in range(nb):
            beats = [(sc[i] > sc[j]) | ((sc[i] == sc[j]) & (i < j)) for i in range(nb) if i != j]
            rank = sum(b.astype(F32) for b in beats)
            allowed = jnp.where(blk == j, jnp.where(rank < MOBA_TOPK, 1.0, 0.0), allowed)
            for c in range(MOBA_TOPK):
                ids = jnp.where((row8 == h) & (lane8 == c) & (rank == c), float(j), ids)
        q8 = jnp.concatenate([q, jnp.zeros((7, HD), F32)], axis=0)
        k_bf = jnp.concatenate([kp.astype(BF) for kp in k_pg], axis=0)
        s = _dot_nt(q8.astype(BF), k_bf) * SCALE - slopes_ref[h] * dist
        s = jnp.where(allowed > 0.5, s, -jnp.inf)
        s_own = _rowdot(q8, kn_ref[0, :, sl]) * SCALE
        m = jnp.maximum(jnp.max(s, axis=1, keepdims=True), s_own)
        p = jnp.exp(s - m)
        p_own = jnp.exp(s_own - m)
        den = jnp.sum(p, axis=1, keepdims=True) + p_own
        for j in range(nb):
            p_ref[0, h * nb + j:h * nb + j + 1, :] = p[0:1, j * MOBA_BLOCK:(j + 1) * MOBA_BLOCK]
        stats = jnp.where((row8 == h) & (lane8 == 0), p_own[0:1], jnp.where((row8 == h) & (lane8 == 1), den[0:1], stats))
    stat_ref[0] = stats
    ids_ref[0] = ids.astype(jnp.int32)


def _moba_sample_probs(page_table, slopes, q, k_new, k_pool):
    row = lambda: pl.BlockSpec((1, 1, MOBA_W), lambda n, pt: (n, 0, 0))
    page = lambda p: pl.BlockSpec((None, MOBA_HEADS, PAGE, HD), lambda n, pt: (pt[n, p], 0, 0, 0))
    tile = lambda r, w: pl.BlockSpec((1, r, w), lambda n, pt: (n, 0, 0))
    n_rows = MOBA_HEADS * MOBA_NB
    return pl.pallas_call(
        _moba_sample_probs_kernel,
        grid_spec=pltpu.PrefetchScalarGridSpec(
            num_scalar_prefetch=1, grid=(DEC_BATCH,),
            in_specs=[pl.BlockSpec(memory_space=pltpu.SMEM), row(), row()] + [page(p) for p in range(N_PAGES)],
            out_specs=[tile(n_rows, MOBA_BLOCK), tile(8, HD), tile(8, HD)]),
        out_shape=[jax.ShapeDtypeStruct((DEC_BATCH, n_rows, MOBA_BLOCK), F32),
                   jax.ShapeDtypeStruct((DEC_BATCH, 8, HD), F32), jax.ShapeDtypeStruct((DEC_BATCH, 8, HD), jnp.int32)],
        compiler_params=_params(1),
        name="moba_sample_probs",
    )(page_table, slopes, q, k_new, *([k_pool] * N_PAGES))


def _moba_sample_values_kernel(pt_ref, ids_ref, p_ref, stat_ref, vn_ref, v_hbm, o_ref, vbuf, sem):
    n = pl.program_id(0)

    def block_copies(sample, slot):
        copies = []
        for h in range(MOBA_HEADS):
            for c in range(MOBA_TOPK):
                j = ids_ref[(sample * MOBA_HEADS + h) * MOBA_TOPK + c]
                for i in range(MOBA_PPB):
                    dst = pl.ds(((h * MOBA_TOPK + c) * MOBA_PPB + i) * PAGE, PAGE)
                    copies.append(pltpu.make_async_copy(v_hbm.at[pt_ref[sample, j * MOBA_PPB + i], h],
                                                        vbuf.at[slot, dst, :], sem.at[slot]))
        return copies

    @pl.when(n == 0)
    def _():
        for cp in block_copies(0, 0):
            cp.start()

    @pl.when(n + 1 < DEC_BATCH)
    def _():
        for cp in block_copies(n + 1, (n + 1) % 2):
            cp.start()

    slot = n % 2
    for cp in block_copies(n, slot):
        cp.wait()

    for h in range(MOBA_HEADS):
        acc = jnp.zeros((8, HD), F32)
        for c in range(MOBA_TOPK):
            j = ids_ref[(n * MOBA_HEADS + h) * MOBA_TOPK + c]
            p_c = p_ref[0, pl.ds(h * MOBA_NB + j, 1), :]
            p8 = jnp.concatenate([p_c, jnp.zeros((7, MOBA_BLOCK), F32)], axis=0)
            v_c = vbuf[slot, (h * MOBA_TOPK + c) * MOBA_BLOCK:(h * MOBA_TOPK + c + 1) * MOBA_BLOCK, :]
            acc = acc + _dot(p8.astype(BF), v_c.astype(BF))
        p_own = stat_ref[0, h:h + 1, 0:1]
        den = stat_ref[0, h:h + 1, 1:2]
        o_ref[0, :, h * HD:(h + 1) * HD] = (acc[0:1] + p_own * vn_ref[0, :, h * HD:(h + 1) * HD]) / den


def _moba_sample_values(page_table, sel_ids, p, stats, v_new, v_pool):
    tile = lambda r, w: pl.BlockSpec((1, r, w), lambda n, pt, ids: (n, 0, 0))
    return pl.pallas_call(
        _moba_sample_values_kernel,
        grid_spec=pltpu.PrefetchScalarGridSpec(
            num_scalar_prefetch=2, grid=(DEC_BATCH,),
            in_specs=[tile(MOBA_HEADS * MOBA_NB, MOBA_BLOCK), tile(8, HD), tile(1, MOBA_W),
                      pl.BlockSpec(memory_space=pl.ANY)],
            out_specs=tile(1, MOBA_W),
            scratch_shapes=[pltpu.VMEM((2, MOBA_HEADS * MOBA_TOPK * MOBA_BLOCK, HD), F32),
                            pltpu.SemaphoreType.DMA((2,))]),
        out_shape=jax.ShapeDtypeStruct((DEC_BATCH, 1, MOBA_W), F32),
        compiler_params=_params(1),
        name="moba_sample_values",
    )(page_table, sel_ids, p, stats, v_new, v_pool)


NS = 8
ROWS_PER_PAGE = PAGE * NSA_G
N_SEL = NSA_TOPN - 1
N_CAND_S = PAST_LEN // SEL_BLOCK


def _nsa_sample_cmp_kernel(pt_ref, slopes_ref, kcn_ref, vcn_ref, q_ref, pe_ref, w1_ref, w2_ref, cmat_ref, tri_ref,
                           *refs):
    del pt_ref
    k_pages, v_pages = refs[:N_PAGES], refs[N_PAGES:2 * N_PAGES]
    ocmp_ref, idx_ref, stage_ref = refs[2 * N_PAGES:2 * N_PAGES + 3]
    s_in = pl.program_id(1)
    unit = N_CHUNK
    chunks_per_page = PAGE // CMP_STRIDE

    rg = CMP_STRIDE * NSA_G
    for which, pages in enumerate((k_pages, v_pages)):
        for pair in range(N_PAGES // 2):
            slabs = [jnp.swapaxes(pages[2 * pair + i][...].reshape(chunks_per_page, rg, HD), 0, 1) for i in range(2)]
            for g in range(NSA_G):
                base = pl.multiple_of(s_in * (NSA_G * unit) + g * unit, unit) + pair * 2 * chunks_per_page
                for r in range(CMP_STRIDE):
                    x = jnp.concatenate([slabs[0][NSA_G * r + g], slabs[1][NSA_G * r + g]], axis=0)
                    stage_ref[which, pl.ds(base, 2 * chunks_per_page), r * HD:(r + 1) * HD] = x.astype(BF)

    @pl.when(s_in == NS - 1)
    def _():
        n_unit = NS * NSA_G
        rows = n_unit * unit
        row = lax.broadcasted_iota(jnp.int32, (rows, HD), 0)
        toks = []
        for which, new_ref in enumerate((kcn_ref, vcn_ref)):
            w1 = w1_ref.at[which]
            acc = _dot(stage_ref[which], w1[...])
            top, bot = acc[:, :HD], acc[:, HD:]
            nxt = pltpu.roll(bot, shift=rows - 1, axis=0)
            new_rows = jnp.concatenate([new_ref[s:s + 1, g * HD:(g + 1) * HD]
                                        for s in range(NS) for g in range(NSA_G)], axis=0)
            b_new = _dot(new_rows.astype(BF), w1[0:HD, HD:])
            b_last = jnp.concatenate([jnp.broadcast_to(b_new[u:u + 1], (unit, HD)) for u in range(n_unit)], axis=0)
            c_row = jnp.zeros((1, HD), F32)
            for r in range(CMP_LEN):
                half, rr = divmod(r, CMP_STRIDE)
                c_row = c_row + _dot(pe_ref[which, :, r * HD:(r + 1) * HD],
                                     w1[rr * HD:(rr + 1) * HD, half * HD:(half + 1) * HD])
            pre = top + jnp.where(row % unit == unit - 1, b_last, nxt) + c_row
            toks.append(_dot(_gelu_tanh(pre).astype(BF), w2_ref[which]))
        ck_all, cv_all = toks

        lane = lax.broadcasted_iota(jnp.int32, (8, HD), 1)
        row8 = lax.broadcasted_iota(jnp.int32, (8, HD), 0)
        dist = PAST_LEN - (lane * CMP_STRIDE + CMP_LEN - 1)
        p_rows = []
        for s in range(NS):
            for g in range(NSA_G):
                u = s * NSA_G + g
                ck_hi, ck_lo = _split_bf16(ck_all[u * unit:(u + 1) * unit])
                cv_bf = cv_all[u * unit:(u + 1) * unit].astype(BF)
                q8 = jnp.concatenate([q_ref[s:s + 1, (g * NSA_R + r) * HD:(g * NSA_R + r + 1) * HD]
                                      for r in range(NSA_R)] + [jnp.zeros((8 - NSA_R, HD), F32)], axis=0)
                slope = jnp.where(row8[:, 0:1] == 0, slopes_ref[g * NSA_R],
                                  jnp.where(row8[:, 0:1] == 1, slopes_ref[g * NSA_R + 1], slopes_ref[g * NSA_R + 2]))
                o8, p8 = _cmp_attend(q8, ck_hi, ck_lo, cv_bf, slope, dist)
                ocmp_ref[s, g] = o8
                p_rows.append(jnp.sum(jnp.where(row8 < NSA_R, p8, 0.0), axis=0, keepdims=True))
        p_grp = jnp.concatenate(p_rows + [jnp.zeros((HD - n_unit, HD), F32)], axis=0)
        member = _select_blocks(p_grp, cmat_ref[...], jnp.full((EXT_MASK, HD), N_CAND_S, jnp.int32), N_CAND_S, N_SEL)
        member_pad = jnp.concatenate([member, jnp.zeros((HD - EXT_MASK, HD), F32)], axis=0)
        rank = _dot(tri_ref[...], member_pad.astype(BF))[:EXT_MASK]
        cand = lax.broadcasted_iota(jnp.int32, (EXT_MASK, HD), 0).astype(F32)
        picks = [jnp.sum(jnp.where((member > 0.5) & (rank == c), cand, 0.0), axis=0, keepdims=True)
                 for c in range(N_SEL)]
        idx_ref[0] = jnp.concatenate(picks + [jnp.zeros((1, HD), F32)], axis=0).astype(jnp.int32)


def _nsa_sample_cmp(page_table, slopes, kc_new, vc_new, q, pe_bf, w1_flat, w2_bf, cmat, tri, ck_pool, cv_pool):
    page = lambda p: pl.BlockSpec((ROWS_PER_PAGE, HD), lambda i, s, pt: (pt[i * NS + s, p], 0))
    full = lambda a: pl.BlockSpec(a.shape, lambda i, s, pt: (0,) * a.ndim)
    rows = lambda w: pl.BlockSpec((NS, w), lambda i, s, pt: (i, 0))
    out = lambda: pl.BlockSpec((NS, NSA_G, 8, HD), lambda i, s, pt: (i, 0, 0, 0))
    return pl.pallas_call(
        _nsa_sample_cmp_kernel,
        grid_spec=pltpu.PrefetchScalarGridSpec(
            num_scalar_prefetch=1, grid=(DEC_BATCH // NS, NS),
            in_specs=[pl.BlockSpec(memory_space=pltpu.SMEM), rows(KV_W), rows(KV_W), rows(NSA_W),
                      full(pe_bf), full(w1_flat), full(w2_bf), full(cmat), full(tri)]
            + [page(p) for p in range(N_PAGES)] * 2,
            out_specs=[out(), pl.BlockSpec((1, N_SEL + 1, HD), lambda i, s, pt: (i, 0, 0))],
            scratch_shapes=[pltpu.VMEM((2, NS * NSA_G * N_CHUNK, CMP_STRIDE * HD), BF)]),
        out_shape=[jax.ShapeDtypeStruct((DEC_BATCH, NSA_G, 8, HD), F32),
                   jax.ShapeDtypeStruct((DEC_BATCH // NS, N_SEL + 1, HD), jnp.int32)],
        compiler_params=_params(2),
        name="nsa_sample_cmp",
    )(page_table, slopes, kc_new, vc_new, q, pe_bf, w1_flat, w2_bf, cmat, tri,
      *([ck_pool] * N_PAGES), *([cv_pool] * N_PAGES))


def _attend_rows(q8, slope, k, v, dist, k_own, v_own):
    s = _dot_nt(q8.astype(BF), k.astype(BF)) * SCALE - slope * dist
    s_own = _rowdot(q8, k_own) * SCALE
    m = jnp.maximum(jnp.max(s, axis=1, keepdims=True), s_own)
    p = jnp.exp(s - m)
    p_own = jnp.exp(s_own - m)
    den = jnp.sum(p, axis=1, keepdims=True) + p_own
    return (_dot(p.astype(BF), v.astype(BF)) + p_own * v_own) / den


BLK_ROWS = SEL_BLOCK * NSA_G


def _nsa_sample_attend_kernel(pt_ref, ix_ref, slopes_ref, q_ref, gn_ref, ksn_ref, vsn_ref, kwn_ref, vwn_ref,
                              ocmp_ref, wk_ref, wv_ref, sk_hbm, sv_hbm, o_ref, wko_ref, wvo_ref, kbuf, vbuf, sem):
    n = pl.program_id(0)
    halves = PAGE // SEL_BLOCK

    def block_copies(sample, slot):
        copies = []
        for g in range(NSA_G):
            for c in range(N_SEL):
                j = ix_ref[(sample * NSA_G + g) * N_SEL + c]
                row0 = pl.multiple_of((pt_ref[sample, j // halves] * halves + j % halves) * BLK_ROWS, BLK_ROWS)
                dst = pl.ds((g * N_SEL + c) * BLK_ROWS, BLK_ROWS)
                for pool, buf in ((sk_hbm, kbuf), (sv_hbm, vbuf)):
                    copies.append(pltpu.make_async_copy(pool.at[pl.ds(row0, BLK_ROWS), :], buf.at[slot, dst, :],
                                                        sem.at[slot]))
        return copies

    @pl.when(n == 0)
    def _():
        for cp in block_copies(0, 0):
            cp.start()

    @pl.when(n + 1 < DEC_BATCH)
    def _():
        for cp in block_copies(n + 1, (n + 1) % 2):
            cp.start()

    slot = n % 2
    for cp in block_copies(n, slot):
        cp.wait()

    wl = wk_ref.shape[0] // NSA_G
    off = lax.broadcasted_iota(jnp.int32, (1, SEL_BLOCK), 1)
    dist_w = (wl - lax.broadcasted_iota(jnp.int32, (1, wl), 1)).astype(F32)
    row8 = lax.broadcasted_iota(jnp.int32, (8, 1), 0)
    gates = jax.nn.sigmoid(gn_ref[0])
    for g in range(NSA_G):
        gsl = slice(g * HD, (g + 1) * HD)
        heads = [g * NSA_R + r for r in range(NSA_R)]
        q8 = jnp.concatenate([q_ref[0, :, hd * HD:(hd + 1) * HD] for hd in heads]
                             + [jnp.zeros((8 - NSA_R, HD), F32)], axis=0)
        slope = jnp.where(row8 == 0, slopes_ref[heads[0]],
                          jnp.where(row8 == 1, slopes_ref[heads[1]], slopes_ref[heads[2]]))
        sel_rows = pl.ds(g * N_SEL * BLK_ROWS + g, N_SEL * SEL_BLOCK, stride=NSA_G)
        pos = jnp.concatenate([ix_ref[(n * NSA_G + g) * N_SEL + c] * SEL_BLOCK + off for c in range(N_SEL)], axis=1)
        o_sel = _attend_rows(q8, slope, kbuf[slot, sel_rows, :], vbuf[slot, sel_rows, :],
                             (PAST_LEN - pos).astype(F32), ksn_ref[0, :, gsl], vsn_ref[0, :, gsl])
        o_win = _attend_rows(q8, slope, wk_ref[pl.ds(g, wl, stride=NSA_G), :], wv_ref[pl.ds(g, wl, stride=NSA_G), :],
                             dist_w, kwn_ref[0, :, gsl], vwn_ref[0, :, gsl])
        for r, hd in enumerate(heads):
            o_ref[0, :, hd * HD:(hd + 1) * HD] = (
                gates[:, 3 * hd:3 * hd + 1] * ocmp_ref[0, g, r:r + 1, :]
                + gates[:, 3 * hd + 1:3 * hd + 2] * o_sel[r:r + 1] + gates[:, 3 * hd + 2:3 * hd + 3] * o_win[r:r + 1])

    for cache, new, out in ((wk_ref, kwn_ref, wko_ref), (wv_ref, vwn_ref, wvo_ref)):
        out[...] = pltpu.roll(cache[...], shift=(wl - 1) * NSA_G, axis=0)
        for g in range(NSA_G):
            out[pl.ds((wl - 1) * NSA_G + g, 1), :] = new[0, :, g * HD:(g + 1) * HD]


def _nsa_sample_attend(page_table, sel_idx, slopes, q, gn, ks_new, vs_new, kw_new, vw_new, o_cmp,
                       win_k, win_v, sk_pool, sv_pool):
    row = lambda w: pl.BlockSpec((1, 1, w), lambda n, pt, ix: (n, 0, 0))
    win_rows = win_k.shape[0] // DEC_BATCH
    win = lambda: pl.BlockSpec((win_rows, HD), lambda n, pt, ix: (n, 0))
    any_spec = pl.BlockSpec(memory_space=pl.ANY)
    buf = pltpu.VMEM((2, NSA_G * N_SEL * BLK_ROWS, HD), F32)
    return pl.pallas_call(
        _nsa_sample_attend_kernel,
        grid_spec=pltpu.PrefetchScalarGridSpec(
            num_scalar_prefetch=2, grid=(DEC_BATCH,),
            in_specs=[pl.BlockSpec(memory_space=pltpu.SMEM), row(NSA_W), row(HD), row(KV_W), row(KV_W), row(KV_W),
                      row(KV_W), pl.BlockSpec((1, NSA_G, 8, HD), lambda n, pt, ix: (n, 0, 0, 0)), win(), win(),
                      any_spec, any_spec],
            out_specs=[row(NSA_W), win(), win()],
            scratch_shapes=[buf, buf, pltpu.SemaphoreType.DMA((2,))]),
        out_shape=[jax.ShapeDtypeStruct((DEC_BATCH, 1, NSA_W), F32), jax.ShapeDtypeStruct(win_k.shape, F32),
                   jax.ShapeDtypeStruct(win_v.shape, F32)],
        compiler_params=_params(1),
        name="nsa_sample_attend",
    )(page_table, sel_idx, slopes, q, gn, ks_new, vs_new, kw_new, vw_new, o_cmp, win_k, win_v, sk_pool, sv_pool)


def _mem_sample_kernel(q_ref, k_ref, v_ref, o_ref):
    for h in range(MEM_HEADS):
        sl = slice(h * HD, (h + 1) * HD)
        q = q_ref[0, :, sl]
        k = k_ref[pl.ds(h, MEM_LEN, stride=MEM_HEADS), :]
        v = v_ref[pl.ds(h, MEM_LEN, stride=MEM_HEADS), :]
        s = _rowdot(k, q) * SCALE
        p = jnp.exp(s - jnp.max(s, axis=0, keepdims=True))
        o_ref[0, :, sl] = jnp.sum(p * v, axis=0, keepdims=True) / jnp.sum(p, axis=0, keepdims=True)


def _mem_sample(q, mem_k, mem_v):
    rows = MEM_LEN * MEM_HEADS
    row = lambda: pl.BlockSpec((1, 1, MEM_W), lambda n: (n, 0, 0))
    kv = lambda: pl.BlockSpec((rows, HD), lambda n: (n, 0))
    return pl.pallas_call(
        _mem_sample_kernel,
        grid=(DEC_BATCH,),
        in_specs=[row(), kv(), kv()],
        out_specs=row(),
        out_shape=jax.ShapeDtypeStruct((DEC_BATCH, 1, MEM_W), F32),
        compiler_params=_params(1),
        name="mem_attend_sample",
    )(q, mem_k, mem_v)


def _project_all(x_bf, w, tm, paged, tag):
    kv_kind = "pages" if paged else "flat"
    qa, ka, va, qb = _proj(x_bf, w["wa"], w["ba"], MOBA_W, ["flat", kv_kind, kv_kind, "flat"], tm,
                           "proj_moba_q_" + tag, True)
    kc, vc, ks, vs, kw, vw = _proj(x_bf, w["wb"], w["bb"], KV_W, ["flat"] * 6, tm, "proj_nsa_kv_" + tag, True)
    (qm,) = _proj(x_bf, w["wm"], w["bm"], MEM_W, ["flat"], tm, "proj_mem_q_" + tag, True)
    (gn,) = _proj(x_bf, w["wg"], w["bg"], HD, ["flat"], tm, "proj_nsa_gate_" + tag, True)
    gbr = _matmul(x_bf, w["wd"], w["bd"], tm, 768, "proj_branch_gate_" + tag)
    return qa, ka, va, qb, kc, vc, ks, vs, kw, vw, qm, gn, gbr


def _strict_lower_ones():
    return jnp.asarray(np.tril(np.ones((HD, HD), np.float32), -1), BF)


def kernel(x_prompt, x_sample, cache_mem_k, cache_mem_v, cache_moba_k, cache_moba_v, cache_nsa_cmp_k, cache_nsa_cmp_v, cache_nsa_sel_k, cache_nsa_sel_v, cache_nsa_win_k, cache_nsa_win_v, cache_ffn_conv, page_table, mem_prompt, w_in, b_in, w_mem_kv, cmp_pe, cmp_w1, cmp_w2, p_moba, p_nsa, p_mem, w_o, ln1_g, ln1_b, w_up, conv_w, conv_b, w_down, ln2_g, ln2_b):
    slopes = _alibi_slopes(MOBA_HEADS)
    w_bf = w_in[0].T.astype(BF)
    b0 = b_in[0][None, :]
    c0, c1, c2, c3 = 4 * MOBA_W, 4 * MOBA_W + 6 * KV_W, 4 * MOBA_W + 6 * KV_W + N_GATE, 4 * MOBA_W + 6 * KV_W + N_GATE + MEM_W
    w = {"wa": w_bf[:c0], "ba": b0[:, :c0], "wb": w_bf[c0:c1], "bb": b0[:, c0:c1],
         "wg": jnp.pad(w_bf[c1:c2], ((0, HD - N_GATE), (0, 0))), "bg": jnp.pad(b0[:, c1:c2], ((0, 0), (0, HD - N_GATE))),
         "wm": w_bf[c2:c3], "bm": b0[:, c2:c3], "wd": w_bf[c3:], "bd": b0[:, c3:]}
    pe_bf, w1_bf, w2_bf = _pack_cmp_weights(cmp_pe[0], cmp_w1[0], cmp_w2[0])
    pa_bf, pb_bf, pm_bf, wo_bf = p_moba[0].astype(BF), p_nsa[0].astype(BF), p_mem[0].astype(BF), w_o[0].astype(BF)
    wu_bf, wd_bf = w_up[0].astype(BF), w_down[0].astype(BF)

    xp = x_prompt.reshape(BATCH * SEQ, D_MODEL)
    xp_bf = xp.astype(BF)
    qa, ka, va, qb, kc, vc, ks, vs, kw, vw, qm, gn, gbr = _project_all(xp_bf, w, 1024, True, "prompt")
    mem_k, mem_v = _proj(mem_prompt.reshape(BATCH * MEM_LEN, D_MODEL).astype(BF), w_mem_kv[0].astype(BF),
                         jnp.zeros((1, 2 * MEM_W), F32), MEM_W, ["flat", "flat"], 512, "proj_mem_kv")
    slope_rows = _slope_rows(MOBA_HEADS)
    o_a = _moba_prompt(slope_rows, qa, ka, va)
    ck, cv = _compress_prompt(kc, vc, pe_bf, w1_bf, w2_bf)
    cmat = _cmp_to_sel_matrix(N_CHUNK - 1, SEQ // SEL_BLOCK - 1)
    o_b = _nsa_prompt(slopes, slope_rows, qb, gn, ks, vs, kw, vw, ck, cv, cmat)
    o_m = _mem_attend_prompt(qm, mem_k, mem_v)
    merged = _merge(o_a, o_b, o_m, gbr, pa_bf, pb_bf, pm_bf, 1024, "prompt")
    h, h_bf = _out_ln(xp, merged, wo_bf, ln1_g, ln1_b, 512, "prompt")
    y_p, u_tail = _ffn(h, h_bf, wu_bf, conv_w[0], conv_b, wd_bf, ln2_g, ln2_b, None, None, FFN_TM, SEQ, "prompt")

    pages = lambda a: a.reshape(1, BATCH, SEQ // PAGE, PAGE, NSA_G, HD)
    head_pages = lambda a: a.transpose(0, 1, 3, 2, 4)[None]
    last = lambda a: a.reshape(BATCH, SEQ, NSA_G, HD)[None, :, SEQ - WINDOW:]
    tiles_per_seq = SEQ // FFN_TM
    conv_p = u_tail.reshape(BATCH, tiles_per_seq, HALO, FFN_DIM)[:, -1, HALO - 2:][None]
    prompt_out = (y_p.reshape(BATCH, SEQ, D_MODEL),
                  mem_k.reshape(1, BATCH, MEM_LEN, MEM_HEADS, HD), mem_v.reshape(1, BATCH, MEM_LEN, MEM_HEADS, HD),
                  head_pages(ka), head_pages(va), pages(kc), pages(vc), pages(ks), pages(vs), last(kw), last(vw), conv_p)

    n = DEC_BATCH
    xs = x_sample.reshape(n, D_MODEL)
    qa, ka, va, qb, kc, vc, ks, vs, kw, vw, qm, gn, gbr = _project_all(xs.astype(BF), w, n, False, "sample")
    r3 = lambda a: a.reshape(n, 1, a.shape[-1])
    moba_pool = lambda c: c[0].transpose(0, 2, 1, 3)
    rows2d = lambda c: c.reshape(-1, HD)
    p_a, stat_a, ids_a = _moba_sample_probs(page_table, slopes, r3(qa), r3(ka), moba_pool(cache_moba_k))
    o_a = _moba_sample_values(page_table, ids_a[:, :MOBA_HEADS, :MOBA_TOPK].reshape(-1), p_a, stat_a, r3(va),
                              moba_pool(cache_moba_v))
    w1_flat = w1_bf.reshape(2, CMP_STRIDE * HD, 2 * HD)
    cmat_s = _cmp_to_sel_matrix(N_CHUNK, N_CAND_S)
    o_cmp, sel = _nsa_sample_cmp(page_table, slopes, kc, vc, qb, pe_bf, w1_flat, w2_bf, cmat_s, _strict_lower_ones(),
                                 rows2d(cache_nsa_cmp_k), rows2d(cache_nsa_cmp_v))
    sel_idx = sel[:, :N_SEL, :NS * NSA_G].transpose(0, 2, 1).reshape(-1)
    o_b, win_k_new, win_v_new = _nsa_sample_attend(
        page_table, sel_idx, slopes, r3(qb), r3(gn), r3(ks), r3(vs), r3(kw), r3(vw), o_cmp,
        rows2d(cache_nsa_win_k), rows2d(cache_nsa_win_v), rows2d(cache_nsa_sel_k), rows2d(cache_nsa_sel_v))
    o_m = _mem_sample(r3(qm), rows2d(cache_mem_k), rows2d(cache_mem_v))
    flat_bf = lambda a: a.reshape(n, a.shape[-1]).astype(BF)
    merged = _merge(flat_bf(o_a), flat_bf(o_b), flat_bf(o_m), gbr, pa_bf, pb_bf, pm_bf, n, "sample")
    h, h_bf = _out_ln(xs, merged, wo_bf, ln1_g, ln1_b, n, "sample")
    conv_old = cache_ffn_conv[0]
    y_s, u_new = _ffn(h, h_bf, wu_bf, conv_w[0], conv_b, wd_bf, ln2_g, ln2_b, conv_old[:, 0], conv_old[:, 1], n, 0,
                      "sample")

    heads = lambda a, nh: a.reshape(1, n, 1, nh, HD)
    sample_out = (heads(ka, MOBA_HEADS), heads(va, MOBA_HEADS), heads(kc, NSA_G), heads(vc, NSA_G),
                  heads(ks, NSA_G), heads(vs, NSA_G), win_k_new.reshape(cache_nsa_win_k.shape),
                  win_v_new.reshape(cache_nsa_win_v.shape),
                  jnp.stack([conv_old[:, 1], u_new], axis=1)[None])
    return (prompt_out[0], y_s.reshape(n, 1, D_MODEL)) + prompt_out[1:] + sample_out
```

```python
import functools

import numpy as np
import jax
import jax.numpy as jnp
from jax import lax
from jax.experimental import pallas as pl
from jax.experimental.pallas import tpu as pltpu

D_MODEL = 2048
BATCH = 4
SEQ = 2048
DEC_BATCH = 128
PAST_LEN = 2048
PAGE = 128
N_PAGES = PAST_LEN // PAGE
HD = 128
MOBA_HEADS = 6
MOBA_BLOCK = 256
MOBA_TOPK = 3
NSA_HEADS = 6
NSA_G = 2
NSA_R = 3
CMP_LEN = 32
CMP_STRIDE = 16
SEL_BLOCK = 64
NSA_TOPN = 16
WINDOW = 512
MEM_HEADS = 4
MEM_LEN = 256
FFN_DIM = 5632
DN_ALPHA = 2.0 ** 0.25
LN_EPS = 1e-5
SCALE = HD ** -0.5
MOBA_W = MOBA_HEADS * HD
NSA_W = NSA_HEADS * HD
KV_W = NSA_G * HD
MEM_W = MEM_HEADS * HD
N_GATE = NSA_HEADS * 3

F32 = jnp.float32
BF = jnp.bfloat16
NEG_BIG = -(2.0 ** 100)
VMEM_LIMIT = 48 * 1024 * 1024
TQ = 256


def _dot(a, b):
    return jnp.dot(a, b, preferred_element_type=F32)


def _dot_nt(a, b):
    return lax.dot_general(a, b, (((1,), (1,)), ((), ())), preferred_element_type=F32)


def _split_bf16(a):
    hi = a.astype(BF)
    lo = (a - hi.astype(F32)).astype(BF)
    return hi, lo


def _alibi_slopes(n):
    return jnp.asarray(2.0 ** (-8.0 * np.arange(1, n + 1) / n), F32)


def _gelu_tanh(x):
    return 0.5 * x * (1.0 + jnp.tanh(np.sqrt(2.0 / np.pi).astype(np.float32) * (x + 0.044715 * (x * x * x))))


def _params(n_axes):
    return pltpu.CompilerParams(dimension_semantics=("arbitrary",) * n_axes, vmem_limit_bytes=VMEM_LIMIT)


def _proj_kernel(x_ref, w_ref, b_ref, *out_refs, kinds, w_rows):
    j = pl.program_id(1)
    acc = (_dot_nt if w_rows else _dot)(x_ref[...], w_ref[...]) + b_ref[...]
    for idx, (kind, o_ref) in enumerate(zip(kinds, out_refs)):
        @pl.when(j == idx)
        def _(kind=kind, o_ref=o_ref):
            if kind == "flat":
                o_ref[...] = acc
            else:
                for p in range(o_ref.shape[1]):
                    for h in range(o_ref.shape[2]):
                        o_ref[0, p, h] = acc[p * PAGE:(p + 1) * PAGE, h * HD:(h + 1) * HD]


def _proj(x_bf, w_bf, b, tn, kinds, tm, name, w_rows=False):
    rows, k = x_bf.shape
    n_out = len(kinds)
    assert w_bf.shape == ((n_out * tn, k) if w_rows else (k, n_out * tn)) and rows % tm == 0
    w_spec = pl.BlockSpec((tn, k), lambda i, j: (j, 0)) if w_rows else pl.BlockSpec((k, tn), lambda i, j: (0, j))
    tiles_per_batch = SEQ // tm if tm <= SEQ else 1
    out_shape, out_specs = [], []
    for idx, kind in enumerate(kinds):
        if kind == "flat":
            out_shape.append(jax.ShapeDtypeStruct((rows, tn), F32))
            out_specs.append(pl.BlockSpec((tm, tn), lambda i, j: (i, 0)))
        else:
            heads = tn // HD
            out_shape.append(jax.ShapeDtypeStruct((rows // SEQ, SEQ // PAGE, heads, PAGE, HD), F32))
            out_specs.append(pl.BlockSpec((1, tm // PAGE, heads, PAGE, HD),
                                          lambda i, j: (i // tiles_per_batch, i % tiles_per_batch, 0, 0, 0)))
    return pl.pallas_call(
        functools.partial(_proj_kernel, kinds=tuple(kinds), w_rows=w_rows),
        grid=(rows // tm, n_out),
        in_specs=[pl.BlockSpec((tm, k), lambda i, j: (i, 0)), w_spec,
                  pl.BlockSpec((1, tn), lambda i, j: (0, j))],
        out_specs=out_specs,
        out_shape=out_shape,
        compiler_params=_params(2),
        name=name,
    )(x_bf, w_bf, b)


def _topk_member(v, cand, n_cand, k):
    member = jnp.zeros(v.shape, F32)
    for j in range(n_cand):
        vj = v[j:j + 1, :]
        beats = (v > vj) | ((v == vj) & (cand < j))
        cnt = jnp.sum(beats.astype(F32), axis=0, keepdims=True)
        member = jnp.where((cand == j) & (cnt < k), 1.0, member)
    return member


def _to_rows(member_t):
    pad = jnp.zeros((HD - member_t.shape[0], member_t.shape[1]), F32)
    return jnp.transpose(jnp.concatenate([member_t, pad], axis=0))


EXT_MASK = 32
N_PIECE = 3
POS_SPLIT = 64
EXP2_SCALE = float(SCALE * np.log2(np.e))


def _slope_rows(n_heads):
    slope = np.asarray(2.0 ** (-8.0 * np.arange(1, n_heads + 1) / n_heads), np.float32) / np.float32(SCALE)
    rows = np.zeros((8, HD), np.float32)
    rest = slope.astype(np.float32)
    for i in range(N_PIECE):
        piece = rest.astype(BF).astype(np.float32)
        rows[:n_heads, EXT_MASK + i] = piece * POS_SPLIT
        rows[:n_heads, EXT_MASK + N_PIECE + i] = piece
        rest = rest - piece
    return jnp.asarray(rows)


def _key_ext(n_keys, block):
    key = lax.broadcasted_iota(jnp.int32, (n_keys, HD), 0)
    lane = lax.broadcasted_iota(jnp.int32, (n_keys, HD), 1)
    hot = jnp.where(key // block == lane, 1.0, 0.0)
    pos = jnp.where(lane < EXT_MASK + N_PIECE, (key // POS_SPLIT).astype(F32), (key % POS_SPLIT).astype(F32))
    return jnp.where(lane < EXT_MASK, hot, jnp.where(lane < EXT_MASK + 2 * N_PIECE, pos, 0.0)).astype(BF)


def _query_ext(allowed, slope_row, lane):
    mask = 0.0 if allowed is None else jnp.where(allowed, 0.0, NEG_BIG)
    return jnp.where(lane < EXT_MASK, mask, slope_row).astype(BF)


def _flash_init(qx, kx, v_bf, keep):
    s = jnp.where(keep, _dot_nt(qx, kx), NEG_BIG)
    m = jnp.max(s, axis=1, keepdims=True)
    p = jnp.exp2((s - m) * EXP2_SCALE)
    return m, jnp.sum(p, axis=1, keepdims=True), _dot(p.astype(BF), v_bf)


def _flash_step(state, qx, kx, v_bf, keep=None):
    m_old, l_old, acc_old = state
    s = _dot_nt(qx, kx)
    if keep is not None:
        s = jnp.where(keep, s, NEG_BIG)
    m_new = jnp.maximum(m_old, jnp.max(s, axis=1, keepdims=True))
    alpha = jnp.exp2((m_old - m_new) * EXP2_SCALE)
    p = jnp.exp2((s - m_new) * EXP2_SCALE)
    return (m_new, alpha * l_old + jnp.sum(p, axis=1, keepdims=True),
            alpha * acc_old + _dot(p.astype(BF), v_bf))


MOBA_HB = 6


def _moba_prompt_kernel(srow_ref, q_ref, k_ref, v_ref, o_ref, kx_ref, vb_ref, kmh_ref, kml_ref):
    hg = pl.program_id(1)
    qi = pl.program_id(2)
    nb = SEQ // MOBA_BLOCK

    @pl.when(qi == 0)
    def _():
        ext = _key_ext(SEQ, MOBA_BLOCK)
        for hh in range(MOBA_HB):
            k = k_ref[:, hh].reshape(SEQ, HD)
            kx_ref[hh, :, :HD] = k.astype(BF)
            kx_ref[hh, :, HD:] = ext
            vb_ref[hh] = v_ref[:, hh].reshape(SEQ, HD).astype(BF)
            km = jnp.sum(k.reshape(nb, MOBA_BLOCK, HD), axis=1) / MOBA_BLOCK
            hi, lo = _split_bf16(jnp.concatenate([km, jnp.zeros((HD - nb, HD), F32)], axis=0))
            kmh_ref[hh] = hi
            kml_ref[hh] = lo

    lane = lax.broadcasted_iota(jnp.int32, (TQ, HD), 1)
    cand = lax.broadcasted_iota(jnp.int32, (nb, TQ), 0)
    rr = lax.broadcasted_iota(jnp.int32, (TQ, MOBA_BLOCK), 0)
    cc = lax.broadcasted_iota(jnp.int32, (TQ, MOBA_BLOCK), 1)
    own = pl.multiple_of(qi * MOBA_BLOCK, MOBA_BLOCK)
    qx, states = [], []
    for hh in range(MOBA_HB):
        q = q_ref[:, hh * HD:(hh + 1) * HD]
        qh, ql = _split_bf16(q)
        sc = (_dot_nt(kmh_ref[hh], qh) + _dot_nt(kml_ref[hh], qh) + _dot_nt(kmh_ref[hh], ql))[:nb]
        sc = jnp.where(cand < qi, sc, -jnp.inf)
        member = _topk_member(sc, cand, nb - 1, MOBA_TOPK)
        allowed = (_to_rows(member) > 0.5) | (lane == qi)
        ext = _query_ext(allowed, srow_ref[pl.ds(hg * MOBA_HB + hh, 1), :], lane)
        qx.append(jnp.concatenate([qh, ext], axis=1))
        states.append(_flash_init(qx[hh], kx_ref[hh, pl.ds(own, MOBA_BLOCK), :],
                                  vb_ref[hh, pl.ds(own, MOBA_BLOCK), :], cc <= rr))

    def body(c, states):
        rows = pl.ds(pl.multiple_of(c * MOBA_BLOCK, MOBA_BLOCK), MOBA_BLOCK)
        return tuple(_flash_step(states[hh], qx[hh], kx_ref[hh, rows, :], vb_ref[hh, rows, :])
                     for hh in range(MOBA_HB))

    states = lax.fori_loop(0, qi, body, tuple(states))
    for hh in range(MOBA_HB):
        m, l, acc = states[hh]
        o_ref[:, hh * HD:(hh + 1) * HD] = (acc / l).astype(BF)


def _moba_prompt(slope_rows, q, k_pages, v_pages):
    nq = SEQ // TQ
    kv_spec = pl.BlockSpec((None, SEQ // PAGE, MOBA_HB, PAGE, HD), lambda b, h, qi: (b, 0, h, 0, 0))
    q_spec = pl.BlockSpec((TQ, MOBA_HB * HD), lambda b, h, qi: (b * nq + qi, h))
    return pl.pallas_call(
        _moba_prompt_kernel,
        grid=(BATCH, MOBA_HEADS // MOBA_HB, nq),
        in_specs=[pl.BlockSpec((8, HD), lambda b, h, qi: (0, 0)), q_spec, kv_spec, kv_spec],
        out_specs=q_spec,
        out_shape=jax.ShapeDtypeStruct((BATCH * SEQ, MOBA_W), BF),
        scratch_shapes=[pltpu.VMEM((MOBA_HB, SEQ, 2 * HD), BF), pltpu.VMEM((MOBA_HB, SEQ, HD), BF),
                        pltpu.VMEM((MOBA_HB, HD, HD), BF), pltpu.VMEM((MOBA_HB, HD, HD), BF)],
        compiler_params=_params(3),
        name="moba_prompt",
    )(slope_rows, q, k_pages, v_pages)


N_CHUNK = SEQ // CMP_STRIDE


def _compress_tokens(get_x, w1_ref, w2_ref, c_row, b_last):
    acc = jnp.zeros((N_CHUNK, 2 * HD), F32)
    for r in range(CMP_STRIDE):
        acc = acc + _dot(get_x(r).astype(BF), w1_ref[r])
    top, bot = acc[:, :HD], acc[:, HD:]
    nxt = pltpu.roll(bot, shift=N_CHUNK - 1, axis=0)
    row = lax.broadcasted_iota(jnp.int32, (N_CHUNK, HD), 0)
    pre = top + jnp.where(row == N_CHUNK - 1, b_last, nxt) + c_row
    return _dot(_gelu_tanh(pre).astype(BF), w2_ref[...])


def _compress_prompt_kernel(kc_ref, vc_ref, pe_ref, w1_ref, w2_ref, ck_ref, cv_ref):
    for which, (src, dst) in enumerate(((kc_ref, ck_ref), (vc_ref, cv_ref))):
        w1 = w1_ref.at[which]
        c_row = _pe_term(pe_ref, w1, which)
        get_x = lambda r, src=src: src[pl.ds(r, N_CHUNK, stride=CMP_STRIDE), :]
        dst[0, 0] = _compress_tokens(get_x, w1, w2_ref.at[which], c_row, jnp.zeros((1, HD), F32))


def _pe_term(pe_ref, w1, which):
    c_row = jnp.zeros((1, HD), F32)
    for r in range(CMP_LEN):
        half, rr = divmod(r, CMP_STRIDE)
        c_row = c_row + _dot(pe_ref[which, :, r * HD:(r + 1) * HD], w1[rr, :, half * HD:(half + 1) * HD])
    return c_row


def _pack_cmp_weights(cmp_pe, cmp_w1, cmp_w2):
    w1 = cmp_w1.reshape(2, 2, CMP_STRIDE, HD, HD).transpose(0, 2, 3, 1, 4).reshape(2, CMP_STRIDE, HD, 2 * HD)
    return cmp_pe.reshape(2, 1, CMP_LEN * HD).astype(BF), w1.astype(BF), cmp_w2.astype(BF)


def _compress_prompt(kc, vc, pe_bf, w1_bf, w2_bf):
    full = lambda a: pl.BlockSpec(a.shape, lambda b, g: (0,) * a.ndim)
    return pl.pallas_call(
        _compress_prompt_kernel,
        grid=(BATCH, NSA_G),
        in_specs=[pl.BlockSpec((SEQ, HD), lambda b, g: (b, g)), pl.BlockSpec((SEQ, HD), lambda b, g: (b, g)),
                  full(pe_bf), full(w1_bf), full(w2_bf)],
        out_specs=[pl.BlockSpec((1, 1, N_CHUNK, HD), lambda b, g: (b, g, 0, 0))] * 2,
        out_shape=[jax.ShapeDtypeStruct((BATCH, NSA_G, N_CHUNK, HD), F32)] * 2,
        compiler_params=_params(2),
        name="nsa_compress_prompt",
    )(kc, vc, pe_bf, w1_bf, w2_bf)


def _cmp_to_sel_matrix(n_cmp, n_blocks):
    start = np.arange(n_cmp)[:, None] * CMP_STRIDE
    j0 = np.arange(n_blocks)[None, :] * SEL_BLOCK
    ov = np.minimum(start + CMP_LEN, j0 + SEL_BLOCK) - np.maximum(start, j0)
    m = np.zeros((HD, HD), np.float32)
    m[:n_blocks, :n_cmp] = (np.clip(ov, 0, None) / CMP_LEN).T
    return jnp.asarray(m, BF)


def _cmp_attend(q, ck_hi, ck_lo, cv_bf, slope, dist):
    qh, ql = _split_bf16(q)
    s = (_dot_nt(qh, ck_hi) + _dot_nt(qh, ck_lo) + _dot_nt(ql, ck_hi)) * SCALE - slope * dist.astype(F32)
    s = jnp.where(dist >= 0, s, -jnp.inf)
    m = jnp.max(s, axis=1, keepdims=True)
    m = jnp.where(m > -jnp.inf, m, 0.0)
    p = jnp.exp(s - m)
    p = p / jnp.maximum(jnp.sum(p, axis=1, keepdims=True), 1e-30)
    return _dot(p.astype(BF), cv_bf), p


def _select_blocks(p_grp, cmat_bf, cur, n_cand, k):
    ph, plo = _split_bf16(p_grp)
    imp = (_dot_nt(cmat_bf, ph) + _dot_nt(cmat_bf, plo))[:EXT_MASK]
    cand = lax.broadcasted_iota(jnp.int32, imp.shape, 0)
    valid = (cand < cur) & (cand < n_cand)
    forced = valid & ((cand == 0) | (cand == cur - 1))
    v = jnp.where(forced, jnp.inf, jnp.where(valid, imp, -jnp.inf))
    return jnp.where(valid, _topk_member(v, cand, n_cand, k), 0.0)


def _nsa_prompt_kernel(slopes_ref, srow_ref, q_ref, gn_ref, ks_ref, vs_ref, kw_ref, vw_ref, ck_ref, cv_ref, cmat_ref,
                       o_ref, kx_ref, kwx_ref, vsb_ref, vwb_ref):
    qi = pl.program_id(1)
    chunk = TQ
    rows = NSA_R * TQ
    groups = range(NSA_G)

    @pl.when(qi == 0)
    def _():
        ext = _key_ext(SEQ, SEL_BLOCK)
        for g in groups:
            gsl = slice(g * HD, (g + 1) * HD)
            kx_ref[g, :, :HD] = ks_ref[:, gsl].astype(BF)
            kx_ref[g, :, HD:] = ext
            kwx_ref[g, :, :HD] = kw_ref[:, gsl].astype(BF)
            kwx_ref[g, :, HD:] = ext
            vsb_ref[g] = vs_ref[:, gsl].astype(BF)
            vwb_ref[g] = vw_ref[:, gsl].astype(BF)

    lane = lax.broadcasted_iota(jnp.int32, (TQ, HD), 1)
    t = qi * TQ + lax.broadcasted_iota(jnp.int32, (TQ, HD), 0)
    stack = lambda parts: jnp.concatenate(parts, axis=0)
    dist_c = stack([t - (lane * CMP_STRIDE + CMP_LEN - 1)] * NSA_R)
    cur_t = (qi * TQ + lax.broadcasted_iota(jnp.int32, (EXT_MASK, TQ), 1)) // SEL_BLOCK

    qx, qw, o_cmp = [], [], []
    for g in groups:
        q_all = stack([q_ref[:, (g * NSA_R + r) * HD:(g * NSA_R + r + 1) * HD] for r in range(NSA_R)])
        ck_hi, ck_lo = _split_bf16(ck_ref[0, g])
        slope_col = stack([jnp.full((TQ, 1), slopes_ref[g * NSA_R + r], F32) for r in range(NSA_R)])
        o_c, p = _cmp_attend(q_all, ck_hi, ck_lo, cv_ref[0, g].astype(BF), slope_col, dist_c)
        o_cmp.append(o_c)
        p_grp = p[0:TQ] + p[TQ:2 * TQ] + p[2 * TQ:3 * TQ]
        member = _select_blocks(p_grp, cmat_ref[...], cur_t, SEQ // SEL_BLOCK - 1, NSA_TOPN - 1)
        allowed = (_to_rows(member) > 0.5) | (lane == t // SEL_BLOCK)
        srow = [srow_ref[g * NSA_R + r:g * NSA_R + r + 1, :] for r in range(NSA_R)]
        q_bf = q_all.astype(BF)
        qx.append(jnp.concatenate([q_bf, stack([_query_ext(allowed, srow[r], lane) for r in range(NSA_R)])], axis=1))
        qw.append(jnp.concatenate([q_bf, stack([_query_ext(None, srow[r], lane) for r in range(NSA_R)])], axis=1))

    rr = lax.broadcasted_iota(jnp.int32, (rows, chunk), 0) % TQ
    cc = lax.broadcasted_iota(jnp.int32, (rows, chunk), 1)
    own = pl.ds(pl.multiple_of(qi * chunk, chunk), chunk)
    past = lambda c: pl.ds(pl.multiple_of(c * chunk, chunk), chunk)

    st = tuple(_flash_init(qx[g], kx_ref[g, own, :], vsb_ref[g, own, :], cc <= rr) for g in groups)
    st = lax.fori_loop(0, qi, lambda c, st: tuple(
        _flash_step(st[g], qx[g], kx_ref[g, past(c), :], vsb_ref[g, past(c), :]) for g in groups), st)
    o_sel = [st[g][2] / st[g][1] for g in groups]

    st = tuple(_flash_init(qw[g], kwx_ref[g, own, :], vwb_ref[g, own, :], cc <= rr) for g in groups)

    def win_step(c, st):
        slack = jnp.where(c == qi - WINDOW // chunk, 0, chunk)
        return tuple(_flash_step(st[g], qw[g], kwx_ref[g, past(c), :], vwb_ref[g, past(c), :], rr - cc <= slack)
                     for g in groups)

    st = lax.fori_loop(jnp.maximum(qi - WINDOW // chunk, 0), qi, win_step, st)
    o_win = [st[g][2] / st[g][1] for g in groups]

    gates = jax.nn.sigmoid(gn_ref[:, 0:NSA_HEADS * 3])
    for g in groups:
        for r in range(NSA_R):
            sl = slice(r * TQ, (r + 1) * TQ)
            hd = g * NSA_R + r
            o = (gates[:, 3 * hd:3 * hd + 1] * o_cmp[g][sl] + gates[:, 3 * hd + 1:3 * hd + 2] * o_sel[g][sl]
                 + gates[:, 3 * hd + 2:3 * hd + 3] * o_win[g][sl])
            o_ref[:, hd * HD:(hd + 1) * HD] = o.astype(BF)


def _nsa_prompt(slopes, slope_rows, qb, gn, ks, vs, kw, vw, ck, cv, cmat):
    nq = SEQ // TQ
    kv_spec = lambda: pl.BlockSpec((SEQ, KV_W), lambda b, qi: (b, 0))
    c_spec = lambda: pl.BlockSpec((1, NSA_G, N_CHUNK, HD), lambda b, qi: (b, 0, 0, 0))
    return pl.pallas_call(
        _nsa_prompt_kernel,
        grid=(BATCH, nq),
        in_specs=[pl.BlockSpec(memory_space=pltpu.SMEM),
                  pl.BlockSpec((8, HD), lambda b, qi: (0, 0)),
                  pl.BlockSpec((TQ, NSA_W), lambda b, qi: (b * nq + qi, 0)),
                  pl.BlockSpec((TQ, HD), lambda b, qi: (b * nq + qi, 0)),
                  kv_spec(), kv_spec(), kv_spec(), kv_spec(), c_spec(), c_spec(),
                  pl.BlockSpec((HD, HD), lambda b, qi: (0, 0))],
        out_specs=pl.BlockSpec((TQ, NSA_W), lambda b, qi: (b * nq + qi, 0)),
        out_shape=jax.ShapeDtypeStruct((BATCH * SEQ, NSA_W), BF),
        scratch_shapes=[pltpu.VMEM((NSA_G, SEQ, 2 * HD), BF), pltpu.VMEM((NSA_G, SEQ, 2 * HD), BF),
                        pltpu.VMEM((NSA_G, SEQ, HD), BF), pltpu.VMEM((NSA_G, SEQ, HD), BF)],
        compiler_params=_params(2),
        name="nsa_prompt",
    )(slopes, slope_rows, qb, gn, ks, vs, kw, vw, ck, cv, cmat)


def _mem_attend_kernel(q_ref, k_ref, v_ref, o_ref):
    for h in range(MEM_HEADS):
        sl = slice(h * HD, (h + 1) * HD)
        s = _dot_nt(q_ref[:, sl].astype(BF), k_ref[:, sl].astype(BF)) * SCALE
        m = jnp.max(s, axis=1, keepdims=True)
        p = jnp.exp(s - m)
        p = p / jnp.sum(p, axis=1, keepdims=True)
        o_ref[:, sl] = _dot(p.astype(BF), v_ref[:, sl].astype(BF)).astype(BF)


def _mem_attend_prompt(qm, mem_k, mem_v):
    tq = 512
    nq = SEQ // tq
    return pl.pallas_call(
        _mem_attend_kernel,
        grid=(BATCH, nq),
        in_specs=[pl.BlockSpec((tq, MEM_W), lambda b, qi: (b * nq + qi, 0)),
                  pl.BlockSpec((MEM_LEN, MEM_W), lambda b, qi: (b, 0)),
                  pl.BlockSpec((MEM_LEN, MEM_W), lambda b, qi: (b, 0))],
        out_specs=pl.BlockSpec((tq, MEM_W), lambda b, qi: (b * nq + qi, 0)),
        out_shape=jax.ShapeDtypeStruct((BATCH * SEQ, MEM_W), BF),
        compiler_params=_params(2),
        name="mem_attend_prompt",
    )(qm, mem_k, mem_v)


def _merge_kernel(x_ref, oa_ref, ob_ref, om_ref, wa_ref, wb_ref, wm_ref, ba_ref, bb_ref, bm_ref,
                  pa_ref, pb_ref, pm_ref, o_ref):
    x = x_ref[...]
    gate = lambda w_ref, b_ref: jax.nn.sigmoid(_dot_nt(x, w_ref[...]) + b_ref[...])
    merged = (gate(wa_ref, ba_ref) * _dot(oa_ref[...], pa_ref[...])
              + gate(wb_ref, bb_ref) * _dot(ob_ref[...], pb_ref[...])
              + gate(wm_ref, bm_ref) * _dot(om_ref[...], pm_ref[...]))
    o_ref[...] = merged.astype(BF)


def _merge(x_bf, oa, ob, om, w_gate, b_gate, pa, pb, pm, tm, tag):
    rows = oa.shape[0]
    tn = 512
    n_col = D_MODEL // tn
    row = lambda w: pl.BlockSpec((tm, w), lambda i, j: (i, 0))
    gate_w = lambda b: pl.BlockSpec((tn, D_MODEL), lambda i, j: (b * n_col + j, 0))
    gate_b = lambda b: pl.BlockSpec((1, tn), lambda i, j: (0, b * n_col + j))
    proj = lambda w: pl.BlockSpec((w, tn), lambda i, j: (0, j))
    return pl.pallas_call(
        _merge_kernel,
        grid=(rows // tm, n_col),
        in_specs=[row(D_MODEL), row(MOBA_W), row(NSA_W), row(MEM_W), gate_w(0), gate_w(1), gate_w(2),
                  gate_b(0), gate_b(1), gate_b(2), proj(MOBA_W), proj(NSA_W), proj(MEM_W)],
        out_specs=pl.BlockSpec((tm, tn), lambda i, j: (i, j)),
        out_shape=jax.ShapeDtypeStruct((rows, D_MODEL), BF),
        compiler_params=_params(2),
        name="branch_merge_" + tag,
    )(x_bf, oa, ob, om, w_gate, w_gate, w_gate, b_gate, b_gate, b_gate, pa, pb, pm)


def _layer_norm(x, g, b):
    mu = jnp.mean(x, axis=-1, keepdims=True)
    xc = x - mu
    var = jnp.mean(xc * xc, axis=-1, keepdims=True)
    return xc * lax.rsqrt(var + LN_EPS) * g + b


def _out_ln_kernel(x_ref, m_ref, wo_ref, g_ref, b_ref, h_ref, hb_ref):
    h = _layer_norm(DN_ALPHA * x_ref[...] + _dot(m_ref[...], wo_ref[...]), g_ref[...], b_ref[...])
    h_ref[...] = h
    hb_ref[...] = h.astype(BF)


def _out_ln(x, merged, wo, g, b, tm, tag):
    rows = x.shape[0]
    row = lambda: pl.BlockSpec((tm, D_MODEL), lambda i: (i, 0))
    vec = lambda: pl.BlockSpec((1, D_MODEL), lambda i: (0, 0))
    return pl.pallas_call(
        _out_ln_kernel,
        grid=(rows // tm,),
        in_specs=[row(), row(), pl.BlockSpec((D_MODEL, D_MODEL), lambda i: (0, 0)), vec(), vec()],
        out_specs=[row(), row()],
        out_shape=[jax.ShapeDtypeStruct((rows, D_MODEL), F32), jax.ShapeDtypeStruct((rows, D_MODEL), BF)],
        compiler_params=_params(1),
        name="out_proj_ln_" + tag,
    )(x, merged, wo, g, b)


TF = 512
N_TF = FFN_DIM // TF
HALO = 8
FFN_TM = 1024
FFN_VMEM_LIMIT = 58 * 1024 * 1024


def _ffn_kernel(hb_ref, halo_ref, h_ref, wu_ref, wg_ref, cw_ref, cb_ref, wd_ref, g_ref, b_ref, p2_ref, p1_ref,
                y_ref, u_ref, *, seq_rows):
    i = pl.program_id(0)
    f = pl.program_id(1)
    tm = hb_ref.shape[0]
    u = _dot(hb_ref[...], wu_ref[...])
    gate = _dot(hb_ref[...], wg_ref[...])
    if seq_rows:
        tiles_per_seq = seq_rows // tm
        live = jnp.where(i % tiles_per_seq == 0, 0.0, 1.0)
        u_halo = _dot(halo_ref[...], wu_ref[...]) * live
        row = lax.broadcasted_iota(jnp.int32, (HALO, TF), 0)
        prev1 = pltpu.roll(u, shift=1, axis=0)
        prev2 = pltpu.roll(u, shift=2, axis=0)
        head1 = jnp.where(row < 1, pltpu.roll(u_halo, shift=1, axis=0), prev1[:HALO])
        head2 = jnp.where(row < 2, pltpu.roll(u_halo, shift=2, axis=0), prev2[:HALO])
        prev1 = jnp.concatenate([head1, prev1[HALO:]], axis=0)
        prev2 = jnp.concatenate([head2, prev2[HALO:]], axis=0)
        u_ref[0] = u[tm - HALO:]
    else:
        prev2 = p2_ref[...]
        prev1 = p1_ref[...]
        u_ref[...] = u
    uc = cb_ref[...] + cw_ref[0:1, :] * prev2 + cw_ref[1:2, :] * prev1 + cw_ref[2:3, :] * u
    act = (_gelu_tanh(uc) * gate).astype(BF)

    @pl.when(f == 0)
    def _():
        y_ref[...] = jnp.zeros(y_ref.shape, F32)

    y_ref[...] += _dot(act, wd_ref[...])

    @pl.when(f == N_TF - 1)
    def _():
        y_ref[...] = _layer_norm(DN_ALPHA * h_ref[...] + y_ref[...], g_ref[...], b_ref[...])


def _ffn(h, h_bf, wu_bf, cw, cb, wd_bf, g, b, prev2, prev1, tm, seq_rows, tag):
    rows = h.shape[0]
    n_halo = tm // HALO
    if seq_rows:
        prev2 = prev1 = jnp.zeros((HALO, TF), F32)
        cs_spec = pl.BlockSpec((HALO, TF), lambda i, f: (0, 0))
        halo_spec = pl.BlockSpec((HALO, D_MODEL), lambda i, f: (jnp.maximum(i * n_halo - 1, 0), 0))
        u_shape = jax.ShapeDtypeStruct((rows // tm, HALO, FFN_DIM), F32)
        u_spec = pl.BlockSpec((1, HALO, TF), lambda i, f: (i, 0, f))
    else:
        cs_spec = pl.BlockSpec((tm, TF), lambda i, f: (i, f))
        halo_spec = pl.BlockSpec((HALO, D_MODEL), lambda i, f: (0, 0))
        u_shape = jax.ShapeDtypeStruct((rows, FFN_DIM), F32)
        u_spec = pl.BlockSpec((tm, TF), lambda i, f: (i, f))
    row = lambda: pl.BlockSpec((tm, D_MODEL), lambda i, f: (i, 0))
    row_once = lambda: pl.BlockSpec((tm, D_MODEL), lambda i, f: (i, 0), pipeline_mode=pl.Buffered(1))
    vec = lambda: pl.BlockSpec((1, D_MODEL), lambda i, f: (0, 0))
    return pl.pallas_call(
        functools.partial(_ffn_kernel, seq_rows=seq_rows),
        grid=(rows // tm, N_TF),
        in_specs=[row_once(), halo_spec, row_once(),
                  pl.BlockSpec((D_MODEL, TF), lambda i, f: (0, f)),
                  pl.BlockSpec((D_MODEL, TF), lambda i, f: (0, N_TF + f)),
                  pl.BlockSpec((3, TF), lambda i, f: (0, f)),
                  pl.BlockSpec((1, TF), lambda i, f: (0, f)),
                  pl.BlockSpec((TF, D_MODEL), lambda i, f: (f, 0)),
                  vec(), vec(), cs_spec, cs_spec],
        out_specs=[row(), u_spec],
        out_shape=[jax.ShapeDtypeStruct((rows, D_MODEL), F32), u_shape],
        compiler_params=pltpu.CompilerParams(dimension_semantics=("arbitrary", "arbitrary"),
                                             vmem_limit_bytes=FFN_VMEM_LIMIT),
        name="conv_ffn_" + tag,
    )(h_bf, h_bf, h, wu_bf, wu_bf, cw, cb, wd_bf, g, b, prev2, prev1)


def _rowdot(mat, vec):
    return jnp.sum(mat * vec, axis=1, keepdims=True)


MOBA_NB = PAST_LEN // MOBA_BLOCK
MOBA_PPB = MOBA_BLOCK // PAGE


def _moba_sample_probs_kernel(pt_ref, slopes_ref, q_ref, kn_ref, *refs):
    del pt_ref
    k_pages = refs[:N_PAGES]
    p_ref, stat_ref, ids_ref = refs[N_PAGES:N_PAGES + 3]
    nb, ppb = MOBA_NB, MOBA_PPB
    pos = lax.broadcasted_iota(jnp.int32, (1, PAST_LEN), 1)
    dist = (PAST_LEN - pos).astype(F32)
    blk = pos // MOBA_BLOCK
    row8 = lax.broadcasted_iota(jnp.int32, (8, HD), 0)
    lane8 = lax.broadcasted_iota(jnp.int32, (8, HD), 1)
    stats = jnp.zeros((8, HD), F32)
    ids = jnp.zeros((8, HD), F32)
    for h in range(MOBA_HEADS):
        sl = slice(h * HD, (h + 1) * HD)
        q = q_ref[0, :, sl]
        k_pg = [p[h] for p in k_pages]
        km = [sum(jnp.sum(k_pg[j * ppb + i], axis=0, keepdims=True) for i in range(ppb)) / MOBA_BLOCK
              for j in range(nb)]
        sc = [_rowdot(kmj, q) for kmj in km]
        allowed = jnp.zeros((1, PAST_LEN), F32)
        for j in range(nb):
            beats = [(sc[i] > sc[j]) | ((sc[i] == sc[j]) & (i < j)) for i in range(nb) if i != j]
            rank = sum(b.astype(F32) for b in beats)
            allowed = jnp.where(blk == j, jnp.where(rank < MOBA_TOPK, 1.0, 0.0), allowed)
            for c in range(MOBA_TOPK):
                ids = jnp.where((row8 == h) & (lane8 == c) & (rank == c), float(j), ids)
        q8 = jnp.concatenate([q, jnp.zeros((7, HD), F32)], axis=0)
        k_bf = jnp.concatenate([kp.astype(BF) for kp in k_pg], axis=0)
        s = _dot_nt(q8.astype(BF), k_bf) * SCALE - slopes_ref[h] * dist
        s = jnp.where(allowed > 0.5, s, -jnp.inf)
        s_own = _rowdot(q8, kn_ref[0, :, sl]) * SCALE
        m = jnp.maximum(jnp.max(s, axis=1, keepdims=True), s_own)
        p = jnp.exp(s - m)
        p_own = jnp.exp(s_own - m)
        den = jnp.sum(p, axis=1, keepdims=True) + p_own
        for j in range(nb):
            p_ref[0, h * nb + j:h * nb + j + 1, :] = p[0:1, j * MOBA_BLOCK:(j + 1) * MOBA_BLOCK]
        stats = jnp.where((row8 == h) & (lane8 == 0), p_own[0:1], jnp.where((row8 == h) & (lane8 == 1), den[0:1], stats))
    stat_ref[0] = stats
    ids_ref[0] = ids.astype(jnp.int32)


def _moba_sample_probs(page_table, slopes, q, k_new, k_pool):
    row = lambda: pl.BlockSpec((1, 1, MOBA_W), lambda n, pt: (n, 0, 0))
    page = lambda p: pl.BlockSpec((None, MOBA_HEADS, PAGE, HD), lambda n, pt: (pt[n, p], 0, 0, 0))
    tile = lambda r, w: pl.BlockSpec((1, r, w), lambda n, pt: (n, 0, 0))
    n_rows = MOBA_HEADS * MOBA_NB
    return pl.pallas_call(
        _moba_sample_probs_kernel,
        grid_spec=pltpu.PrefetchScalarGridSpec(
            num_scalar_prefetch=1, grid=(DEC_BATCH,),
            in_specs=[pl.BlockSpec(memory_space=pltpu.SMEM), row(), row()] + [page(p) for p in range(N_PAGES)],
            out_specs=[tile(n_rows, MOBA_BLOCK), tile(8, HD), tile(8, HD)]),
        out_shape=[jax.ShapeDtypeStruct((DEC_BATCH, n_rows, MOBA_BLOCK), F32),
                   jax.ShapeDtypeStruct((DEC_BATCH, 8, HD), F32), jax.ShapeDtypeStruct((DEC_BATCH, 8, HD), jnp.int32)],
        compiler_params=_params(1),
        name="moba_sample_probs",
    )(page_table, slopes, q, k_new, *([k_pool] * N_PAGES))


def _moba_sample_values_kernel(pt_ref, ids_ref, p_ref, stat_ref, vn_ref, v_hbm, o_ref, vbuf, sem):
    n = pl.program_id(0)

    def block_copies(sample, slot):
        copies = []
        for h in range(MOBA_HEADS):
            for c in range(MOBA_TOPK):
                j = ids_ref[(sample * MOBA_HEADS + h) * MOBA_TOPK + c]
                for i in range(MOBA_PPB):
                    dst = pl.ds(((h * MOBA_TOPK + c) * MOBA_PPB + i) * PAGE, PAGE)
                    copies.append(pltpu.make_async_copy(v_hbm.at[pt_ref[sample, j * MOBA_PPB + i], h],
                                                        vbuf.at[slot, dst, :], sem.at[slot]))
        return copies

    @pl.when(n == 0)
    def _():
        for cp in block_copies(0, 0):
            cp.start()

    @pl.when(n + 1 < DEC_BATCH)
    def _():
        for cp in block_copies(n + 1, (n + 1) % 2):
            cp.start()

    slot = n % 2
    for cp in block_copies(n, slot):
        cp.wait()

    for h in range(MOBA_HEADS):
        acc = jnp.zeros((8, HD), F32)
        for c in range(MOBA_TOPK):
            j = ids_ref[(n * MOBA_HEADS + h) * MOBA_TOPK + c]
            p_c = p_ref[0, pl.ds(h * MOBA_NB + j, 1), :]
            p8 = jnp.concatenate([p_c, jnp.zeros((7, MOBA_BLOCK), F32)], axis=0)
            v_c = vbuf[slot, (h * MOBA_TOPK + c) * MOBA_BLOCK:(h * MOBA_TOPK + c + 1) * MOBA_BLOCK, :]
            acc = acc + _dot(p8.astype(BF), v_c.astype(BF))
        p_own = stat_ref[0, h:h + 1, 0:1]
        den = stat_ref[0, h:h + 1, 1:2]
        o_ref[0, :, h * HD:(h + 1) * HD] = (acc[0:1] + p_own * vn_ref[0, :, h * HD:(h + 1) * HD]) / den


def _moba_sample_values(page_table, sel_ids, p, stats, v_new, v_pool):
    tile = lambda r, w: pl.BlockSpec((1, r, w), lambda n, pt, ids: (n, 0, 0))
    return pl.pallas_call(
        _moba_sample_values_kernel,
        grid_spec=pltpu.PrefetchScalarGridSpec(
            num_scalar_prefetch=2, grid=(DEC_BATCH,),
            in_specs=[tile(MOBA_HEADS * MOBA_NB, MOBA_BLOCK), tile(8, HD), tile(1, MOBA_W),
                      pl.BlockSpec(memory_space=pl.ANY)],
            out_specs=tile(1, MOBA_W),
            scratch_shapes=[pltpu.VMEM((2, MOBA_HEADS * MOBA_TOPK * MOBA_BLOCK, HD), F32),
                            pltpu.SemaphoreType.DMA((2,))]),
        out_shape=jax.ShapeDtypeStruct((DEC_BATCH, 1, MOBA_W), F32),
        compiler_params=_params(1),
        name="moba_sample_values",
    )(page_table, sel_ids, p, stats, v_new, v_pool)


NS = 8
ROWS_PER_PAGE = PAGE * NSA_G
N_SEL = NSA_TOPN - 1
N_CAND_S = PAST_LEN // SEL_BLOCK


def _nsa_sample_cmp_kernel(pt_ref, slopes_ref, kcn_ref, vcn_ref, q_ref, pe_ref, w1_ref, w2_ref, cmat_ref, tri_ref,
                           *refs):
    del pt_ref
    k_pages, v_pages = refs[:N_PAGES], refs[N_PAGES:2 * N_PAGES]
    ocmp_ref, idx_ref, stage_ref = refs[2 * N_PAGES:2 * N_PAGES + 3]
    s_in = pl.program_id(1)
    unit = N_CHUNK
    chunks_per_page = PAGE // CMP_STRIDE

    rg = CMP_STRIDE * NSA_G
    for which, pages in enumerate((k_pages, v_pages)):
        for pair in range(N_PAGES // 2):
            slabs = [jnp.swapaxes(pages[2 * pair + i][...].reshape(chunks_per_page, rg, HD), 0, 1) for i in range(2)]
            for g in range(NSA_G):
                base = pl.multiple_of(s_in * (NSA_G * unit) + g * unit, unit) + pair * 2 * chunks_per_page
                for r in range(CMP_STRIDE):
                    x = jnp.concatenate([slabs[0][NSA_G * r + g], slabs[1][NSA_G * r + g]], axis=0)
                    stage_ref[which, pl.ds(base, 2 * chunks_per_page), r * HD:(r + 1) * HD] = x.astype(BF)

    @pl.when(s_in == NS - 1)
    def _():
        n_unit = NS * NSA_G
        rows = n_unit * unit
        row = lax.broadcasted_iota(jnp.int32, (rows, HD), 0)
        toks = []
        for which, new_ref in enumerate((kcn_ref, vcn_ref)):
            w1 = w1_ref.at[which]
            acc = _dot(stage_ref[which], w1[...])
            top, bot = acc[:, :HD], acc[:, HD:]
            nxt = pltpu.roll(bot, shift=rows - 1, axis=0)
            new_rows = jnp.concatenate([new_ref[s:s + 1, g * HD:(g + 1) * HD]
                                        for s in range(NS) for g in range(NSA_G)], axis=0)
            b_new = _dot(new_rows.astype(BF), w1[0:HD, HD:])
            b_last = jnp.concatenate([jnp.broadcast_to(b_new[u:u + 1], (unit, HD)) for u in range(n_unit)], axis=0)
            c_row = jnp.zeros((1, HD), F32)
            for r in range(CMP_LEN):
                half, rr = divmod(r, CMP_STRIDE)
                c_row = c_row + _dot(pe_ref[which, :, r * HD:(r + 1) * HD],
                                     w1[rr * HD:(rr + 1) * HD, half * HD:(half + 1) * HD])
            pre = top + jnp.where(row % unit == unit - 1, b_last, nxt) + c_row
            toks.append(_dot(_gelu_tanh(pre).astype(BF), w2_ref[which]))
        ck_all, cv_all = toks

        lane = lax.broadcasted_iota(jnp.int32, (8, HD), 1)
        row8 = lax.broadcasted_iota(jnp.int32, (8, HD), 0)
        dist = PAST_LEN - (lane * CMP_STRIDE + CMP_LEN - 1)
        p_rows = []
        for s in range(NS):
            for g in range(NSA_G):
                u = s * NSA_G + g
                ck_hi, ck_lo = _split_bf16(ck_all[u * unit:(u + 1) * unit])
                cv_bf = cv_all[u * unit:(u + 1) * unit].astype(BF)
                q8 = jnp.concatenate([q_ref[s:s + 1, (g * NSA_R + r) * HD:(g * NSA_R + r + 1) * HD]
                                      for r in range(NSA_R)] + [jnp.zeros((8 - NSA_R, HD), F32)], axis=0)
                slope = jnp.where(row8[:, 0:1] == 0, slopes_ref[g * NSA_R],
                                  jnp.where(row8[:, 0:1] == 1, slopes_ref[g * NSA_R + 1], slopes_ref[g * NSA_R + 2]))
                o8, p8 = _cmp_attend(q8, ck_hi, ck_lo, cv_bf, slope, dist)
                ocmp_ref[s, g] = o8
                p_rows.append(jnp.sum(jnp.where(row8 < NSA_R, p8, 0.0), axis=0, keepdims=True))
        p_grp = jnp.concatenate(p_rows + [jnp.zeros((HD - n_unit, HD), F32)], axis=0)
        member = _select_blocks(p_grp, cmat_ref[...], jnp.full((EXT_MASK, HD), N_CAND_S, jnp.int32), N_CAND_S, N_SEL)
        member_pad = jnp.concatenate([member, jnp.zeros((HD - EXT_MASK, HD), F32)], axis=0)
        rank = _dot(tri_ref[...], member_pad.astype(BF))[:EXT_MASK]
        cand = lax.broadcasted_iota(jnp.int32, (EXT_MASK, HD), 0).astype(F32)
        picks = [jnp.sum(jnp.where((member > 0.5) & (rank == c), cand, 0.0), axis=0, keepdims=True)
                 for c in range(N_SEL)]
        idx_ref[0] = jnp.concatenate(picks + [jnp.zeros((1, HD), F32)], axis=0).astype(jnp.int32)


def _nsa_sample_cmp(page_table, slopes, kc_new, vc_new, q, pe_bf, w1_flat, w2_bf, cmat, tri, ck_pool, cv_pool):
    page = lambda p: pl.BlockSpec((ROWS_PER_PAGE, HD), lambda i, s, pt: (pt[i * NS + s, p], 0))
    full = lambda a: pl.BlockSpec(a.shape, lambda i, s, pt: (0,) * a.ndim)
    rows = lambda w: pl.BlockSpec((NS, w), lambda i, s, pt: (i, 0))
    out = lambda: pl.BlockSpec((NS, NSA_G, 8, HD), lambda i, s, pt: (i, 0, 0, 0))
    return pl.pallas_call(
        _nsa_sample_cmp_kernel,
        grid_spec=pltpu.PrefetchScalarGridSpec(
            num_scalar_prefetch=1, grid=(DEC_BATCH // NS, NS),
            in_specs=[pl.BlockSpec(memory_space=pltpu.SMEM), rows(KV_W), rows(KV_W), rows(NSA_W),
                      full(pe_bf), full(w1_flat), full(w2_bf), full(cmat), full(tri)]
            + [page(p) for p in range(N_PAGES)] * 2,
            out_specs=[out(), pl.BlockSpec((1, N_SEL + 1, HD), lambda i, s, pt: (i, 0, 0))],
            scratch_shapes=[pltpu.VMEM((2, NS * NSA_G * N_CHUNK, CMP_STRIDE * HD), BF)]),
        out_shape=[jax.ShapeDtypeStruct((DEC_BATCH, NSA_G, 8, HD), F32),
                   jax.ShapeDtypeStruct((DEC_BATCH // NS, N_SEL + 1, HD), jnp.int32)],
        compiler_params=_params(2),
        name="nsa_sample_cmp",
    )(page_table, slopes, kc_new, vc_new, q, pe_bf, w1_flat, w2_bf, cmat, tri,
      *([ck_pool] * N_PAGES), *([cv_pool] * N_PAGES))


def _attend_rows(q8, slope, k, v, dist, k_own, v_own):
    s = _dot_nt(q8.astype(BF), k.astype(BF)) * SCALE - slope * dist
    s_own = _rowdot(q8, k_own) * SCALE
    m = jnp.maximum(jnp.max(s, axis=1, keepdims=True), s_own)
    p = jnp.exp(s - m)
    p_own = jnp.exp(s_own - m)
    den = jnp.sum(p, axis=1, keepdims=True) + p_own
    return (_dot(p.astype(BF), v.astype(BF)) + p_own * v_own) / den


BLK_ROWS = SEL_BLOCK * NSA_G


def _nsa_sample_attend_kernel(pt_ref, ix_ref, slopes_ref, q_ref, gn_ref, ksn_ref, vsn_ref, kwn_ref, vwn_ref,
                              ocmp_ref, wk_ref, wv_ref, sk_hbm, sv_hbm, o_ref, wko_ref, wvo_ref, kbuf, vbuf, sem):
    n = pl.program_id(0)
    halves = PAGE // SEL_BLOCK

    def block_copies(sample, slot):
        copies = []
        for g in range(NSA_G):
            for c in range(N_SEL):
                j = ix_ref[(sample * NSA_G + g) * N_SEL + c]
                row0 = pl.multiple_of((pt_ref[sample, j // halves] * halves + j % halves) * BLK_ROWS, BLK_ROWS)
                dst = pl.ds((g * N_SEL + c) * BLK_ROWS, BLK_ROWS)
                for pool, buf in ((sk_hbm, kbuf), (sv_hbm, vbuf)):
                    copies.append(pltpu.make_async_copy(pool.at[pl.ds(row0, BLK_ROWS), :], buf.at[slot, dst, :],
                                                        sem.at[slot]))
        return copies

    @pl.when(n == 0)
    def _():
        for cp in block_copies(0, 0):
            cp.start()

    @pl.when(n + 1 < DEC_BATCH)
    def _():
        for cp in block_copies(n + 1, (n + 1) % 2):
            cp.start()

    slot = n % 2
    for cp in block_copies(n, slot):
        cp.wait()

    wl = wk_ref.shape[0] // NSA_G
    off = lax.broadcasted_iota(jnp.int32, (1, SEL_BLOCK), 1)
    dist_w = (wl - lax.broadcasted_iota(jnp.int32, (1, wl), 1)).astype(F32)
    row8 = lax.broadcasted_iota(jnp.int32, (8, 1), 0)
    gates = jax.nn.sigmoid(gn_ref[0])
    for g in range(NSA_G):
        gsl = slice(g * HD, (g + 1) * HD)
        heads = [g * NSA_R + r for r in range(NSA_R)]
        q8 = jnp.concatenate([q_ref[0, :, hd * HD:(hd + 1) * HD] for hd in heads]
                             + [jnp.zeros((8 - NSA_R, HD), F32)], axis=0)
        slope = jnp.where(row8 == 0, slopes_ref[heads[0]],
                          jnp.where(row8 == 1, slopes_ref[heads[1]], slopes_ref[heads[2]]))
        sel_rows = pl.ds(g * N_SEL * BLK_ROWS + g, N_SEL * SEL_BLOCK, stride=NSA_G)
        pos = jnp.concatenate([ix_ref[(n * NSA_G + g) * N_SEL + c] * SEL_BLOCK + off for c in range(N_SEL)], axis=1)
        o_sel = _attend_rows(q8, slope, kbuf[slot, sel_rows, :], vbuf[slot, sel_rows, :],
                             (PAST_LEN - pos).astype(F32), ksn_ref[0, :, gsl], vsn_ref[0, :, gsl])
        o_win = _attend_rows(q8, slope, wk_ref[pl.ds(g, wl, stride=NSA_G), :], wv_ref[pl.ds(g, wl, stride=NSA_G), :],
                             dist_w, kwn_ref[0, :, gsl], vwn_ref[0, :, gsl])
        for r, hd in enumerate(heads):
            o_ref[0, :, hd * HD:(hd + 1) * HD] = (
                gates[:, 3 * hd:3 * hd + 1] * ocmp_ref[0, g, r:r + 1, :]
                + gates[:, 3 * hd + 1:3 * hd + 2] * o_sel[r:r + 1] + gates[:, 3 * hd + 2:3 * hd + 3] * o_win[r:r + 1])

    for cache, new, out in ((wk_ref, kwn_ref, wko_ref), (wv_ref, vwn_ref, wvo_ref)):
        out[...] = pltpu.roll(cache[...], shift=(wl - 1) * NSA_G, axis=0)
        for g in range(NSA_G):
            out[pl.ds((wl - 1) * NSA_G + g, 1), :] = new[0, :, g * HD:(g + 1) * HD]


def _nsa_sample_attend(page_table, sel_idx, slopes, q, gn, ks_new, vs_new, kw_new, vw_new, o_cmp,
                       win_k, win_v, sk_pool, sv_pool):
    row = lambda w: pl.BlockSpec((1, 1, w), lambda n, pt, ix: (n, 0, 0))
    win_rows = win_k.shape[0] // DEC_BATCH
    win = lambda: pl.BlockSpec((win_rows, HD), lambda n, pt, ix: (n, 0))
    any_spec = pl.BlockSpec(memory_space=pl.ANY)
    buf = pltpu.VMEM((2, NSA_G * N_SEL * BLK_ROWS, HD), F32)
    return pl.pallas_call(
        _nsa_sample_attend_kernel,
        grid_spec=pltpu.PrefetchScalarGridSpec(
            num_scalar_prefetch=2, grid=(DEC_BATCH,),
            in_specs=[pl.BlockSpec(memory_space=pltpu.SMEM), row(NSA_W), row(HD), row(KV_W), row(KV_W), row(KV_W),
                      row(KV_W), pl.BlockSpec((1, NSA_G, 8, HD), lambda n, pt, ix: (n, 0, 0, 0)), win(), win(),
                      any_spec, any_spec],
            out_specs=[row(NSA_W), win(), win()],
            scratch_shapes=[buf, buf, pltpu.SemaphoreType.DMA((2,))]),
        out_shape=[jax.ShapeDtypeStruct((DEC_BATCH, 1, NSA_W), F32), jax.ShapeDtypeStruct(win_k.shape, F32),
                   jax.ShapeDtypeStruct(win_v.shape, F32)],
        compiler_params=_params(1),
        name="nsa_sample_attend",
    )(page_table, sel_idx, slopes, q, gn, ks_new, vs_new, kw_new, vw_new, o_cmp, win_k, win_v, sk_pool, sv_pool)


def _mem_sample_kernel(q_ref, k_ref, v_ref, o_ref):
    for h in range(MEM_HEADS):
        sl = slice(h * HD, (h + 1) * HD)
        q = q_ref[0, :, sl]
        k = k_ref[pl.ds(h, MEM_LEN, stride=MEM_HEADS), :]
        v = v_ref[pl.ds(h, MEM_LEN, stride=MEM_HEADS), :]
        s = _rowdot(k, q) * SCALE
        p = jnp.exp(s - jnp.max(s, axis=0, keepdims=True))
        o_ref[0, :, sl] = jnp.sum(p * v, axis=0, keepdims=True) / jnp.sum(p, axis=0, keepdims=True)


def _mem_sample(q, mem_k, mem_v):
    rows = MEM_LEN * MEM_HEADS
    row = lambda: pl.BlockSpec((1, 1, MEM_W), lambda n: (n, 0, 0))
    kv = lambda: pl.BlockSpec((rows, HD), lambda n: (n, 0))
    return pl.pallas_call(
        _mem_sample_kernel,
        grid=(DEC_BATCH,),
        in_specs=[row(), kv(), kv()],
        out_specs=row(),
        out_shape=jax.ShapeDtypeStruct((DEC_BATCH, 1, MEM_W), F32),
        compiler_params=_params(1),
        name="mem_attend_sample",
    )(q, mem_k, mem_v)


def _project_all(x_bf, w, tm, paged, tag):
    kv_kind = "pages" if paged else "flat"
    qa, ka, va, qb = _proj(x_bf, w["wa"], w["ba"], MOBA_W, ["flat", kv_kind, kv_kind, "flat"], tm,
                           "proj_moba_q_" + tag, True)
    kc, vc, ks, vs, kw, vw = _proj(x_bf, w["wb"], w["bb"], KV_W, ["flat"] * 6, tm, "proj_nsa_kv_" + tag, True)
    (qm,) = _proj(x_bf, w["wm"], w["bm"], MEM_W, ["flat"], tm, "proj_mem_q_" + tag, True)
    (gn,) = _proj(x_bf, w["wg"], w["bg"], HD, ["flat"], tm, "proj_nsa_gate_" + tag, True)
    return qa, ka, va, qb, kc, vc, ks, vs, kw, vw, qm, gn


def _strict_lower_ones():
    return jnp.asarray(np.tril(np.ones((HD, HD), np.float32), -1), BF)


def kernel(x_prompt, x_sample, cache_mem_k, cache_mem_v, cache_moba_k, cache_moba_v, cache_nsa_cmp_k, cache_nsa_cmp_v, cache_nsa_sel_k, cache_nsa_sel_v, cache_nsa_win_k, cache_nsa_win_v, cache_ffn_conv, page_table, mem_prompt, w_in, b_in, w_mem_kv, cmp_pe, cmp_w1, cmp_w2, p_moba, p_nsa, p_mem, w_o, ln1_g, ln1_b, w_up, conv_w, conv_b, w_down, ln2_g, ln2_b):
    slopes = _alibi_slopes(MOBA_HEADS)
    w_bf = w_in[0].T.astype(BF)
    b0 = b_in[0][None, :]
    c0, c1, c2, c3 = 4 * MOBA_W, 4 * MOBA_W + 6 * KV_W, 4 * MOBA_W + 6 * KV_W + N_GATE, 4 * MOBA_W + 6 * KV_W + N_GATE + MEM_W
    w = {"wa": w_bf[:c0], "ba": b0[:, :c0], "wb": w_bf[c0:c1], "bb": b0[:, c0:c1],
         "wg": jnp.pad(w_bf[c1:c2], ((0, HD - N_GATE), (0, 0))), "bg": jnp.pad(b0[:, c1:c2], ((0, 0), (0, HD - N_GATE))),
         "wm": w_bf[c2:c3], "bm": b0[:, c2:c3], "wd": w_bf[c3:], "bd": b0[:, c3:]}
    pe_bf, w1_bf, w2_bf = _pack_cmp_weights(cmp_pe[0], cmp_w1[0], cmp_w2[0])
    pa_bf, pb_bf, pm_bf, wo_bf = p_moba[0].astype(BF), p_nsa[0].astype(BF), p_mem[0].astype(BF), w_o[0].astype(BF)
    wu_bf, wd_bf = w_up[0].astype(BF), w_down[0].astype(BF)

    xp = x_prompt.reshape(BATCH * SEQ, D_MODEL)
    xp_bf = xp.astype(BF)
    qa, ka, va, qb, kc, vc, ks, vs, kw, vw, qm, gn = _project_all(xp_bf, w, 1024, True, "prompt")
    mem_k, mem_v = _proj(mem_prompt.reshape(BATCH * MEM_LEN, D_MODEL).astype(BF), w_mem_kv[0].astype(BF),
                         jnp.zeros((1, 2 * MEM_W), F32), MEM_W, ["flat", "flat"], 512, "proj_mem_kv")
    slope_rows = _slope_rows(MOBA_HEADS)
    o_a = _moba_prompt(slope_rows, qa, ka, va)
    ck, cv = _compress_prompt(kc, vc, pe_bf, w1_bf, w2_bf)
    cmat = _cmp_to_sel_matrix(N_CHUNK - 1, SEQ // SEL_BLOCK - 1)
    o_b = _nsa_prompt(slopes, slope_rows, qb, gn, ks, vs, kw, vw, ck, cv, cmat)
    o_m = _mem_attend_prompt(qm, mem_k, mem_v)
    merged = _merge(xp_bf, o_a, o_b, o_m, w["wd"], w["bd"], pa_bf, pb_bf, pm_bf, 1024, "prompt")
    h, h_bf = _out_ln(xp, merged, wo_bf, ln1_g, ln1_b, 512, "prompt")
    y_p, u_tail = _ffn(h, h_bf, wu_bf, conv_w[0], conv_b, wd_bf, ln2_g, ln2_b, None, None, FFN_TM, SEQ, "prompt")

    pages = lambda a: a.reshape(1, BATCH, SEQ // PAGE, PAGE, NSA_G, HD)
    head_pages = lambda a: a.transpose(0, 1, 3, 2, 4)[None]
    last = lambda a: a.reshape(BATCH, SEQ, NSA_G, HD)[None, :, SEQ - WINDOW:]
    tiles_per_seq = SEQ // FFN_TM
    conv_p = u_tail.reshape(BATCH, tiles_per_seq, HALO, FFN_DIM)[:, -1, HALO - 2:][None]
    prompt_out = (y_p.reshape(BATCH, SEQ, D_MODEL),
                  mem_k.reshape(1, BATCH, MEM_LEN, MEM_HEADS, HD), mem_v.reshape(1, BATCH, MEM_LEN, MEM_HEADS, HD),
                  head_pages(ka), head_pages(va), pages(kc), pages(vc), pages(ks), pages(vs), last(kw), last(vw), conv_p)

    n = DEC_BATCH
    xs = x_sample.reshape(n, D_MODEL)
    xs_bf = xs.astype(BF)
    qa, ka, va, qb, kc, vc, ks, vs, kw, vw, qm, gn = _project_all(xs_bf, w, n, False, "sample")
    r3 = lambda a: a.reshape(n, 1, a.shape[-1])
    moba_pool = lambda c: c[0].transpose(0, 2, 1, 3)
    rows2d = lambda c: c.reshape(-1, HD)
    p_a, stat_a, ids_a = _moba_sample_probs(page_table, slopes, r3(qa), r3(ka), moba_pool(cache_moba_k))
    o_a = _moba_sample_values(page_table, ids_a[:, :MOBA_HEADS, :MOBA_TOPK].reshape(-1), p_a, stat_a, r3(va),
                              moba_pool(cache_moba_v))
    w1_flat = w1_bf.reshape(2, CMP_STRIDE * HD, 2 * HD)
    cmat_s = _cmp_to_sel_matrix(N_CHUNK, N_CAND_S)
    o_cmp, sel = _nsa_sample_cmp(page_table, slopes, kc, vc, qb, pe_bf, w1_flat, w2_bf, cmat_s, _strict_lower_ones(),
                                 rows2d(cache_nsa_cmp_k), rows2d(cache_nsa_cmp_v))
    sel_idx = sel[:, :N_SEL, :NS * NSA_G].transpose(0, 2, 1).reshape(-1)
    o_b, win_k_new, win_v_new = _nsa_sample_attend(
        page_table, sel_idx, slopes, r3(qb), r3(gn), r3(ks), r3(vs), r3(kw), r3(vw), o_cmp,
        rows2d(cache_nsa_win_k), rows2d(cache_nsa_win_v), rows2d(cache_nsa_sel_k), rows2d(cache_nsa_sel_v))
    o_m = _mem_sample(r3(qm), rows2d(cache_mem_k), rows2d(cache_mem_v))
    flat_bf = lambda a: a.reshape(n, a.shape[-1]).astype(BF)
    merged = _merge(xs_bf, flat_bf(o_a), flat_bf(o_b), flat_bf(o_m), w["wd"], w["bd"], pa_bf, pb_bf, pm_bf, n,
                    "sample")
    h, h_bf = _out_ln(xs, merged, wo_bf, ln1_g, ln1_b, n, "sample")
    conv_old = cache_ffn_conv[0]
    y_s, u_new = _ffn(h, h_bf, wu_bf, conv_w[0], conv_b, wd_bf, ln2_g, ln2_b, conv_old[:, 0], conv_old[:, 1], n, 0,
                      "sample")

    heads = lambda a, nh: a.reshape(1, n, 1, nh, HD)
    sample_out = (heads(ka, MOBA_HEADS), heads(va, MOBA_HEADS), heads(kc, NSA_G), heads(vc, NSA_G),
                  heads(ks, NSA_G), heads(vs, NSA_G), win_k_new.reshape(cache_nsa_win_k.shape),
                  win_v_new.reshape(cache_nsa_win_v.shape),
                  jnp.stack([conv_old[:, 1], u_new], axis=1)[None])
    return (prompt_out[0], y_s.reshape(n, 1, D_MODEL)) + prompt_out[1:] + sample_out
```

```python
import functools

import numpy as np
import jax
import jax.numpy as jnp
from jax import lax
from jax.experimental import pallas as pl
from jax.experimental.pallas import tpu as pltpu

D_MODEL = 2048
BATCH = 4
SEQ = 2048
DEC_BATCH = 128
PAST_LEN = 2048
PAGE = 128
N_PAGES = PAST_LEN // PAGE
HD = 128
MOBA_HEADS = 6
MOBA_BLOCK = 256
MOBA_TOPK = 3
NSA_HEADS = 6
NSA_G = 2
NSA_R = 3
CMP_LEN = 32
CMP_STRIDE = 16
SEL_BLOCK = 64
NSA_TOPN = 16
WINDOW = 512
MEM_HEADS = 4
MEM_LEN = 256
FFN_DIM = 5632
DN_ALPHA = 2.0 ** 0.25
LN_EPS = 1e-5
SCALE = HD ** -0.5
MOBA_W = MOBA_HEADS * HD
NSA_W = NSA_HEADS * HD
KV_W = NSA_G * HD
MEM_W = MEM_HEADS * HD
N_GATE = NSA_HEADS * 3

F32 = jnp.float32
BF = jnp.bfloat16
NEG_BIG = -(2.0 ** 100)
VMEM_LIMIT = 48 * 1024 * 1024
TQ = 256


def _dot(a, b):
    return jnp.dot(a, b, preferred_element_type=F32)


def _dot_nt(a, b):
    return lax.dot_general(a, b, (((1,), (1,)), ((), ())), preferred_element_type=F32)


def _split_bf16(a):
    hi = a.astype(BF)
    lo = (a - hi.astype(F32)).astype(BF)
    return hi, lo


def _alibi_slopes(n):
    return jnp.asarray(2.0 ** (-8.0 * np.arange(1, n + 1) / n), F32)


def _gelu_tanh(x):
    return 0.5 * x * (1.0 + jnp.tanh(np.sqrt(2.0 / np.pi).astype(np.float32) * (x + 0.044715 * (x * x * x))))


def _params(n_axes):
    return pltpu.CompilerParams(dimension_semantics=("arbitrary",) * n_axes, vmem_limit_bytes=VMEM_LIMIT)


def _proj_kernel(x_ref, w_ref, b_ref, *out_refs, kinds, w_rows):
    j = pl.program_id(1)
    acc = (_dot_nt if w_rows else _dot)(x_ref[...], w_ref[...]) + b_ref[...]
    for idx, (kind, o_ref) in enumerate(zip(kinds, out_refs)):
        @pl.when(j == idx)
        def _(kind=kind, o_ref=o_ref):
            if kind == "flat":
                o_ref[...] = acc
            else:
                for p in range(o_ref.shape[1]):
                    for h in range(o_ref.shape[2]):
                        o_ref[0, p, h] = acc[p * PAGE:(p + 1) * PAGE, h * HD:(h + 1) * HD]


def _proj(x_bf, w_bf, b, tn, kinds, tm, name, w_rows=False):
    rows, k = x_bf.shape
    n_out = len(kinds)
    assert w_bf.shape == ((n_out * tn, k) if w_rows else (k, n_out * tn)) and rows % tm == 0
    w_spec = pl.BlockSpec((tn, k), lambda i, j: (j, 0)) if w_rows else pl.BlockSpec((k, tn), lambda i, j: (0, j))
    tiles_per_batch = SEQ // tm if tm <= SEQ else 1
    out_shape, out_specs = [], []
    for idx, kind in enumerate(kinds):
        if kind == "flat":
            out_shape.append(jax.ShapeDtypeStruct((rows, tn), F32))
            out_specs.append(pl.BlockSpec((tm, tn), lambda i, j: (i, 0)))
        else:
            heads = tn // HD
            out_shape.append(jax.ShapeDtypeStruct((rows // SEQ, SEQ // PAGE, heads, PAGE, HD), F32))
            out_specs.append(pl.BlockSpec((1, tm // PAGE, heads, PAGE, HD),
                                          lambda i, j: (i // tiles_per_batch, i % tiles_per_batch, 0, 0, 0)))
    return pl.pallas_call(
        functools.partial(_proj_kernel, kinds=tuple(kinds), w_rows=w_rows),
        grid=(rows // tm, n_out),
        in_specs=[pl.BlockSpec((tm, k), lambda i, j: (i, 0)), w_spec,
                  pl.BlockSpec((1, tn), lambda i, j: (0, j))],
        out_specs=out_specs,
        out_shape=out_shape,
        compiler_params=_params(2),
        name=name,
    )(x_bf, w_bf, b)


def _topk_member(v, cand, n_cand, k):
    member = jnp.zeros(v.shape, F32)
    for j in range(n_cand):
        vj = v[j:j + 1, :]
        beats = (v > vj) | ((v == vj) & (cand < j))
        cnt = jnp.sum(beats.astype(F32), axis=0, keepdims=True)
        member = jnp.where((cand == j) & (cnt < k), 1.0, member)
    return member


def _to_rows(member_t):
    pad = jnp.zeros((HD - member_t.shape[0], member_t.shape[1]), F32)
    return jnp.transpose(jnp.concatenate([member_t, pad], axis=0))


EXT_MASK = 32
N_PIECE = 3
POS_SPLIT = 64
EXP2_SCALE = float(SCALE * np.log2(np.e))


def _slope_rows(n_heads):
    slope = np.asarray(2.0 ** (-8.0 * np.arange(1, n_heads + 1) / n_heads), np.float32) / np.float32(SCALE)
    rows = np.zeros((8, HD), np.float32)
    rest = slope.astype(np.float32)
    for i in range(N_PIECE):
        piece = rest.astype(BF).astype(np.float32)
        rows[:n_heads, EXT_MASK + i] = piece * POS_SPLIT
        rows[:n_heads, EXT_MASK + N_PIECE + i] = piece
        rest = rest - piece
    return jnp.asarray(rows)


def _key_ext(n_keys, block):
    key = lax.broadcasted_iota(jnp.int32, (n_keys, HD), 0)
    lane = lax.broadcasted_iota(jnp.int32, (n_keys, HD), 1)
    hot = jnp.where(key // block == lane, 1.0, 0.0)
    pos = jnp.where(lane < EXT_MASK + N_PIECE, (key // POS_SPLIT).astype(F32), (key % POS_SPLIT).astype(F32))
    return jnp.where(lane < EXT_MASK, hot, jnp.where(lane < EXT_MASK + 2 * N_PIECE, pos, 0.0)).astype(BF)


def _query_ext(allowed, slope_row, lane):
    mask = 0.0 if allowed is None else jnp.where(allowed, 0.0, NEG_BIG)
    return jnp.where(lane < EXT_MASK, mask, slope_row).astype(BF)


def _flash_init(qx, kx, v_bf, keep):
    s = jnp.where(keep, _dot_nt(qx, kx), NEG_BIG)
    m = jnp.max(s, axis=1, keepdims=True)
    p = jnp.exp2((s - m) * EXP2_SCALE)
    return m, jnp.sum(p, axis=1, keepdims=True), _dot(p.astype(BF), v_bf)


def _flash_step(state, qx, kx, v_bf, keep=None):
    m_old, l_old, acc_old = state
    s = _dot_nt(qx, kx)
    if keep is not None:
        s = jnp.where(keep, s, NEG_BIG)
    m_new = jnp.maximum(m_old, jnp.max(s, axis=1, keepdims=True))
    alpha = jnp.exp2((m_old - m_new) * EXP2_SCALE)
    p = jnp.exp2((s - m_new) * EXP2_SCALE)
    return (m_new, alpha * l_old + jnp.sum(p, axis=1, keepdims=True),
            alpha * acc_old + _dot(p.astype(BF), v_bf))


MOBA_HB = 6


def _moba_prompt_kernel(srow_ref, q_ref, k_ref, v_ref, o_ref, kx_ref, vb_ref, kmh_ref, kml_ref):
    hg = pl.program_id(1)
    qi = pl.program_id(2)
    nb = SEQ // MOBA_BLOCK

    @pl.when(qi == 0)
    def _():
        ext = _key_ext(SEQ, MOBA_BLOCK)
        for hh in range(MOBA_HB):
            k = k_ref[:, hh].reshape(SEQ, HD)
            kx_ref[hh, :, :HD] = k.astype(BF)
            kx_ref[hh, :, HD:] = ext
            vb_ref[hh] = v_ref[:, hh].reshape(SEQ, HD).astype(BF)
            km = jnp.sum(k.reshape(nb, MOBA_BLOCK, HD), axis=1) / MOBA_BLOCK
            hi, lo = _split_bf16(jnp.concatenate([km, jnp.zeros((HD - nb, HD), F32)], axis=0))
            kmh_ref[hh] = hi
            kml_ref[hh] = lo

    lane = lax.broadcasted_iota(jnp.int32, (TQ, HD), 1)
    cand = lax.broadcasted_iota(jnp.int32, (nb, TQ), 0)
    rr = lax.broadcasted_iota(jnp.int32, (TQ, MOBA_BLOCK), 0)
    cc = lax.broadcasted_iota(jnp.int32, (TQ, MOBA_BLOCK), 1)
    own = pl.multiple_of(qi * MOBA_BLOCK, MOBA_BLOCK)
    qx, states = [], []
    for hh in range(MOBA_HB):
        q = q_ref[:, hh * HD:(hh + 1) * HD]
        qh, ql = _split_bf16(q)
        sc = (_dot_nt(kmh_ref[hh], qh) + _dot_nt(kml_ref[hh], qh) + _dot_nt(kmh_ref[hh], ql))[:nb]
        sc = jnp.where(cand < qi, sc, -jnp.inf)
        member = _topk_member(sc, cand, nb - 1, MOBA_TOPK)
        allowed = (_to_rows(member) > 0.5) | (lane == qi)
        ext = _query_ext(allowed, srow_ref[pl.ds(hg * MOBA_HB + hh, 1), :], lane)
        qx.append(jnp.concatenate([qh, ext], axis=1))
        states.append(_flash_init(qx[hh], kx_ref[hh, pl.ds(own, MOBA_BLOCK), :],
                                  vb_ref[hh, pl.ds(own, MOBA_BLOCK), :], cc <= rr))

    def body(c, states):
        rows = pl.ds(pl.multiple_of(c * MOBA_BLOCK, MOBA_BLOCK), MOBA_BLOCK)
        return tuple(_flash_step(states[hh], qx[hh], kx_ref[hh, rows, :], vb_ref[hh, rows, :])
                     for hh in range(MOBA_HB))

    states = lax.fori_loop(0, qi, body, tuple(states))
    for hh in range(MOBA_HB):
        m, l, acc = states[hh]
        o_ref[:, hh * HD:(hh + 1) * HD] = (acc / l).astype(BF)


def _moba_prompt(slope_rows, q, k_pages, v_pages):
    nq = SEQ // TQ
    kv_spec = pl.BlockSpec((None, SEQ // PAGE, MOBA_HB, PAGE, HD), lambda b, h, qi: (b, 0, h, 0, 0))
    q_spec = pl.BlockSpec((TQ, MOBA_HB * HD), lambda b, h, qi: (b * nq + qi, h))
    return pl.pallas_call(
        _moba_prompt_kernel,
        grid=(BATCH, MOBA_HEADS // MOBA_HB, nq),
        in_specs=[pl.BlockSpec((8, HD), lambda b, h, qi: (0, 0)), q_spec, kv_spec, kv_spec],
        out_specs=q_spec,
        out_shape=jax.ShapeDtypeStruct((BATCH * SEQ, MOBA_W), BF),
        scratch_shapes=[pltpu.VMEM((MOBA_HB, SEQ, 2 * HD), BF), pltpu.VMEM((MOBA_HB, SEQ, HD), BF),
                        pltpu.VMEM((MOBA_HB, HD, HD), BF), pltpu.VMEM((MOBA_HB, HD, HD), BF)],
        compiler_params=_params(3),
        name="moba_prompt",
    )(slope_rows, q, k_pages, v_pages)


N_CHUNK = SEQ // CMP_STRIDE


def _compress_tokens(get_x, w1_ref, w2_ref, c_row, b_last):
    acc = jnp.zeros((N_CHUNK, 2 * HD), F32)
    for r in range(CMP_STRIDE):
        acc = acc + _dot(get_x(r).astype(BF), w1_ref[r])
    top, bot = acc[:, :HD], acc[:, HD:]
    nxt = pltpu.roll(bot, shift=N_CHUNK - 1, axis=0)
    row = lax.broadcasted_iota(jnp.int32, (N_CHUNK, HD), 0)
    pre = top + jnp.where(row == N_CHUNK - 1, b_last, nxt) + c_row
    return _dot(_gelu_tanh(pre).astype(BF), w2_ref[...])


def _compress_prompt_kernel(kc_ref, vc_ref, pe_ref, w1_ref, w2_ref, ck_ref, cv_ref):
    for which, (src, dst) in enumerate(((kc_ref, ck_ref), (vc_ref, cv_ref))):
        w1 = w1_ref.at[which]
        c_row = _pe_term(pe_ref, w1, which)
        get_x = lambda r, src=src: src[pl.ds(r, N_CHUNK, stride=CMP_STRIDE), :]
        dst[0, 0] = _compress_tokens(get_x, w1, w2_ref.at[which], c_row, jnp.zeros((1, HD), F32))


def _pe_term(pe_ref, w1, which):
    c_row = jnp.zeros((1, HD), F32)
    for r in range(CMP_LEN):
        half, rr = divmod(r, CMP_STRIDE)
        c_row = c_row + _dot(pe_ref[which, :, r * HD:(r + 1) * HD], w1[rr, :, half * HD:(half + 1) * HD])
    return c_row


def _pack_cmp_weights(cmp_pe, cmp_w1, cmp_w2):
    w1 = cmp_w1.reshape(2, 2, CMP_STRIDE, HD, HD).transpose(0, 2, 3, 1, 4).reshape(2, CMP_STRIDE, HD, 2 * HD)
    return cmp_pe.reshape(2, 1, CMP_LEN * HD).astype(BF), w1.astype(BF), cmp_w2.astype(BF)


def _compress_prompt(kc, vc, pe_bf, w1_bf, w2_bf):
    full = lambda a: pl.BlockSpec(a.shape, lambda b, g: (0,) * a.ndim)
    return pl.pallas_call(
        _compress_prompt_kernel,
        grid=(BATCH, NSA_G),
        in_specs=[pl.BlockSpec((SEQ, HD), lambda b, g: (b, g)), pl.BlockSpec((SEQ, HD), lambda b, g: (b, g)),
                  full(pe_bf), full(w1_bf), full(w2_bf)],
        out_specs=[pl.BlockSpec((1, 1, N_CHUNK, HD), lambda b, g: (b, g, 0, 0))] * 2,
        out_shape=[jax.ShapeDtypeStruct((BATCH, NSA_G, N_CHUNK, HD), F32)] * 2,
        compiler_params=_params(2),
        name="nsa_compress_prompt",
    )(kc, vc, pe_bf, w1_bf, w2_bf)


def _cmp_to_sel_matrix(n_cmp, n_blocks):
    start = np.arange(n_cmp)[:, None] * CMP_STRIDE
    j0 = np.arange(n_blocks)[None, :] * SEL_BLOCK
    ov = np.minimum(start + CMP_LEN, j0 + SEL_BLOCK) - np.maximum(start, j0)
    m = np.zeros((HD, HD), np.float32)
    m[:n_blocks, :n_cmp] = (np.clip(ov, 0, None) / CMP_LEN).T
    return jnp.asarray(m, BF)


def _cmp_attend(q, ck_hi, ck_lo, cv_bf, slope, dist):
    qh, ql = _split_bf16(q)
    s = (_dot_nt(qh, ck_hi) + _dot_nt(qh, ck_lo) + _dot_nt(ql, ck_hi)) * SCALE - slope * dist.astype(F32)
    s = jnp.where(dist >= 0, s, -jnp.inf)
    m = jnp.max(s, axis=1, keepdims=True)
    m = jnp.where(m > -jnp.inf, m, 0.0)
    p = jnp.exp(s - m)
    p = p / jnp.maximum(jnp.sum(p, axis=1, keepdims=True), 1e-30)
    return _dot(p.astype(BF), cv_bf), p


def _select_blocks(p_grp, cmat_bf, cur, n_cand, k):
    ph, plo = _split_bf16(p_grp)
    imp = (_dot_nt(cmat_bf, ph) + _dot_nt(cmat_bf, plo))[:EXT_MASK]
    cand = lax.broadcasted_iota(jnp.int32, imp.shape, 0)
    valid = (cand < cur) & (cand < n_cand)
    forced = valid & ((cand == 0) | (cand == cur - 1))
    v = jnp.where(forced, jnp.inf, jnp.where(valid, imp, -jnp.inf))
    return jnp.where(valid, _topk_member(v, cand, n_cand, k), 0.0)


def _nsa_prompt_kernel(slopes_ref, srow_ref, q_ref, gn_ref, ks_ref, vs_ref, kw_ref, vw_ref, ck_ref, cv_ref, cmat_ref,
                       o_ref, kx_ref, kwx_ref, vsb_ref, vwb_ref):
    qi = pl.program_id(1)
    chunk = TQ
    rows = NSA_R * TQ
    groups = range(NSA_G)

    @pl.when(qi == 0)
    def _():
        ext = _key_ext(SEQ, SEL_BLOCK)
        for g in groups:
            gsl = slice(g * HD, (g + 1) * HD)
            kx_ref[g, :, :HD] = ks_ref[:, gsl].astype(BF)
            kx_ref[g, :, HD:] = ext
            kwx_ref[g, :, :HD] = kw_ref[:, gsl].astype(BF)
            kwx_ref[g, :, HD:] = ext
            vsb_ref[g] = vs_ref[:, gsl].astype(BF)
            vwb_ref[g] = vw_ref[:, gsl].astype(BF)

    lane = lax.broadcasted_iota(jnp.int32, (TQ, HD), 1)
    t = qi * TQ + lax.broadcasted_iota(jnp.int32, (TQ, HD), 0)
    stack = lambda parts: jnp.concatenate(parts, axis=0)
    dist_c = stack([t - (lane * CMP_STRIDE + CMP_LEN - 1)] * NSA_R)
    cur_t = (qi * TQ + lax.broadcasted_iota(jnp.int32, (EXT_MASK, TQ), 1)) // SEL_BLOCK

    qx, qw, o_cmp = [], [], []
    for g in groups:
        q_all = stack([q_ref[:, (g * NSA_R + r) * HD:(g * NSA_R + r + 1) * HD] for r in range(NSA_R)])
        ck_hi, ck_lo = _split_bf16(ck_ref[0, g])
        slope_col = stack([jnp.full((TQ, 1), slopes_ref[g * NSA_R + r], F32) for r in range(NSA_R)])
        o_c, p = _cmp_attend(q_all, ck_hi, ck_lo, cv_ref[0, g].astype(BF), slope_col, dist_c)
        o_cmp.append(o_c)
        p_grp = p[0:TQ] + p[TQ:2 * TQ] + p[2 * TQ:3 * TQ]
        member = _select_blocks(p_grp, cmat_ref[...], cur_t, SEQ // SEL_BLOCK - 1, NSA_TOPN - 1)
        allowed = (_to_rows(member) > 0.5) | (lane == t // SEL_BLOCK)
        srow = [srow_ref[g * NSA_R + r:g * NSA_R + r + 1, :] for r in range(NSA_R)]
        q_bf = q_all.astype(BF)
        qx.append(jnp.concatenate([q_bf, stack([_query_ext(allowed, srow[r], lane) for r in range(NSA_R)])], axis=1))
        qw.append(jnp.concatenate([q_bf, stack([_query_ext(None, srow[r], lane) for r in range(NSA_R)])], axis=1))

    rr = lax.broadcasted_iota(jnp.int32, (rows, chunk), 0) % TQ
    cc = lax.broadcasted_iota(jnp.int32, (rows, chunk), 1)
    own = pl.ds(pl.multiple_of(qi * chunk, chunk), chunk)
    past = lambda c: pl.ds(pl.multiple_of(c * chunk, chunk), chunk)

    st = tuple(_flash_init(qx[g], kx_ref[g, own, :], vsb_ref[g, own, :], cc <= rr) for g in groups)
    st = lax.fori_loop(0, qi, lambda c, st: tuple(
        _flash_step(st[g], qx[g], kx_ref[g, past(c), :], vsb_ref[g, past(c), :]) for g in groups), st)
    o_sel = [st[g][2] / st[g][1] for g in groups]

    st = tuple(_flash_init(qw[g], kwx_ref[g, own, :], vwb_ref[g, own, :], cc <= rr) for g in groups)

    def win_step(c, st):
        slack = jnp.where(c == qi - WINDOW // chunk, 0, chunk)
        return tuple(_flash_step(st[g], qw[g], kwx_ref[g, past(c), :], vwb_ref[g, past(c), :], rr - cc <= slack)
                     for g in groups)

    st = lax.fori_loop(jnp.maximum(qi - WINDOW // chunk, 0), qi, win_step, st)
    o_win = [st[g][2] / st[g][1] for g in groups]

    gates = jax.nn.sigmoid(gn_ref[:, 0:NSA_HEADS * 3])
    for g in groups:
        for r in range(NSA_R):
            sl = slice(r * TQ, (r + 1) * TQ)
            hd = g * NSA_R + r
            o = (gates[:, 3 * hd:3 * hd + 1] * o_cmp[g][sl] + gates[:, 3 * hd + 1:3 * hd + 2] * o_sel[g][sl]
                 + gates[:, 3 * hd + 2:3 * hd + 3] * o_win[g][sl])
            o_ref[:, hd * HD:(hd + 1) * HD] = o.astype(BF)


def _nsa_prompt(slopes, slope_rows, qb, gn, ks, vs, kw, vw, ck, cv, cmat):
    nq = SEQ // TQ
    kv_spec = lambda: pl.BlockSpec((SEQ, KV_W), lambda b, qi: (b, 0))
    c_spec = lambda: pl.BlockSpec((1, NSA_G, N_CHUNK, HD), lambda b, qi: (b, 0, 0, 0))
    return pl.pallas_call(
        _nsa_prompt_kernel,
        grid=(BATCH, nq),
        in_specs=[pl.BlockSpec(memory_space=pltpu.SMEM),
                  pl.BlockSpec((8, HD), lambda b, qi: (0, 0)),
                  pl.BlockSpec((TQ, NSA_W), lambda b, qi: (b * nq + qi, 0)),
                  pl.BlockSpec((TQ, HD), lambda b, qi: (b * nq + qi, 0)),
                  kv_spec(), kv_spec(), kv_spec(), kv_spec(), c_spec(), c_spec(),
                  pl.BlockSpec((HD, HD), lambda b, qi: (0, 0))],
        out_specs=pl.BlockSpec((TQ, NSA_W), lambda b, qi: (b * nq + qi, 0)),
        out_shape=jax.ShapeDtypeStruct((BATCH * SEQ, NSA_W), BF),
        scratch_shapes=[pltpu.VMEM((NSA_G, SEQ, 2 * HD), BF), pltpu.VMEM((NSA_G, SEQ, 2 * HD), BF),
                        pltpu.VMEM((NSA_G, SEQ, HD), BF), pltpu.VMEM((NSA_G, SEQ, HD), BF)],
        compiler_params=_params(2),
        name="nsa_prompt",
    )(slopes, slope_rows, qb, gn, ks, vs, kw, vw, ck, cv, cmat)


def _mem_attend_kernel(q_ref, k_ref, v_ref, o_ref):
    for h in range(MEM_HEADS):
        sl = slice(h * HD, (h + 1) * HD)
        s = _dot_nt(q_ref[:, sl].astype(BF), k_ref[:, sl].astype(BF)) * SCALE
        m = jnp.max(s, axis=1, keepdims=True)
        p = jnp.exp(s - m)
        p = p / jnp.sum(p, axis=1, keepdims=True)
        o_ref[:, sl] = _dot(p.astype(BF), v_ref[:, sl].astype(BF)).astype(BF)


def _mem_attend_prompt(qm, mem_k, mem_v):
    tq = 512
    nq = SEQ // tq
    return pl.pallas_call(
        _mem_attend_kernel,
        grid=(BATCH, nq),
        in_specs=[pl.BlockSpec((tq, MEM_W), lambda b, qi: (b * nq + qi, 0)),
                  pl.BlockSpec((MEM_LEN, MEM_W), lambda b, qi: (b, 0)),
                  pl.BlockSpec((MEM_LEN, MEM_W), lambda b, qi: (b, 0))],
        out_specs=pl.BlockSpec((tq, MEM_W), lambda b, qi: (b * nq + qi, 0)),
        out_shape=jax.ShapeDtypeStruct((BATCH * SEQ, MEM_W), BF),
        compiler_params=_params(2),
        name="mem_attend_prompt",
    )(qm, mem_k, mem_v)


def _merge_kernel(x_ref, oa_ref, ob_ref, om_ref, wa_ref, wb_ref, wm_ref, ba_ref, bb_ref, bm_ref,
                  pa_ref, pb_ref, pm_ref, o_ref):
    x = x_ref[...]
    gate = lambda w_ref, b_ref: jax.nn.sigmoid(_dot_nt(x, w_ref[...]) + b_ref[...])
    merged = (gate(wa_ref, ba_ref) * _dot(oa_ref[...], pa_ref[...])
              + gate(wb_ref, bb_ref) * _dot(ob_ref[...], pb_ref[...])
              + gate(wm_ref, bm_ref) * _dot(om_ref[...], pm_ref[...]))
    o_ref[...] = merged.astype(BF)


def _merge(x_bf, oa, ob, om, w_gate, b_gate, pa, pb, pm, tm, tag):
    rows = oa.shape[0]
    tn = 512
    n_col = D_MODEL // tn
    row = lambda w: pl.BlockSpec((tm, w), lambda i, j: (i, 0))
    gate_w = lambda b: pl.BlockSpec((tn, D_MODEL), lambda i, j: (b * n_col + j, 0))
    gate_b = lambda b: pl.BlockSpec((1, tn), lambda i, j: (0, b * n_col + j))
    proj = lambda w: pl.BlockSpec((w, tn), lambda i, j: (0, j))
    return pl.pallas_call(
        _merge_kernel,
        grid=(rows // tm, n_col),
        in_specs=[row(D_MODEL), row(MOBA_W), row(NSA_W), row(MEM_W), gate_w(0), gate_w(1), gate_w(2),
                  gate_b(0), gate_b(1), gate_b(2), proj(MOBA_W), proj(NSA_W), proj(MEM_W)],
        out_specs=pl.BlockSpec((tm, tn), lambda i, j: (i, j)),
        out_shape=jax.ShapeDtypeStruct((rows, D_MODEL), BF),
        compiler_params=_params(2),
        name="branch_merge_" + tag,
    )(x_bf, oa, ob, om, w_gate, w_gate, w_gate, b_gate, b_gate, b_gate, pa, pb, pm)


def _layer_norm(x, g, b):
    mu = jnp.mean(x, axis=-1, keepdims=True)
    xc = x - mu
    var = jnp.mean(xc * xc, axis=-1, keepdims=True)
    return xc * lax.rsqrt(var + LN_EPS) * g + b


def _out_ln_kernel(x_ref, m_ref, wo_ref, g_ref, b_ref, h_ref, hb_ref):
    h = _layer_norm(DN_ALPHA * x_ref[...] + _dot(m_ref[...], wo_ref[...]), g_ref[...], b_ref[...])
    h_ref[...] = h
    hb_ref[...] = h.astype(BF)


def _out_ln(x, merged, wo, g, b, tm, tag):
    rows = x.shape[0]
    row = lambda: pl.BlockSpec((tm, D_MODEL), lambda i: (i, 0))
    vec = lambda: pl.BlockSpec((1, D_MODEL), lambda i: (0, 0))
    return pl.pallas_call(
        _out_ln_kernel,
        grid=(rows // tm,),
        in_specs=[row(), row(), pl.BlockSpec((D_MODEL, D_MODEL), lambda i: (0, 0)), vec(), vec()],
        out_specs=[row(), row()],
        out_shape=[jax.ShapeDtypeStruct((rows, D_MODEL), F32), jax.ShapeDtypeStruct((rows, D_MODEL), BF)],
        compiler_params=_params(1),
        name="out_proj_ln_" + tag,
    )(x, merged, wo, g, b)


TF = 512
N_TF = FFN_DIM // TF
HALO = 8
FFN_TM = 1024
FFN_VMEM_LIMIT = 58 * 1024 * 1024


def _ffn_kernel(hb_ref, halo_ref, h_ref, wu_ref, wg_ref, cw_ref, cb_ref, wd_ref, g_ref, b_ref, p2_ref, p1_ref,
                y_ref, u_ref, *, seq_rows):
    i = pl.program_id(0)
    f = pl.program_id(1)
    tm = hb_ref.shape[0]
    u = _dot(hb_ref[...], wu_ref[...])
    gate = _dot(hb_ref[...], wg_ref[...])
    if seq_rows:
        tiles_per_seq = seq_rows // tm
        live = jnp.where(i % tiles_per_seq == 0, 0.0, 1.0)
        u_halo = _dot(halo_ref[...], wu_ref[...]) * live
        row = lax.broadcasted_iota(jnp.int32, (HALO, TF), 0)
        prev1 = pltpu.roll(u, shift=1, axis=0)
        prev2 = pltpu.roll(u, shift=2, axis=0)
        head1 = jnp.where(row < 1, pltpu.roll(u_halo, shift=1, axis=0), prev1[:HALO])
        head2 = jnp.where(row < 2, pltpu.roll(u_halo, shift=2, axis=0), prev2[:HALO])
        prev1 = jnp.concatenate([head1, prev1[HALO:]], axis=0)
        prev2 = jnp.concatenate([head2, prev2[HALO:]], axis=0)
        u_ref[0] = u[tm - HALO:]
    else:
        prev2 = p2_ref[...]
        prev1 = p1_ref[...]
        u_ref[...] = u
    uc = cb_ref[...] + cw_ref[0:1, :] * prev2 + cw_ref[1:2, :] * prev1 + cw_ref[2:3, :] * u
    act = (_gelu_tanh(uc) * gate).astype(BF)

    @pl.when(f == 0)
    def _():
        y_ref[...] = jnp.zeros(y_ref.shape, F32)

    y_ref[...] += _dot(act, wd_ref[...])

    @pl.when(f == N_TF - 1)
    def _():
        y_ref[...] = _layer_norm(DN_ALPHA * h_ref[...] + y_ref[...], g_ref[...], b_ref[...])


def _ffn(h, h_bf, wu_bf, cw, cb, wd_bf, g, b, prev2, prev1, tm, seq_rows, tag):
    rows = h.shape[0]
    n_halo = tm // HALO
    if seq_rows:
        prev2 = prev1 = jnp.zeros((HALO, TF), F32)
        cs_spec = pl.BlockSpec((HALO, TF), lambda i, f: (0, 0))
        halo_spec = pl.BlockSpec((HALO, D_MODEL), lambda i, f: (jnp.maximum(i * n_halo - 1, 0), 0))
        u_shape = jax.ShapeDtypeStruct((rows // tm, HALO, FFN_DIM), F32)
        u_spec = pl.BlockSpec((1, HALO, TF), lambda i, f: (i, 0, f))
    else:
        cs_spec = pl.BlockSpec((tm, TF), lambda i, f: (i, f))
        halo_spec = pl.BlockSpec((HALO, D_MODEL), lambda i, f: (0, 0))
        u_shape = jax.ShapeDtypeStruct((rows, FFN_DIM), F32)
        u_spec = pl.BlockSpec((tm, TF), lambda i, f: (i, f))
    row = lambda: pl.BlockSpec((tm, D_MODEL), lambda i, f: (i, 0))
    row_once = lambda: pl.BlockSpec((tm, D_MODEL), lambda i, f: (i, 0), pipeline_mode=pl.Buffered(1))
    vec = lambda: pl.BlockSpec((1, D_MODEL), lambda i, f: (0, 0))
    return pl.pallas_call(
        functools.partial(_ffn_kernel, seq_rows=seq_rows),
        grid=(rows // tm, N_TF),
        in_specs=[row_once(), halo_spec, row_once(),
                  pl.BlockSpec((D_MODEL, TF), lambda i, f: (0, f)),
                  pl.BlockSpec((D_MODEL, TF), lambda i, f: (0, N_TF + f)),
                  pl.BlockSpec((3, TF), lambda i, f: (0, f)),
                  pl.BlockSpec((1, TF), lambda i, f: (0, f)),
                  pl.BlockSpec((TF, D_MODEL), lambda i, f: (f, 0)),
                  vec(), vec(), cs_spec, cs_spec],
        out_specs=[row(), u_spec],
        out_shape=[jax.ShapeDtypeStruct((rows, D_MODEL), F32), u_shape],
        compiler_params=pltpu.CompilerParams(dimension_semantics=("arbitrary", "arbitrary"),
                                             vmem_limit_bytes=FFN_VMEM_LIMIT),
        name="conv_ffn_" + tag,
    )(h_bf, h_bf, h, wu_bf, wu_bf, cw, cb, wd_bf, g, b, prev2, prev1)


def _rowdot(mat, vec):
    return jnp.sum(mat * vec, axis=1, keepdims=True)


MOBA_NB = PAST_LEN // MOBA_BLOCK
MOBA_PPB = MOBA_BLOCK // PAGE


def _moba_sample_probs_kernel(pt_ref, slopes_ref, q_ref, kn_ref, *refs):
    del pt_ref
    k_pages = refs[:N_PAGES]
    p_ref, stat_ref, ids_ref = refs[N_PAGES:N_PAGES + 3]
    nb, ppb = MOBA_NB, MOBA_PPB
    pos = lax.broadcasted_iota(jnp.int32, (1, PAST_LEN), 1)
    dist = (PAST_LEN - pos).astype(F32)
    blk = pos // MOBA_BLOCK
    row8 = lax.broadcasted_iota(jnp.int32, (8, HD), 0)
    lane8 = lax.broadcasted_iota(jnp.int32, (8, HD), 1)
    stats = jnp.zeros((8, HD), F32)
    ids = jnp.zeros((8, HD), F32)
    for h in range(MOBA_HEADS):
        sl = slice(h * HD, (h + 1) * HD)
        q = q_ref[0, :, sl]
        k_pg = [p[h] for p in k_pages]
        km = [sum(jnp.sum(k_pg[j * ppb + i], axis=0, keepdims=True) for i in range(ppb)) / MOBA_BLOCK
              for j in range(nb)]
        sc = [_rowdot(kmj, q) for kmj in km]
        allowed = jnp.zeros((1, PAST_LEN), F32)
        for j in range(nb):
            beats = [(sc[i] > sc[j]) | ((sc[i] == sc[j]) & (i < j)) for i in range(nb) if i != j]
            rank = sum(b.astype(F32) for b in beats)
            allowed = jnp.where(blk == j, jnp.where(rank < MOBA_TOPK, 1.0, 0.0), allowed)
            for c in range(MOBA_TOPK):
                ids = jnp.where((row8 == h) & (lane8 == c) & (rank == c), float(j), ids)
        q8 = jnp.concatenate([q, jnp.zeros((7, HD), F32)], axis=0)
        k_bf = jnp.concatenate([kp.astype(BF) for kp in k_pg], axis=0)
        s = _dot_nt(q8.astype(BF), k_bf) * SCALE - slopes_ref[h] * dist
        s = jnp.where(allowed > 0.5, s, -jnp.inf)
        s_own = _rowdot(q8, kn_ref[0, :, sl]) * SCALE
        m = jnp.maximum(jnp.max(s, axis=1, keepdims=True), s_own)
        p = jnp.exp(s - m)
        p_own = jnp.exp(s_own - m)
        den = jnp.sum(p, axis=1, keepdims=True) + p_own
        for j in range(nb):
            p_ref[0, h * nb + j:h * nb + j + 1, :] = p[0:1, j * MOBA_BLOCK:(j + 1) * MOBA_BLOCK]
        stats = jnp.where((row8 == h) & (lane8 == 0), p_own[0:1], jnp.where((row8 == h) & (lane8 == 1), den[0:1], stats))
    stat_ref[0] = stats
    ids_ref[0] = ids.astype(jnp.int32)


def _moba_sample_probs(page_table, slopes, q, k_new, k_pool):
    row = lambda: pl.BlockSpec((1, 1, MOBA_W), lambda n, pt: (n, 0, 0))
    page = lambda p: pl.BlockSpec((None, MOBA_HEADS, PAGE, HD), lambda n, pt: (pt[n, p], 0, 0, 0))
    tile = lambda r, w: pl.BlockSpec((1, r, w), lambda n, pt: (n, 0, 0))
    n_rows = MOBA_HEADS * MOBA_NB
    return pl.pallas_call(
        _moba_sample_probs_kernel,
        grid_spec=pltpu.PrefetchScalarGridSpec(
            num_scalar_prefetch=1, grid=(DEC_BATCH,),
            in_specs=[pl.BlockSpec(memory_space=pltpu.SMEM), row(), row()] + [page(p) for p in range(N_PAGES)],
            out_specs=[tile(n_rows, MOBA_BLOCK), tile(8, HD), tile(8, HD)]),
        out_shape=[jax.ShapeDtypeStruct((DEC_BATCH, n_rows, MOBA_BLOCK), F32),
                   jax.ShapeDtypeStruct((DEC_BATCH, 8, HD), F32), jax.ShapeDtypeStruct((DEC_BATCH, 8, HD), jnp.int32)],
        compiler_params=_params(1),
        name="moba_sample_probs",
    )(page_table, slopes, q, k_new, *([k_pool] * N_PAGES))


def _moba_sample_values_kernel(pt_ref, ids_ref, p_ref, stat_ref, vn_ref, v_hbm, o_ref, vbuf, sem):
    n = pl.program_id(0)

    def block_copies(sample, slot):
        copies = []
        for h in range(MOBA_HEADS):
            for c in range(MOBA_TOPK):
                j = ids_ref[(sample * MOBA_HEADS + h) * MOBA_TOPK + c]
                for i in range(MOBA_PPB):
                    dst = pl.ds(((h * MOBA_TOPK + c) * MOBA_PPB + i) * PAGE, PAGE)
                    copies.append(pltpu.make_async_copy(v_hbm.at[pt_ref[sample, j * MOBA_PPB + i], h],
                                                        vbuf.at[slot, dst, :], sem.at[slot]))
        return copies

    @pl.when(n == 0)
    def _():
        for cp in block_copies(0, 0):
            cp.start()

    @pl.when(n + 1 < DEC_BATCH)
    def _():
        for cp in block_copies(n + 1, (n + 1) % 2):
            cp.start()

    slot = n % 2
    for cp in block_copies(n, slot):
        cp.wait()

    for h in range(MOBA_HEADS):
        acc = jnp.zeros((8, HD), F32)
        for c in range(MOBA_TOPK):
            j = ids_ref[(n * MOBA_HEADS + h) * MOBA_TOPK + c]
            p_c = p_ref[0, pl.ds(h * MOBA_NB + j, 1), :]
            p8 = jnp.concatenate([p_c, jnp.zeros((7, MOBA_BLOCK), F32)], axis=0)
            v_c = vbuf[slot, (h * MOBA_TOPK + c) * MOBA_BLOCK:(h * MOBA_TOPK + c + 1) * MOBA_BLOCK, :]
            acc = acc + _dot(p8.astype(BF), v_c.astype(BF))
        p_own = stat_ref[0, h:h + 1, 0:1]
        den = stat_ref[0, h:h + 1, 1:2]
        o_ref[0, :, h * HD:(h + 1) * HD] = (acc[0:1] + p_own * vn_ref[0, :, h * HD:(h + 1) * HD]) / den


def _moba_sample_values(page_table, sel_ids, p, stats, v_new, v_pool):
    tile = lambda r, w: pl.BlockSpec((1, r, w), lambda n, pt, ids: (n, 0, 0))
    return pl.pallas_call(
        _moba_sample_values_kernel,
        grid_spec=pltpu.PrefetchScalarGridSpec(
            num_scalar_prefetch=2, grid=(DEC_BATCH,),
            in_specs=[tile(MOBA_HEADS * MOBA_NB, MOBA_BLOCK), tile(8, HD), tile(1, MOBA_W),
                      pl.BlockSpec(memory_space=pl.ANY)],
            out_specs=tile(1, MOBA_W),
            scratch_shapes=[pltpu.VMEM((2, MOBA_HEADS * MOBA_TOPK * MOBA_BLOCK, HD), F32),
                            pltpu.SemaphoreType.DMA((2,))]),
        out_shape=jax.ShapeDtypeStruct((DEC_BATCH, 1, MOBA_W), F32),
        compiler_params=_params(1),
        name="moba_sample_values",
    )(page_table, sel_ids, p, stats, v_new, v_pool)


NS = 8
ROWS_PER_PAGE = PAGE * NSA_G
N_SEL = NSA_TOPN - 1
N_CAND_S = PAST_LEN // SEL_BLOCK


def _nsa_sample_cmp_kernel(pt_ref, slopes_ref, kcn_ref, vcn_ref, q_ref, pe_ref, w1_ref, w2_ref, cmat_ref, tri_ref,
                           *refs):
    del pt_ref
    k_pages, v_pages = refs[:N_PAGES], refs[N_PAGES:2 * N_PAGES]
    ocmp_ref, idx_ref, stage_ref = refs[2 * N_PAGES:2 * N_PAGES + 3]
    s_in = pl.program_id(1)
    unit = N_CHUNK
    chunks_per_page = PAGE // CMP_STRIDE

    rg = CMP_STRIDE * NSA_G
    for which, pages in enumerate((k_pages, v_pages)):
        for pair in range(N_PAGES // 2):
            slabs = [jnp.swapaxes(pages[2 * pair + i][...].reshape(chunks_per_page, rg, HD), 0, 1) for i in range(2)]
            for g in range(NSA_G):
                base = pl.multiple_of(s_in * (NSA_G * unit) + g * unit, unit) + pair * 2 * chunks_per_page
                for r in range(CMP_STRIDE):
                    x = jnp.concatenate([slabs[0][NSA_G * r + g], slabs[1][NSA_G * r + g]], axis=0)
                    stage_ref[which, pl.ds(base, 2 * chunks_per_page), r * HD:(r + 1) * HD] = x.astype(BF)

    @pl.when(s_in == NS - 1)
    def _():
        n_unit = NS * NSA_G
        rows = n_unit * unit
        row = lax.broadcasted_iota(jnp.int32, (rows, HD), 0)
        toks = []
        for which, new_ref in enumerate((kcn_ref, vcn_ref)):
            w1 = w1_ref.at[which]
            acc = _dot(stage_ref[which], w1[...])
            top, bot = acc[:, :HD], acc[:, HD:]
            nxt = pltpu.roll(bot, shift=rows - 1, axis=0)
            new_rows = jnp.concatenate([new_ref[s:s + 1, g * HD:(g + 1) * HD]
                                        for s in range(NS) for g in range(NSA_G)], axis=0)
            b_new = _dot(new_rows.astype(BF), w1[0:HD, HD:])
            b_last = jnp.concatenate([jnp.broadcast_to(b_new[u:u + 1], (unit, HD)) for u in range(n_unit)], axis=0)
            c_row = jnp.zeros((1, HD), F32)
            for r in range(CMP_LEN):
                half, rr = divmod(r, CMP_STRIDE)
                c_row = c_row + _dot(pe_ref[which, :, r * HD:(r + 1) * HD],
                                     w1[rr * HD:(rr + 1) * HD, half * HD:(half + 1) * HD])
            pre = top + jnp.where(row % unit == unit - 1, b_last, nxt) + c_row
            toks.append(_dot(_gelu_tanh(pre).astype(BF), w2_ref[which]))
        ck_all, cv_all = toks

        lane = lax.broadcasted_iota(jnp.int32, (8, HD), 1)
        row8 = lax.broadcasted_iota(jnp.int32, (8, HD), 0)
        dist = PAST_LEN - (lane * CMP_STRIDE + CMP_LEN - 1)
        p_rows = []
        for s in range(NS):
            for g in range(NSA_G):
                u = s * NSA_G + g
                ck_hi, ck_lo = _split_bf16(ck_all[u * unit:(u + 1) * unit])
                cv_bf = cv_all[u * unit:(u + 1) * unit].astype(BF)
                q8 = jnp.concatenate([q_ref[s:s + 1, (g * NSA_R + r) * HD:(g * NSA_R + r + 1) * HD]
                                      for r in range(NSA_R)] + [jnp.zeros((8 - NSA_R, HD), F32)], axis=0)
                slope = jnp.where(row8[:, 0:1] == 0, slopes_ref[g * NSA_R],
                                  jnp.where(row8[:, 0:1] == 1, slopes_ref[g * NSA_R + 1], slopes_ref[g * NSA_R + 2]))
                o8, p8 = _cmp_attend(q8, ck_hi, ck_lo, cv_bf, slope, dist)
                ocmp_ref[s, g] = o8
                p_rows.append(jnp.sum(jnp.where(row8 < NSA_R, p8, 0.0), axis=0, keepdims=True))
        p_grp = jnp.concatenate(p_rows + [jnp.zeros((HD - n_unit, HD), F32)], axis=0)
        member = _select_blocks(p_grp, cmat_ref[...], jnp.full((EXT_MASK, HD), N_CAND_S, jnp.int32), N_CAND_S, N_SEL)
        member_pad = jnp.concatenate([member, jnp.zeros((HD - EXT_MASK, HD), F32)], axis=0)
        rank = _dot(tri_ref[...], member_pad.astype(BF))[:EXT_MASK]
        cand = lax.broadcasted_iota(jnp.int32, (EXT_MASK, HD), 0).astype(F32)
        picks = [jnp.sum(jnp.where((member > 0.5) & (rank == c), cand, 0.0), axis=0, keepdims=True)
                 for c in range(N_SEL)]
        idx_ref[0] = jnp.concatenate(picks + [jnp.zeros((1, HD), F32)], axis=0).astype(jnp.int32)


def _nsa_sample_cmp(page_table, slopes, kc_new, vc_new, q, pe_bf, w1_flat, w2_bf, cmat, tri, ck_pool, cv_pool):
    page = lambda p: pl.BlockSpec((ROWS_PER_PAGE, HD), lambda i, s, pt: (pt[i * NS + s, p], 0))
    full = lambda a: pl.BlockSpec(a.shape, lambda i, s, pt: (0,) * a.ndim)
    rows = lambda w: pl.BlockSpec((NS, w), lambda i, s, pt: (i, 0))
    out = lambda: pl.BlockSpec((NS, NSA_G, 8, HD), lambda i, s, pt: (i, 0, 0, 0))
    return pl.pallas_call(
        _nsa_sample_cmp_kernel,
        grid_spec=pltpu.PrefetchScalarGridSpec(
            num_scalar_prefetch=1, grid=(DEC_BATCH // NS, NS),
            in_specs=[pl.BlockSpec(memory_space=pltpu.SMEM), rows(KV_W), rows(KV_W), rows(NSA_W),
                      full(pe_bf), full(w1_flat), full(w2_bf), full(cmat), full(tri)]
            + [page(p) for p in range(N_PAGES)] * 2,
            out_specs=[out(), pl.BlockSpec((1, N_SEL + 1, HD), lambda i, s, pt: (i, 0, 0))],
            scratch_shapes=[pltpu.VMEM((2, NS * NSA_G * N_CHUNK, CMP_STRIDE * HD), BF)]),
        out_shape=[jax.ShapeDtypeStruct((DEC_BATCH, NSA_G, 8, HD), F32),
                   jax.ShapeDtypeStruct((DEC_BATCH // NS, N_SEL + 1, HD), jnp.int32)],
        compiler_params=_params(2),
        name="nsa_sample_cmp",
    )(page_table, slopes, kc_new, vc_new, q, pe_bf, w1_flat, w2_bf, cmat, tri,
      *([ck_pool] * N_PAGES), *([cv_pool] * N_PAGES))


def _attend_rows(q8, slope, k, v, dist, k_own, v_own):
    s = _dot_nt(q8.astype(BF), k.astype(BF)) * SCALE - slope * dist
    s_own = _rowdot(q8, k_own) * SCALE
    m = jnp.maximum(jnp.max(s, axis=1, keepdims=True), s_own)
    p = jnp.exp(s - m)
    p_own = jnp.exp(s_own - m)
    den = jnp.sum(p, axis=1, keepdims=True) + p_own
    return (_dot(p.astype(BF), v.astype(BF)) + p_own * v_own) / den


BLK_ROWS = SEL_BLOCK * NSA_G


def _nsa_sample_attend_kernel(pt_ref, ix_ref, slopes_ref, q_ref, gn_ref, ksn_ref, vsn_ref, kwn_ref, vwn_ref,
                              ocmp_ref, wk_ref, wv_ref, sk_hbm, sv_hbm, o_ref, wko_ref, wvo_ref, kbuf, vbuf, sem):
    n = pl.program_id(0)
    halves = PAGE // SEL_BLOCK

    def block_copies(sample, slot):
        copies = []
        for g in range(NSA_G):
            for c in range(N_SEL):
                j = ix_ref[(sample * NSA_G + g) * N_SEL + c]
                row0 = pl.multiple_of((pt_ref[sample, j // halves] * halves + j % halves) * BLK_ROWS, BLK_ROWS)
                dst = pl.ds((g * N_SEL + c) * BLK_ROWS, BLK_ROWS)
                for pool, buf in ((sk_hbm, kbuf), (sv_hbm, vbuf)):
                    copies.append(pltpu.make_async_copy(pool.at[pl.ds(row0, BLK_ROWS), :], buf.at[slot, dst, :],
                                                        sem.at[slot]))
        return copies

    @pl.when(n == 0)
    def _():
        for cp in block_copies(0, 0):
            cp.start()

    @pl.when(n + 1 < DEC_BATCH)
    def _():
        for cp in block_copies(n + 1, (n + 1) % 2):
            cp.start()

    slot = n % 2
    for cp in block_copies(n, slot):
        cp.wait()

    wl = wk_ref.shape[0] // NSA_G
    off = lax.broadcasted_iota(jnp.int32, (1, SEL_BLOCK), 1)
    dist_w = (wl - lax.broadcasted_iota(jnp.int32, (1, wl), 1)).astype(F32)
    row8 = lax.broadcasted_iota(jnp.int32, (8, 1), 0)
    gates = jax.nn.sigmoid(gn_ref[0])
    for g in range(NSA_G):
        gsl = slice(g * HD, (g + 1) * HD)
        heads = [g * NSA_R + r for r in range(NSA_R)]
        q8 = jnp.concatenate([q_ref[0, :, hd * HD:(hd + 1) * HD] for hd in heads]
                             + [jnp.zeros((8 - NSA_R, HD), F32)], axis=0)
        slope = jnp.where(row8 == 0, slopes_ref[heads[0]],
                          jnp.where(row8 == 1, slopes_ref[heads[1]], slopes_ref[heads[2]]))
        sel_rows = pl.ds(g * N_SEL * BLK_ROWS + g, N_SEL * SEL_BLOCK, stride=NSA_G)
        pos = jnp.concatenate([ix_ref[(n * NSA_G + g) * N_SEL + c] * SEL_BLOCK + off for c in range(N_SEL)], axis=1)
        o_sel = _attend_rows(q8, slope, kbuf[slot, sel_rows, :], vbuf[slot, sel_rows, :],
                             (PAST_LEN - pos).astype(F32), ksn_ref[0, :, gsl], vsn_ref[0, :, gsl])
        o_win = _attend_rows(q8, slope, wk_ref[pl.ds(g, wl, stride=NSA_G), :], wv_ref[pl.ds(g, wl, stride=NSA_G), :],
                             dist_w, kwn_ref[0, :, gsl], vwn_ref[0, :, gsl])
        for r, hd in enumerate(heads):
            o_ref[0, :, hd * HD:(hd + 1) * HD] = (
                gates[:, 3 * hd:3 * hd + 1] * ocmp_ref[0, g, r:r + 1, :]
                + gates[:, 3 * hd + 1:3 * hd + 2] * o_sel[r:r + 1] + gates[:, 3 * hd + 2:3 * hd + 3] * o_win[r:r + 1])

    for cache, new, out in ((wk_ref, kwn_ref, wko_ref), (wv_ref, vwn_ref, wvo_ref)):
        out[...] = pltpu.roll(cache[...], shift=(wl - 1) * NSA_G, axis=0)
        for g in range(NSA_G):
            out[pl.ds((wl - 1) * NSA_G + g, 1), :] = new[0, :, g * HD:(g + 1) * HD]


def _nsa_sample_attend(page_table, sel_idx, slopes, q, gn, ks_new, vs_new, kw_new, vw_new, o_cmp,
                       win_k, win_v, sk_pool, sv_pool):
    row = lambda w: pl.BlockSpec((1, 1, w), lambda n, pt, ix: (n, 0, 0))
    win_rows = win_k.shape[0] // DEC_BATCH
    win = lambda: pl.BlockSpec((win_rows, HD), lambda n, pt, ix: (n, 0))
    any_spec = pl.BlockSpec(memory_space=pl.ANY)
    buf = pltpu.VMEM((2, NSA_G * N_SEL * BLK_ROWS, HD), F32)
    return pl.pallas_call(
        _nsa_sample_attend_kernel,
        grid_spec=pltpu.PrefetchScalarGridSpec(
            num_scalar_prefetch=2, grid=(DEC_BATCH,),
            in_specs=[pl.BlockSpec(memory_space=pltpu.SMEM), row(NSA_W), row(HD), row(KV_W), row(KV_W), row(KV_W),
                      row(KV_W), pl.BlockSpec((1, NSA_G, 8, HD), lambda n, pt, ix: (n, 0, 0, 0)), win(), win(),
                      any_spec, any_spec],
            out_specs=[row(NSA_W), win(), win()],
            scratch_shapes=[buf, buf, pltpu.SemaphoreType.DMA((2,))]),
        out_shape=[jax.ShapeDtypeStruct((DEC_BATCH, 1, NSA_W), F32), jax.ShapeDtypeStruct(win_k.shape, F32),
                   jax.ShapeDtypeStruct(win_v.shape, F32)],
        compiler_params=_params(1),
        name="nsa_sample_attend",
    )(page_table, sel_idx, slopes, q, gn, ks_new, vs_new, kw_new, vw_new, o_cmp, win_k, win_v, sk_pool, sv_pool)


MEM_RING = 3
MEM_ROWS = MEM_LEN * MEM_HEADS


def _mem_sample_kernel(q_ref, k_hbm, v_hbm, o_ref, kbuf, vbuf, sem):
    n = pl.program_id(0)

    def copies(sample, slot):
        src = pl.ds(pl.multiple_of(sample * MEM_ROWS, MEM_ROWS), MEM_ROWS)
        return [pltpu.make_async_copy(k_hbm.at[src, :], kbuf.at[slot], sem.at[0, slot]),
                pltpu.make_async_copy(v_hbm.at[src, :], vbuf.at[slot], sem.at[1, slot])]

    @pl.when(n == 0)
    def _():
        for ahead in range(MEM_RING - 1):
            for cp in copies(ahead, ahead):
                cp.start()

    @pl.when(n + MEM_RING - 1 < DEC_BATCH)
    def _():
        for cp in copies(n + MEM_RING - 1, (n + MEM_RING - 1) % MEM_RING):
            cp.start()

    slot = n % MEM_RING
    for cp in copies(n, slot):
        cp.wait()

    for h in range(MEM_HEADS):
        sl = slice(h * HD, (h + 1) * HD)
        q = q_ref[0, :, sl]
        k = kbuf[slot, pl.ds(h, MEM_LEN, stride=MEM_HEADS), :]
        v = vbuf[slot, pl.ds(h, MEM_LEN, stride=MEM_HEADS), :]
        s = _rowdot(k, q) * SCALE
        p = jnp.exp(s - jnp.max(s, axis=0, keepdims=True))
        o_ref[0, :, sl] = jnp.sum(p * v, axis=0, keepdims=True) / jnp.sum(p, axis=0, keepdims=True)


def _mem_sample(q, mem_k, mem_v):
    row = lambda: pl.BlockSpec((1, 1, MEM_W), lambda n: (n, 0, 0))
    any_spec = pl.BlockSpec(memory_space=pl.ANY)
    buf = pltpu.VMEM((MEM_RING, MEM_ROWS, HD), F32)
    return pl.pallas_call(
        _mem_sample_kernel,
        grid=(DEC_BATCH,),
        in_specs=[row(), any_spec, any_spec],
        out_specs=row(),
        out_shape=jax.ShapeDtypeStruct((DEC_BATCH, 1, MEM_W), F32),
        scratch_shapes=[buf, buf, pltpu.SemaphoreType.DMA((2, MEM_RING))],
        compiler_params=_params(1),
        name="mem_attend_sample",
    )(q, mem_k, mem_v)


def _project_all(x_bf, w, tm, paged, tag):
    kv_kind = "pages" if paged else "flat"
    qa, ka, va, qb = _proj(x_bf, w["wa"], w["ba"], MOBA_W, ["flat", kv_kind, kv_kind, "flat"], tm,
                           "proj_moba_q_" + tag, True)
    kc, vc, ks, vs, kw, vw = _proj(x_bf, w["wb"], w["bb"], KV_W, ["flat"] * 6, tm, "proj_nsa_kv_" + tag, True)
    (qm,) = _proj(x_bf, w["wm"], w["bm"], MEM_W, ["flat"], tm, "proj_mem_q_" + tag, True)
    (gn,) = _proj(x_bf, w["wg"], w["bg"], HD, ["flat"], tm, "proj_nsa_gate_" + tag, True)
    return qa, ka, va, qb, kc, vc, ks, vs, kw, vw, qm, gn


def _strict_lower_ones():
    return jnp.asarray(np.tril(np.ones((HD, HD), np.float32), -1), BF)


def kernel(x_prompt, x_sample, cache_mem_k, cache_mem_v, cache_moba_k, cache_moba_v, cache_nsa_cmp_k, cache_nsa_cmp_v, cache_nsa_sel_k, cache_nsa_sel_v, cache_nsa_win_k, cache_nsa_win_v, cache_ffn_conv, page_table, mem_prompt, w_in, b_in, w_mem_kv, cmp_pe, cmp_w1, cmp_w2, p_moba, p_nsa, p_mem, w_o, ln1_g, ln1_b, w_up, conv_w, conv_b, w_down, ln2_g, ln2_b):
    slopes = _alibi_slopes(MOBA_HEADS)
    w_bf = w_in[0].T.astype(BF)
    b0 = b_in[0][None, :]
    c0, c1, c2, c3 = 4 * MOBA_W, 4 * MOBA_W + 6 * KV_W, 4 * MOBA_W + 6 * KV_W + N_GATE, 4 * MOBA_W + 6 * KV_W + N_GATE + MEM_W
    w = {"wa": w_bf[:c0], "ba": b0[:, :c0], "wb": w_bf[c0:c1], "bb": b0[:, c0:c1],
         "wg": jnp.pad(w_bf[c1:c2], ((0, HD - N_GATE), (0, 0))), "bg": jnp.pad(b0[:, c1:c2], ((0, 0), (0, HD - N_GATE))),
         "wm": w_bf[c2:c3], "bm": b0[:, c2:c3], "wd": w_bf[c3:], "bd": b0[:, c3:]}
    pe_bf, w1_bf, w2_bf = _pack_cmp_weights(cmp_pe[0], cmp_w1[0], cmp_w2[0])
    pa_bf, pb_bf, pm_bf, wo_bf = p_moba[0].astype(BF), p_nsa[0].astype(BF), p_mem[0].astype(BF), w_o[0].astype(BF)
    wu_bf, wd_bf = w_up[0].astype(BF), w_down[0].astype(BF)

    xp = x_prompt.reshape(BATCH * SEQ, D_MODEL)
    xp_bf = xp.astype(BF)
    qa, ka, va, qb, kc, vc, ks, vs, kw, vw, qm, gn = _project_all(xp_bf, w, 1024, True, "prompt")
    mem_k, mem_v = _proj(mem_prompt.reshape(BATCH * MEM_LEN, D_MODEL).astype(BF), w_mem_kv[0].astype(BF),
                         jnp.zeros((1, 2 * MEM_W), F32), MEM_W, ["flat", "flat"], 512, "proj_mem_kv")
    slope_rows = _slope_rows(MOBA_HEADS)
    o_a = _moba_prompt(slope_rows, qa, ka, va)
    ck, cv = _compress_prompt(kc, vc, pe_bf, w1_bf, w2_bf)
    cmat = _cmp_to_sel_matrix(N_CHUNK - 1, SEQ // SEL_BLOCK - 1)
    o_b = _nsa_prompt(slopes, slope_rows, qb, gn, ks, vs, kw, vw, ck, cv, cmat)
    o_m = _mem_attend_prompt(qm, mem_k, mem_v)
    merged = _merge(xp_bf, o_a, o_b, o_m, w["wd"], w["bd"], pa_bf, pb_bf, pm_bf, 1024, "prompt")
    h, h_bf = _out_ln(xp, merged, wo_bf, ln1_g, ln1_b, 512, "prompt")
    y_p, u_tail = _ffn(h, h_bf, wu_bf, conv_w[0], conv_b, wd_bf, ln2_g, ln2_b, None, None, FFN_TM, SEQ, "prompt")

    pages = lambda a: a.reshape(1, BATCH, SEQ // PAGE, PAGE, NSA_G, HD)
    head_pages = lambda a: a.transpose(0, 1, 3, 2, 4)[None]
    last = lambda a: a.reshape(BATCH, SEQ, NSA_G, HD)[None, :, SEQ - WINDOW:]
    tiles_per_seq = SEQ // FFN_TM
    conv_p = u_tail.reshape(BATCH, tiles_per_seq, HALO, FFN_DIM)[:, -1, HALO - 2:][None]
    prompt_out = (y_p.reshape(BATCH, SEQ, D_MODEL),
                  mem_k.reshape(1, BATCH, MEM_LEN, MEM_HEADS, HD), mem_v.reshape(1, BATCH, MEM_LEN, MEM_HEADS, HD),
                  head_pages(ka), head_pages(va), pages(kc), pages(vc), pages(ks), pages(vs), last(kw), last(vw), conv_p)

    n = DEC_BATCH
    xs = x_sample.reshape(n, D_MODEL)
    xs_bf = xs.astype(BF)
    qa, ka, va, qb, kc, vc, ks, vs, kw, vw, qm, gn = _project_all(xs_bf, w, n, False, "sample")
    r3 = lambda a: a.reshape(n, 1, a.shape[-1])
    moba_pool = lambda c: c[0].transpose(0, 2, 1, 3)
    rows2d = lambda c: c.reshape(-1, HD)
    p_a, stat_a, ids_a = _moba_sample_probs(page_table, slopes, r3(qa), r3(ka), moba_pool(cache_moba_k))
    o_a = _moba_sample_values(page_table, ids_a[:, :MOBA_HEADS, :MOBA_TOPK].reshape(-1), p_a, stat_a, r3(va),
                              moba_pool(cache_moba_v))
    w1_flat = w1_bf.reshape(2, CMP_STRIDE * HD, 2 * HD)
    cmat_s = _cmp_to_sel_matrix(N_CHUNK, N_CAND_S)
    o_cmp, sel = _nsa_sample_cmp(page_table, slopes, kc, vc, qb, pe_bf, w1_flat, w2_bf, cmat_s, _strict_lower_ones(),
                                 rows2d(cache_nsa_cmp_k), rows2d(cache_nsa_cmp_v))
    sel_idx = sel[:, :N_SEL, :NS * NSA_G].transpose(0, 2, 1).reshape(-1)
    o_b, win_k_new, win_v_new = _nsa_sample_attend(
        page_table, sel_idx, slopes, r3(qb), r3(gn), r3(ks), r3(vs), r3(kw), r3(vw), o_cmp,
        rows2d(cache_nsa_win_k), rows2d(cache_nsa_win_v), rows2d(cache_nsa_sel_k), rows2d(cache_nsa_sel_v))
    o_m = _mem_sample(r3(qm), rows2d(cache_mem_k), rows2d(cache_mem_v))
    flat_bf = lambda a: a.reshape(n, a.shape[-1]).astype(BF)
    merged = _merge(xs_bf, flat_bf(o_a), flat_bf(o_b), flat_bf(o_m), w["wd"], w["bd"], pa_bf, pb_bf, pm_bf, n,
                    "sample")
    h, h_bf = _out_ln(xs, merged, wo_bf, ln1_g, ln1_b, n, "sample")
    conv_old = cache_ffn_conv[0]
    y_s, u_new = _ffn(h, h_bf, wu_bf, conv_w[0], conv_b, wd_bf, ln2_g, ln2_b, conv_old[:, 0], conv_old[:, 1], n, 0,
                      "sample")

    heads = lambda a, nh: a.reshape(1, n, 1, nh, HD)
    sample_out = (heads(ka, MOBA_HEADS), heads(va, MOBA_HEADS), heads(kc, NSA_G), heads(vc, NSA_G),
                  heads(ks, NSA_G), heads(vs, NSA_G), win_k_new.reshape(cache_nsa_win_k.shape),
                  win_v_new.reshape(cache_nsa_win_v.shape),
                  jnp.stack([conv_old[:, 1], u_new], axis=1)[None])
    return (prompt_out[0], y_s.reshape(n, 1, D_MODEL)) + prompt_out[1:] + sample_out
```
